```python
import jax, jax.numpy as jnp
from jax import lax
import numpy as np

D_MODEL = 2048
BATCH = 8
SEQ = 2048
DEPTH = 4
DEC_BATCH = 32
DEC_SEQ = 32
PAST_LEN = 1024

CHUNK = 64
D_A = D_MODEL // 2
D_B = D_MODEL // 2
CONV_W = 3
GMLP_CHUNK = 128
GMLP_GROUPS = 8
GMLP_GD = D_B // GMLP_GROUPS
SB_HEADS = 16
SB_HEAD_DIM = D_MODEL // SB_HEADS
QBLOCK = 128
D_FF = ((8 * D_MODEL // 3 + 255) // 256) * 256
EPS = 1e-6

kernel_name = 'hybrid_streaming_encoder_step'


def rms_norm(x, g):
    x32 = x.astype(jnp.float32)
    y = x32 * lax.rsqrt(jnp.mean(x32 * x32, axis=-1, keepdims=True) + EPS)
    return y.astype(x.dtype) * g


def causal_dwconv(x, w, prev):
    T = x.shape[1]
    xp = jnp.concatenate([prev, x], axis=1)
    y = xp[:, 0:T] * w[0]
    for k in range(1, CONV_W):
        y = y + xp[:, k:k + T] * w[k]
    return y, xp[:, -(CONV_W - 1):]


def chunk_sgu(u, v, w_s, b_s):
    B, T, _ = v.shape
    Tp = -(-T // GMLP_CHUNK) * GMLP_CHUNK
    vp = jnp.pad(v, ((0, 0), (0, Tp - T), (0, 0)))
    vp = vp.reshape(B, Tp // GMLP_CHUNK, GMLP_CHUNK, GMLP_GROUPS, GMLP_GD)
    blk = jnp.arange(GMLP_CHUNK) // CHUNK
    mask = blk[None, :] <= blk[:, None]
    ws = jnp.where(mask[None], w_s, 0.0)
    mixed = jnp.einsum('gij,bcjgd->bcigd', ws, vp) + b_s.T[None, None, :, :, None]
    mixed = mixed.reshape(B, Tp, D_B)[:, :T]
    return u * mixed


def stick_breaking_attention(q, k, v, q_offset):
    B, Tq, H, Dh = q.shape
    Tk = k.shape[1]
    qb = QBLOCK if Tq % QBLOCK == 0 else Tq
    nb = Tq // qb
    scale = Dh ** -0.5
    k32 = k.astype(jnp.float32)
    kpos = jnp.arange(Tk)

    def block(args):
        q_blk, qpos = args
        z = jnp.einsum('bqhd,bkhd->bhqk', q_blk.astype(jnp.float32), k32) * scale
        mask = kpos[None, :] < qpos[:, None]
        log_fail = jnp.where(mask, jax.nn.log_sigmoid(-z), 0.0)
        later = lax.cumsum(log_fail, axis=3, reverse=True) - log_fail
        w = jnp.where(mask, jnp.exp(jax.nn.log_sigmoid(z) + later), 0.0)
        return jnp.einsum('bhqk,bkhd->bqhd', w.astype(v.dtype), v)

    q_blocks = q.reshape(B, nb, qb, H, Dh).transpose(1, 0, 2, 3, 4)
    qpos = (q_offset + jnp.arange(Tq)).reshape(nb, qb)
    out = lax.map(block, (q_blocks, qpos))
    return out.transpose(1, 0, 2, 3, 4).reshape(B, Tq, H, Dh)


def conv_chunk_mixer(h, w_in, w_conv, g_sgu, w_s, b_s, w_out, conv_prev):
    z = h @ w_in
    xa, gb, gc, u, v = jnp.split(z, [D_A, 2 * D_A, 3 * D_A, 3 * D_A + D_B], axis=-1)
    conv_out, conv_state = causal_dwconv(gc * xa, w_conv, conv_prev)
    ya = gb * conv_out
    vn = rms_norm(v, g_sgu)
    yb = chunk_sgu(u, vn, w_s, b_s)
    y = jnp.concatenate([ya, yb], axis=-1) @ w_out
    return y, conv_state, vn


def sb_mixer(h, w_qkv, w_o, k_past, v_past, q_offset):
    B, T, _ = h.shape
    q, k, v = jnp.split(h @ w_qkv, 3, axis=-1)
    q = q.reshape(B, T, SB_HEADS, SB_HEAD_DIM)
    k = k.reshape(B, T, SB_HEADS, SB_HEAD_DIM)
    v = v.reshape(B, T, SB_HEADS, SB_HEAD_DIM)
    if k_past is None:
        kk, vv = k, v
    else:
        kk = jnp.concatenate([k_past, k], axis=1)
        vv = jnp.concatenate([v_past, v], axis=1)
    o = stick_breaking_attention(q, kk, vv, q_offset)
    return o.reshape(B, T, D_MODEL) @ w_o, k, v


def conv_ffn(h, w_up, w_conv, b_conv, w_down, prev):
    a, g = jnp.split(h @ w_up, 2, axis=-1)
    a, new_prev = causal_dwconv(a, w_conv, prev)
    y = (jax.nn.silu(a + b_conv) * g) @ w_down
    return y, new_prev


def trunk(x, c, k_past, v_past, conv_a_prev, ffn_prev, w_mod, b_mod, norm_g, w_in_ab, w_conv_a,
          g_sgu, w_sgu, b_sgu, w_out_ab, w_qkv_sb, w_o_sb, w_ffn_up, w_ffn_conv, b_ffn_conv, w_ffn_down):
    q_offset = 0 if k_past is None else k_past.shape[2]
    k_new, v_new, conv_a_new, ffn_new, sgu_v_new = [], [], [], [], []
    for l in range(DEPTH):
        i = l // 2
        mod = jax.nn.silu(c) @ w_mod[l] + b_mod[l]
        sh1, sc1, g1, sh2, sc2, g2 = jnp.split(mod[:, None, :], 6, axis=-1)
        h = rms_norm(x, norm_g[l, 0]) * (1 + sc1) + sh1
        if l % 2 == 0:
            o, cs, vr = conv_chunk_mixer(h, w_in_ab[i], w_conv_a[i], g_sgu[i], w_sgu[i], b_sgu[i],
                                         w_out_ab[i], conv_a_prev[i])
            conv_a_new.append(cs)
            sgu_v_new.append(vr)
        else:
            kp = None if k_past is None else k_past[i]
            vp = None if v_past is None else v_past[i]
            o, kn, vn = sb_mixer(h, w_qkv_sb[i], w_o_sb[i], kp, vp, q_offset)
            k_new.append(kn)
            v_new.append(vn)
        x = x + g1 * rms_norm(o, norm_g[l, 1])
        h = rms_norm(x, norm_g[l, 2]) * (1 + sc2) + sh2
        o, fs = conv_ffn(h, w_ffn_up[l], w_ffn_conv[l], b_ffn_conv[l], w_ffn_down[l], ffn_prev[l])
        ffn_new.append(fs)
        x = x + g2 * rms_norm(o, norm_g[l, 3])
    return x, jnp.stack(k_new), jnp.stack(v_new), jnp.stack(conv_a_new), jnp.stack(ffn_new), sgu_v_new


def setup_inputs(seed: int = 0) -> dict:
    key = jax.random.key(seed)
    ks = jax.random.split(key, 24)
    n_even = (DEPTH + 1) // 2
    n_odd = DEPTH // 2
    d_in = 3 * D_A + 2 * D_B

    def nrm(k, shape, scale):
        return jax.random.normal(k, shape, jnp.float32) * scale

    return {
        'x_prompt': nrm(ks[0], (BATCH, SEQ, D_MODEL), 1.0),
        'x_sample': nrm(ks[1], (DEC_BATCH, DEC_SEQ, D_MODEL), 1.0),
        'c_prompt': nrm(ks[2], (BATCH, D_MODEL), 1.0),
        'c_sample': nrm(ks[3], (DEC_BATCH, D_MODEL), 1.0),
        'cache_sb_k': nrm(ks[4], (n_odd, DEC_BATCH, PAST_LEN, SB_HEADS, SB_HEAD_DIM), 1.0),
        'cache_sb_v': nrm(ks[5], (n_odd, DEC_BATCH, PAST_LEN, SB_HEADS, SB_HEAD_DIM), 1.0),
        'state_conv_a': nrm(ks[6], (n_even, DEC_BATCH, CONV_W - 1, D_A), 1.0),
        'state_ffn_conv': nrm(ks[7], (DEPTH, DEC_BATCH, CONV_W - 1, D_FF), 1.0),
        'w_mod': nrm(ks[8], (DEPTH, D_MODEL, 6 * D_MODEL), 0.5 * D_MODEL ** -0.5),
        'b_mod': nrm(ks[9], (DEPTH, 6 * D_MODEL), 0.02),
        'norm_g': 1.0 + nrm(ks[10], (DEPTH, 4, D_MODEL), 0.02),
        'w_in_ab': nrm(ks[11], (n_even, D_MODEL, d_in), D_MODEL ** -0.5),
        'w_conv_a': nrm(ks[12], (n_even, CONV_W, D_A), CONV_W ** -0.5),
        'g_sgu': 1.0 + nrm(ks[13], (n_even, D_B), 0.02),
        'w_sgu': nrm(ks[14], (n_even, GMLP_GROUPS, GMLP_CHUNK, GMLP_CHUNK), GMLP_CHUNK ** -0.5),
        'b_sgu': 1.0 + nrm(ks[15], (n_even, GMLP_GROUPS, GMLP_CHUNK), 0.1),
        'w_out_ab': nrm(ks[16], (n_even, D_A + D_B, D_MODEL), (D_A + D_B) ** -0.5),
        'w_qkv_sb': nrm(ks[17], (n_odd, D_MODEL, 3 * D_MODEL), D_MODEL ** -0.5),
        'w_o_sb': nrm(ks[18], (n_odd, D_MODEL, D_MODEL), D_MODEL ** -0.5),
        'w_ffn_up': nrm(ks[19], (DEPTH, D_MODEL, 2 * D_FF), D_MODEL ** -0.5),
        'w_ffn_conv': nrm(ks[20], (DEPTH, CONV_W, D_FF), CONV_W ** -0.5),
        'b_ffn_conv': nrm(ks[21], (DEPTH, D_FF), 0.02),
        'w_ffn_down': nrm(ks[22], (DEPTH, D_FF, D_MODEL), D_FF ** -0.5),
    }


def reference(x_prompt, x_sample, c_prompt, c_sample, cache_sb_k, cache_sb_v, state_conv_a, state_ffn_conv,
              w_mod, b_mod, norm_g, w_in_ab, w_conv_a, g_sgu, w_sgu, b_sgu, w_out_ab, w_qkv_sb, w_o_sb,
              w_ffn_up, w_ffn_conv, b_ffn_conv, w_ffn_down):
    n_even = (DEPTH + 1) // 2
    bp = x_prompt.shape[0]
    zeros_a = jnp.zeros((n_even, bp, CONV_W - 1, D_A), x_prompt.dtype)
    zeros_f = jnp.zeros((DEPTH, bp, CONV_W - 1, D_FF), x_prompt.dtype)
    y_prompt, k_p, v_p, conv_a_p, ffn_p, _ = trunk(
        x_prompt, c_prompt, None, None, zeros_a, zeros_f,
        w_mod, b_mod, norm_g, w_in_ab, w_conv_a, g_sgu, w_sgu, b_sgu, w_out_ab, w_qkv_sb, w_o_sb,
        w_ffn_up, w_ffn_conv, b_ffn_conv, w_ffn_down)
    y_sample, k_s, v_s, conv_a_s, ffn_s, sgu_v_list = trunk(
        x_sample, c_sample, cache_sb_k, cache_sb_v, state_conv_a, state_ffn_conv,
        w_mod, b_mod, norm_g, w_in_ab, w_conv_a, g_sgu, w_sgu, b_sgu, w_out_ab, w_qkv_sb, w_o_sb,
        w_ffn_up, w_ffn_conv, b_ffn_conv, w_ffn_down)
    sgu_v_s = jnp.stack(sgu_v_list)
    return (y_prompt, y_sample, k_p, v_p, conv_a_p, ffn_p, k_s, v_s, conv_a_s, ffn_s, sgu_v_s)
```

```python
import functools

import jax
import jax.numpy as jnp
from jax import lax
from jax.experimental import pallas as pl
from jax.experimental.pallas import tpu as pltpu

F32 = jnp.float32
BF16 = jnp.bfloat16

EPS = 1e-6
CONV_TAPS = 3
GMLP_CAUSAL_BLOCK = 64
SUBLANES = 8
VMEM_LIMIT_BYTES = 56 * 1024 * 1024

ROWS_IN_PROJ = 1024
ROWS_MIX = 256
ROWS_OUT_PROJ = 512
ROWS_FFN = 512
COLS_IN_PROJ = 512
COLS_MOD = 1024
ATTN_BLOCK = 256
ATTN_SAMPLE_HEADS = 4


def _params(*semantics):
    return pltpu.CompilerParams(dimension_semantics=semantics, vmem_limit_bytes=VMEM_LIMIT_BYTES)


def _split_rows(n_seq, seq_len, rows):
    if seq_len >= rows:
        assert seq_len % rows == 0
        return 1, rows
    s = min(rows // seq_len, n_seq)
    assert n_seq % s == 0 and seq_len % SUBLANES == 0
    return s, seq_len


def _seq_block(i, S, R, T):
    return i // (T // R) if S == 1 else i


def _largest_divisor(n, candidates):
    for c in candidates:
        if n % c == 0:
            return c
    raise ValueError(f"no block size among {candidates} divides {n}")


def _rms(x, g):
    return x * lax.rsqrt(jnp.mean(x * x, axis=-1, keepdims=True) + EPS) * g


def _per_row(v, R):
    S, _, C = v.shape
    if S == 1:
        return v[0]
    return jnp.broadcast_to(v, (S, R, C)).reshape(S * R, C)


def _sigmoid(x):
    return 1.0 / (1.0 + jnp.exp(-x))


def _causal_conv(a, p0, p1, w, R):
    rows, C = a.shape
    rt = lax.broadcasted_iota(jnp.int32, (rows, C), 0) & (R - 1)
    s1 = jnp.where(rt == 0, p1, pltpu.roll(a, 1, 0))
    s2 = jnp.where(rt == 0, p0, jnp.where(rt == 1, p1, pltpu.roll(a, 2, 0)))
    return s2 * w[0:1] + s1 * w[1:2] + a * w[2:3]


def _mod_kernel(c_ref, w_ref, b_ref, o_ref):
    c = c_ref[...]
    s = (c * _sigmoid(c)).astype(BF16)
    o_ref[...] = jnp.dot(s, w_ref[...].astype(BF16), preferred_element_type=F32) + b_ref[...]


def _modulation(c, w_mod, b_mod):
    L, D, N = w_mod.shape
    B = c.shape[0]
    tn = _largest_divisor(N, (COLS_MOD, 512, 256, 128))
    return pl.pallas_call(
        _mod_kernel,
        grid=(L, N // tn),
        in_specs=[pl.BlockSpec((B, D), lambda l, j: (0, 0)),
                  pl.BlockSpec((None, D, tn), lambda l, j: (l, 0, j)),
                  pl.BlockSpec((None, 1, tn), lambda l, j: (l, 0, j))],
        out_specs=pl.BlockSpec((None, B, tn), lambda l, j: (l, 0, j)),
        out_shape=jax.ShapeDtypeStruct((L, B, N), F32),
        compiler_params=_params("parallel", "parallel"),
        name="modulation",
    )(c, w_mod, b_mod.reshape(L, 1, N))


def _in_proj_kernel(x_ref, g_ref, sc_ref, sh_ref, w_ref, *rest, R, blocks_per_out):
    outs, h_ref = rest[:-1], rest[-1]
    j = pl.program_id(1)

    @pl.when(j == 0)
    def _():
        h = _rms(x_ref[...], g_ref[...]) * (1.0 + _per_row(sc_ref[...], R)) + _per_row(sh_ref[...], R)
        h_ref[...] = h.astype(BF16)

    y = jnp.dot(h_ref[...], w_ref[...], preferred_element_type=F32)
    if len(outs) == 1:
        outs[0][...] = y.astype(outs[0].dtype)
    else:
        for k, o_ref in enumerate(outs):
            @pl.when(j // blocks_per_out == k)
            def _(o_ref=o_ref):
                o_ref[...] = y.astype(o_ref.dtype)


def _in_proj(x, n_seq, T, g, mod, layer, sc_chunk, sh_chunk, w, out_dtypes):
    N, D = x.shape
    n_total = w.shape[1]
    n_out = len(out_dtypes)
    S, R = _split_rows(n_seq, T, ROWS_IN_PROJ)
    rows = S * R
    width = n_total // n_out
    tn = _largest_divisor(width, (COLS_IN_PROJ, 512, 256, 128))
    bpo = width // tn

    def mod_spec(chunk):
        return pl.BlockSpec((None, S, 1, D), lambda i, j: (layer, _seq_block(i, S, R, T), 0, chunk))

    def out_spec(k):
        return pl.BlockSpec((rows, tn), lambda i, j: (i, jnp.clip(j - k * bpo, 0, bpo - 1)))

    outs = pl.pallas_call(
        functools.partial(_in_proj_kernel, R=R, blocks_per_out=bpo),
        grid=(N // rows, n_total // tn),
        in_specs=[pl.BlockSpec((rows, D), lambda i, j: (i, 0)),
                  pl.BlockSpec((1, D), lambda i, j: (0, 0)),
                  mod_spec(sc_chunk), mod_spec(sh_chunk),
                  pl.BlockSpec((D, tn), lambda i, j: (0, j))],
        out_specs=[out_spec(k) for k in range(n_out)],
        out_shape=[jax.ShapeDtypeStruct((N, width), dt) for dt in out_dtypes],
        scratch_shapes=[pltpu.VMEM((rows, D), BF16)],
        compiler_params=_params("parallel", "arbitrary"),
        name="in_proj",
    )(x, g.reshape(1, D), mod, mod, w)
    return outs


def _sgu_matrices(ws_ref, bs_ref, g, T, chunk, whole_chunks):
    w = ws_ref[g]
    b = jnp.broadcast_to(bs_ref[g], (chunk, chunk))
    ii = lax.broadcasted_iota(jnp.int32, (chunk, chunk), 0)
    jj = lax.broadcasted_iota(jnp.int32, (chunk, chunk), 1)
    if whole_chunks:
        return jnp.where(jj // GMLP_CAUSAL_BLOCK <= ii // GMLP_CAUSAL_BLOCK, w, 0.0), b
    w_tl = jnp.where((ii < T) & (jj < T), w, 0.0)
    b_tl = jnp.where(ii < T, b, 0.0)
    wm, bm = w_tl, b_tl
    for k in range(1, chunk // T):
        wm = wm + pltpu.roll(pltpu.roll(w_tl, k * T, 0), k * T, 1)
        bm = bm + pltpu.roll(b_tl, k * T, 0)
    return wm, bm


def _mix_even_kernel(*refs, S, R, T, chunk, groups, halo, emit_vn):
    refs = list(refs)
    xa_ref, gb_ref, gc_ref, u_ref, v_ref = refs[:5]
    del refs[:5]
    if halo:
        hxa_ref, hgc_ref = refs[:2]
        del refs[:2]
    st_ref, x_ref, g1_ref, wc_ref, gs_ref, ws_ref, bs_ref, wo_ref, gn_ref = refs[:9]
    del refs[:9]
    xo_ref, tail_ref = refs[:2]
    del refs[:2]
    if emit_vn:
        vn_out_ref = refs.pop(0)
    cat_ref, vn_ref = refs

    i = pl.program_id(0)
    rows = S * R
    C = xa_ref.shape[1]

    ca = gc_ref[...] * xa_ref[...]
    st = st_ref[...]
    p0, p1 = st[:, 0:1, :], st[:, 1:2, :]
    if halo:
        hca = hgc_ref[...] * hxa_ref[...]
        first = (i % (T // R)) == 0
        p0 = jnp.where(first, p0, hca[SUBLANES - 2:SUBLANES - 1][None])
        p1 = jnp.where(first, p1, hca[SUBLANES - 1:SUBLANES][None])
    conv = _causal_conv(ca, _per_row(p0, R), _per_row(p1, R), wc_ref[...], R)
    cat_ref[:, 0:C] = (gb_ref[...] * conv).astype(BF16)
    tail_ref[...] = ca.reshape(S, R, C)[:, R - SUBLANES:R, :]

    vn = _rms(v_ref[...], gs_ref[...])
    if emit_vn:
        vn_out_ref[...] = vn
    vn_ref[...] = vn.astype(BF16)
    gd = v_ref.shape[1] // groups
    for g in range(groups):
        wm, bm = _sgu_matrices(ws_ref, bs_ref, g, T, chunk, whole_chunks=(R % chunk == 0))
        wm = wm.astype(BF16)
        cs = slice(g * gd, (g + 1) * gd)
        for c in range(rows // chunk):
            rs = slice(c * chunk, (c + 1) * chunk)
            mixed = jnp.dot(wm, vn_ref[rs, cs], preferred_element_type=F32) + bm
            cat_ref[rs, C + g * gd:C + (g + 1) * gd] = (u_ref[rs, cs] * mixed).astype(BF16)

    y = jnp.dot(cat_ref[...], wo_ref[...], preferred_element_type=F32)
    xo_ref[...] = x_ref[...] + _per_row(g1_ref[...], R) * _rms(y, gn_ref[...])


def _mix_even(z, x, n_seq, T, state, mod, layer, w_conv, g_sgu, w_sgu, b_sgu, w_out, g_norm, emit_vn):
    N, D = x.shape
    C = w_conv.shape[1]
    groups, chunk = w_sgu.shape[0], w_sgu.shape[1]
    Cb = g_sgu.shape[0]
    S, R = _split_rows(n_seq, T, ROWS_MIX)
    rows = S * R
    halo = R < T
    assert rows % chunk == 0 and (R % chunk == 0 or (chunk % T == 0 and T <= GMLP_CAUSAL_BLOCK))
    assert z.shape[1] == 3 * C + 2 * Cb and C == Cb

    in_specs = [pl.BlockSpec((rows, C), functools.partial(lambda i, k: (i, k), k=k)) for k in range(5)]
    args = [z] * 5
    if halo:
        per = R // SUBLANES
        in_specs += [pl.BlockSpec((SUBLANES, C), functools.partial(
            lambda i, k: (jnp.maximum(i * per - 1, 0), k), k=k)) for k in (0, 2)]
        args += [z, z]
    in_specs += [
        pl.BlockSpec((S, CONV_TAPS - 1, C), lambda i: (_seq_block(i, S, R, T), 0, 0)),
        pl.BlockSpec((rows, D), lambda i: (i, 0)),
        pl.BlockSpec((None, S, 1, D), lambda i: (layer, _seq_block(i, S, R, T), 0, 2)),
        pl.BlockSpec((CONV_TAPS, C), lambda i: (0, 0)),
        pl.BlockSpec((1, Cb), lambda i: (0, 0)),
        pl.BlockSpec((groups, chunk, chunk), lambda i: (0, 0, 0)),
        pl.BlockSpec((groups, chunk, 1), lambda i: (0, 0, 0)),
        pl.BlockSpec((C + Cb, D), lambda i: (0, 0)),
        pl.BlockSpec((1, D), lambda i: (0, 0)),
    ]
    args += [state, x, mod, w_conv, g_sgu.reshape(1, Cb), w_sgu, b_sgu.reshape(groups, chunk, 1), w_out,
             g_norm.reshape(1, D)]
    out_specs = [pl.BlockSpec((rows, D), lambda i: (i, 0)),
                 pl.BlockSpec((S, SUBLANES, C), lambda i: (i, 0, 0))]
    out_shape = [jax.ShapeDtypeStruct((N, D), F32),
                 jax.ShapeDtypeStruct((N // rows * S, SUBLANES, C), F32)]
    if emit_vn:
        out_specs.append(pl.BlockSpec((rows, Cb), lambda i: (i, 0)))
        out_shape.append(jax.ShapeDtypeStruct((N, Cb), F32))
    return pl.pallas_call(
        functools.partial(_mix_even_kernel, S=S, R=R, T=T, chunk=chunk, groups=groups, halo=halo, emit_vn=emit_vn),
        grid=(N // rows,),
        in_specs=in_specs, out_specs=out_specs, out_shape=out_shape,
        scratch_shapes=[pltpu.VMEM((rows, C + Cb), BF16), pltpu.VMEM((rows, Cb), BF16)],
        compiler_params=_params("parallel"),
        name="mix_even",
    )(*args)


def _upper_inclusive(n):
    r = lax.broadcasted_iota(jnp.int32, (n, n), 0)
    c = lax.broadcasted_iota(jnp.int32, (n, n), 1)
    return jnp.where(r >= c, 1.0, 0.0).astype(BF16)


def _strictly_earlier(n):
    t = lax.broadcasted_iota(jnp.int32, (n, n), 0)
    s = lax.broadcasted_iota(jnp.int32, (n, n), 1)
    return s < t


def _sb_block(q, k, v, u_inc, c, acc, mask, scale):
    z = lax.dot_general(q, k, (((1,), (1,)), ((), ())), preferred_element_type=F32) * scale
    log_fail = -(jnp.maximum(z, 0.0) + jnp.log(1.0 + jnp.exp(-jnp.abs(z))))
    if mask is not None:
        log_fail = jnp.where(mask, log_fail, 0.0)
    hi = log_fail.astype(BF16)
    lo = (log_fail - hi.astype(F32)).astype(BF16)
    incl = (jnp.dot(hi, u_inc, preferred_element_type=F32) + jnp.dot(lo, u_inc, preferred_element_type=F32))
    w = jnp.exp(z + incl + c)
    if mask is not None:
        w = jnp.where(mask, w, 0.0)
    acc = acc + jnp.dot(w.astype(BF16), v, preferred_element_type=F32)
    return c + incl[:, 0:1], acc


def _attn_prompt_kernel(q_ref, k_ref, v_ref, o_ref, kb_ref, vb_ref, u_ref, *, tq, scale):
    qi = pl.program_id(2)

    @pl.when(qi == 0)
    def _():
        kb_ref[...] = k_ref[...].astype(BF16)
        vb_ref[...] = v_ref[...].astype(BF16)
        u_ref[...] = _upper_inclusive(tq)

    q = q_ref[...]
    u = u_ref[...]
    dh = q.shape[1]

    def block(n, c, acc, mask):
        s0 = pl.multiple_of((qi - n) * tq, tq)
        return _sb_block(q, kb_ref[pl.ds(s0, tq), :], vb_ref[pl.ds(s0, tq), :], u, c, acc, mask, scale)

    c, acc = block(0, jnp.zeros((tq, 1), F32), jnp.zeros((tq, dh), F32), _strictly_earlier(tq))
    c, acc = lax.fori_loop(1, qi + 1, lambda n, ca: block(n, ca[0], ca[1], None), (c, acc))
    o_ref[...] = acc.astype(o_ref.dtype)


def _attn_prompt(q, k, v, n_seq, T, heads):
    N, D = q.shape
    dh = D // heads
    tq = min(ATTN_BLOCK, T)
    assert T % tq == 0
    q3, k3, v3 = (a.reshape(n_seq, T, D) for a in (q, k, v))
    o = pl.pallas_call(
        functools.partial(_attn_prompt_kernel, tq=tq, scale=dh ** -0.5),
        grid=(n_seq, heads, T // tq),
        in_specs=[pl.BlockSpec((None, tq, dh), lambda b, h, i: (b, i, h)),
                  pl.BlockSpec((None, T, dh), lambda b, h, i: (b, 0, h)),
                  pl.BlockSpec((None, T, dh), lambda b, h, i: (b, 0, h))],
        out_specs=pl.BlockSpec((None, tq, dh), lambda b, h, i: (b, i, h)),
        out_shape=jax.ShapeDtypeStruct((n_seq, T, D), BF16),
        scratch_shapes=[pltpu.VMEM((T, dh), BF16), pltpu.VMEM((T, dh), BF16), pltpu.VMEM((tq, tq), BF16)],
        compiler_params=_params("parallel", "parallel", "arbitrary"),
        name="attn_prompt",
    )(q3, k3, v3)
    return o.reshape(N, D)


def _attn_sample_kernel(q_ref, kn_ref, vn_ref, ck_ref, cv_ref, o_ref, *, T, P, tk, heads, dh, scale):
    u_new = _upper_inclusive(T)
    u_past = _upper_inclusive(tk)
    mask = _strictly_earlier(T)
    for h in range(heads):
        cs = slice(h * dh, (h + 1) * dh)
        q = q_ref[:, cs]
        c, acc = _sb_block(q, kn_ref[:, cs].astype(BF16), vn_ref[:, cs].astype(BF16), u_new,
                           jnp.zeros((T, 1), F32), jnp.zeros((T, dh), F32), mask, scale)
        for n in range(P // tk):
            rs = slice(P - (n + 1) * tk, P - n * tk)
            c, acc = _sb_block(q, ck_ref[rs, cs].astype(BF16), cv_ref[rs, cs].astype(BF16), u_past,
                               c, acc, None, scale)
        o_ref[:, cs] = acc.astype(o_ref.dtype)


def _attn_sample(q, k, v, cache_k, cache_v, layer, n_seq, T, heads):
    N, D = q.shape
    dh = D // heads
    P = cache_k.shape[2]
    tk = _largest_divisor(P, (ATTN_BLOCK, 128))
    hg = _largest_divisor(heads, (ATTN_SAMPLE_HEADS, 2, 1))
    q3, k3, v3 = (a.reshape(n_seq, T, D) for a in (q, k, v))
    new_spec = pl.BlockSpec((None, T, hg * dh), lambda b, g: (b, 0, g))
    past_spec = pl.BlockSpec((None, None, P, hg * dh), lambda b, g: (layer, b, 0, g))
    o = pl.pallas_call(
        functools.partial(_attn_sample_kernel, T=T, P=P, tk=tk, heads=hg, dh=dh, scale=dh ** -0.5),
        grid=(n_seq, heads // hg),
        in_specs=[new_spec, new_spec, new_spec, past_spec, past_spec],
        out_specs=new_spec,
        out_shape=jax.ShapeDtypeStruct((n_seq, T, D), BF16),
        compiler_params=_params("parallel", "parallel"),
        name="attn_sample",
    )(q3, k3, v3, cache_k, cache_v)
    return o.reshape(N, D)


def _out_proj_kernel(o_ref, w_ref, x_ref, g1_ref, gn_ref, xo_ref, *, R):
    y = jnp.dot(o_ref[...], w_ref[...], preferred_element_type=F32)
    xo_ref[...] = x_ref[...] + _per_row(g1_ref[...], R) * _rms(y, gn_ref[...])


def _out_proj(o, w, x, n_seq, T, mod, layer, g_norm):
    N, D = x.shape
    S, R = _split_rows(n_seq, T, ROWS_OUT_PROJ)
    rows = S * R
    return pl.pallas_call(
        functools.partial(_out_proj_kernel, R=R),
        grid=(N // rows,),
        in_specs=[pl.BlockSpec((rows, D), lambda i: (i, 0)),
                  pl.BlockSpec((D, D), lambda i: (0, 0)),
                  pl.BlockSpec((rows, D), lambda i: (i, 0)),
                  pl.BlockSpec((None, S, 1, D), lambda i: (layer, _seq_block(i, S, R, T), 0, 2)),
                  pl.BlockSpec((1, D), lambda i: (0, 0))],
        out_specs=pl.BlockSpec((rows, D), lambda i: (i, 0)),
        out_shape=jax.ShapeDtypeStruct((N, D), F32),
        compiler_params=_params("parallel"),
        name="out_proj",
    )(o, w, x, mod, g_norm.reshape(1, D))


def _ffn_kernel(x_ref, gn_in_ref, sc_ref, sh_ref, wa_ref, wg_ref, wc_ref, bc_ref, wd_ref, st_ref, g2_ref,
                gn_out_ref, xo_ref, tail_ref, h_ref, acc_ref, carry_ref, *, S, R, T):
    i, j = pl.program_id(0), pl.program_id(1)
    rows = S * R
    tf = wa_ref.shape[1]

    @pl.when(j == 0)
    def _():
        h = _rms(x_ref[...], gn_in_ref[...]) * (1.0 + _per_row(sc_ref[...], R)) + _per_row(sh_ref[...], R)
        h_ref[...] = h.astype(BF16)

    h = h_ref[...]
    a = jnp.dot(h, wa_ref[...], preferred_element_type=F32)
    gate = jnp.dot(h, wg_ref[...], preferred_element_type=F32)
    tail = a.reshape(S, R, tf)[:, R - SUBLANES:R, :]
    tail_ref[...] = tail

    st = st_ref[...]
    p0, p1 = st[:, 0:1, :], st[:, 1:2, :]
    if R < T:
        @pl.when(i == 0)
        def _():
            carry_ref[j] = jnp.zeros((SUBLANES, tf), F32)

        first = (i % (T // R)) == 0
        prev = carry_ref[j]
        p0 = jnp.where(first, p0, prev[SUBLANES - 2:SUBLANES - 1][None])
        p1 = jnp.where(first, p1, prev[SUBLANES - 1:SUBLANES][None])
        carry_ref[j] = tail[0]

    conv = _causal_conv(a, _per_row(p0, R), _per_row(p1, R), wc_ref[...], R) + bc_ref[...]
    act = (conv * _sigmoid(conv) * gate).astype(BF16)
    contrib = jnp.dot(act, wd_ref[...], preferred_element_type=F32)

    @pl.when(j == 0)
    def _():
        acc_ref[...] = contrib

    @pl.when(j > 0)
    def _():
        acc_ref[...] += contrib

    @pl.when(j == pl.num_programs(1) - 1)
    def _():
        xo_ref[...] = x_ref[...] + _per_row(g2_ref[...], R) * _rms(acc_ref[...], gn_out_ref[...])


def _ffn(x, n_seq, T, state, mod, layer, g_in, w_up, w_conv, b_conv, w_down, g_out):
    N, D = x.shape
    F = w_down.shape[0]
    S, R = _split_rows(n_seq, T, ROWS_FFN)
    rows = S * R
    tf = _largest_divisor(F, (512, 256, 128))
    nf = F // tf

    def mod_spec(chunk):
        return pl.BlockSpec((None, S, 1, D), lambda i, j: (layer, _seq_block(i, S, R, T), 0, chunk))

    return pl.pallas_call(
        functools.partial(_ffn_kernel, S=S, R=R, T=T),
        grid=(N // rows, nf),
        in_specs=[pl.BlockSpec((rows, D), lambda i, j: (i, 0)),
                  pl.BlockSpec((1, D), lambda i, j: (0, 0)),
                  mod_spec(4), mod_spec(3),
                  pl.BlockSpec((D, tf), lambda i, j: (0, j)),
                  pl.BlockSpec((D, tf), lambda i, j: (0, nf + j)),
                  pl.BlockSpec((CONV_TAPS, tf), lambda i, j: (0, j)),
                  pl.BlockSpec((1, tf), lambda i, j: (0, j)),
                  pl.BlockSpec((tf, D), lambda i, j: (j, 0)),
                  pl.BlockSpec((S, CONV_TAPS - 1, tf), lambda i, j: (_seq_block(i, S, R, T), 0, j)),
                  mod_spec(5),
                  pl.BlockSpec((1, D), lambda i, j: (0, 0))],
        out_specs=[pl.BlockSpec((rows, D), lambda i, j: (i, 0)),
                   pl.BlockSpec((S, SUBLANES, tf), lambda i, j: (i, 0, j))],
        out_shape=[jax.ShapeDtypeStruct((N, D), F32),
                   jax.ShapeDtypeStruct((N // rows * S, SUBLANES, F), F32)],
        scratch_shapes=[pltpu.VMEM((rows, D), BF16), pltpu.VMEM((rows, D), F32),
                        pltpu.VMEM((nf, SUBLANES, tf), F32)],
        compiler_params=_params("arbitrary", "arbitrary"),
        name="ffn",
    )(x, g_in.reshape(1, D), mod, mod, w_up, w_up, w_conv, b_conv.reshape(1, F), w_down, state, mod,
      g_out.reshape(1, D))


def _last_rows(tails, n_seq):
    t = tails.reshape(n_seq, tails.shape[0] // n_seq, SUBLANES, tails.shape[2])
    return t[:, -1, SUBLANES - (CONV_TAPS - 1):, :]


def _trunk(x, mod, cache_k, cache_v, conv_a_prev, ffn_prev, norm_g, w_in_ab, w_conv_a, g_sgu, w_sgu, b_sgu,
           w_out_ab, w_qkv_sb, w_o_sb, w_ffn_up, w_ffn_conv, b_ffn_conv, w_ffn_down, heads, emit_vn):
    n_seq, T, D = x.shape
    x = x.reshape(n_seq * T, D)
    depth = norm_g.shape[0]
    k_new, v_new, conv_a_new, ffn_new, sgu_v_new = [], [], [], [], []
    for l in range(depth):
        i = l // 2
        if l % 2 == 0:
            (z,) = _in_proj(x, n_seq, T, norm_g[l, 0], mod, l, 1, 0, w_in_ab[i], (F32,))
            res = _mix_even(z, x, n_seq, T, conv_a_prev[i], mod, l, w_conv_a[i], g_sgu[i], w_sgu[i], b_sgu[i],
                            w_out_ab[i], norm_g[l, 1], emit_vn)
            x = res[0]
            conv_a_new.append(_last_rows(res[1], n_seq))
            if emit_vn:
                sgu_v_new.append(res[2].reshape(n_seq, T, -1))
        else:
            q, k, v = _in_proj(x, n_seq, T, norm_g[l, 0], mod, l, 1, 0, w_qkv_sb[i], (BF16, F32, F32))
            if cache_k is None:
                o = _attn_prompt(q, k, v, n_seq, T, heads)
            else:
                o = _attn_sample(q, k, v, cache_k, cache_v, i, n_seq, T, heads)
            x = _out_proj(o, w_o_sb[i], x, n_seq, T, mod, l, norm_g[l, 1])
            k_new.append(k.reshape(n_seq, T, heads, D // heads))
            v_new.append(v.reshape(n_seq, T, heads, D // heads))
        x, tails = _ffn(x, n_seq, T, ffn_prev[l], mod, l, norm_g[l, 2], w_ffn_up[l], w_ffn_conv[l], b_ffn_conv[l],
                        w_ffn_down[l], norm_g[l, 3])
        ffn_new.append(_last_rows(tails, n_seq))
    return (x.reshape(n_seq, T, D), jnp.stack(k_new), jnp.stack(v_new), jnp.stack(conv_a_new), jnp.stack(ffn_new),
            sgu_v_new)


def kernel(x_prompt, x_sample, c_prompt, c_sample, cache_sb_k, cache_sb_v, state_conv_a, state_ffn_conv, w_mod, b_mod,
           norm_g, w_in_ab, w_conv_a, g_sgu, w_sgu, b_sgu, w_out_ab, w_qkv_sb, w_o_sb, w_ffn_up, w_ffn_conv,
           b_ffn_conv, w_ffn_down):
    bp, bs = x_prompt.shape[0], x_sample.shape[0]
    depth = norm_g.shape[0]
    heads = cache_sb_k.shape[3]
    n_odd, _, past, _, dh = cache_sb_k.shape

    mod = _modulation(jnp.concatenate([c_prompt, c_sample], axis=0), w_mod, b_mod)
    mod_p = mod[:, :bp].reshape(depth, bp, 1, -1)
    mod_s = mod[:, bp:].reshape(depth, bs, 1, -1)

    weights = (norm_g, w_in_ab.astype(BF16), w_conv_a, g_sgu, w_sgu, b_sgu, w_out_ab.astype(BF16),
               w_qkv_sb.astype(BF16), w_o_sb.astype(BF16), w_ffn_up.astype(BF16), w_ffn_conv, b_ffn_conv,
               w_ffn_down.astype(BF16))

    zeros_a = jnp.zeros((state_conv_a.shape[0], bp) + state_conv_a.shape[2:], F32)
    zeros_f = jnp.zeros((depth, bp) + state_ffn_conv.shape[2:], F32)
    y_p, k_p, v_p, conv_a_p, ffn_p, _ = _trunk(x_prompt, mod_p, None, None, zeros_a, zeros_f, *weights,
                                               heads=heads, emit_vn=False)
    cache_k = cache_sb_k.reshape(n_odd, bs, past, heads * dh)
    cache_v = cache_sb_v.reshape(n_odd, bs, past, heads * dh)
    y_s, k_s, v_s, conv_a_s, ffn_s, sgu_v = _trunk(x_sample, mod_s, cache_k, cache_v, state_conv_a, state_ffn_conv,
                                                   *weights, heads=heads, emit_vn=True)
    return (y_p, y_s, k_p, v_p, conv_a_p, ffn_p, k_s, v_s, conv_a_s, ffn_s, jnp.stack(sgu_v))
```

```python
import functools
import math

import jax
import jax.numpy as jnp
from jax import lax
from jax.experimental import pallas as pl
from jax.experimental.pallas import tpu as pltpu

F32 = jnp.float32
BF16 = jnp.bfloat16

EPS = 1e-6
CONV_TAPS = 3
GMLP_CAUSAL_BLOCK = 64
SUBLANES = 8
MXU_COLS = 256
VMEM_LIMIT_BYTES = 56 * 1024 * 1024

ROWS_IN_PROJ = 1024
ROWS_MIX = 256
ROWS_OUT_PROJ = 512
ROWS_FFN = 512
COLS_IN_PROJ = 512
COLS_FFN = 512
COLS_MOD = 1024
ATTN_BLOCK = 256
ATTN_PROMPT_HEADS = 4
ATTN_SAMPLE_HEADS = 8

LOG2E = math.log2(math.e)
SB_Z_MAX = 126.0
SB_DONE = 150.0 / LOG2E


def _params(*semantics):
    return pltpu.CompilerParams(dimension_semantics=semantics, vmem_limit_bytes=VMEM_LIMIT_BYTES)


def _split_rows(n_seq, seq_len, rows):
    if seq_len >= rows:
        assert seq_len % rows == 0
        return 1, rows
    s = min(rows // seq_len, n_seq)
    assert n_seq % s == 0 and seq_len % SUBLANES == 0
    return s, seq_len


def _seq_block(i, S, R, T):
    return i // (T // R) if S == 1 else i


def _largest_divisor(n, candidates):
    for c in candidates:
        if n % c == 0:
            return c
    raise ValueError(f"no block size among {candidates} divides {n}")


def _rms(x, g):
    return x * lax.rsqrt(jnp.mean(x * x, axis=-1, keepdims=True) + EPS) * g


def _per_row(v, R):
    S, _, C = v.shape
    if S == 1:
        return v[0]
    return jnp.broadcast_to(v, (S, R, C)).reshape(S * R, C)


def _sigmoid(x):
    return 1.0 / (1.0 + jnp.exp(-x))


def _causal_conv(a, p0, p1, w, R):
    rows, C = a.shape
    rt = lax.broadcasted_iota(jnp.int32, (rows, C), 0) & (R - 1)
    s1 = jnp.where(rt == 0, p1, pltpu.roll(a, 1, 0))
    s2 = jnp.where(rt == 0, p0, jnp.where(rt == 1, p1, pltpu.roll(a, 2, 0)))
    return s2 * w[0:1] + s1 * w[1:2] + a * w[2:3]


def _mod_kernel(c_ref, w_ref, b_ref, o_ref):
    c = c_ref[...]
    s = (c * _sigmoid(c)).astype(BF16)
    o_ref[...] = jnp.dot(s, w_ref[...].astype(BF16), preferred_element_type=F32) + b_ref[...]


def _modulation(c, w_mod, b_mod):
    L, D, N = w_mod.shape
    B = c.shape[0]
    tn = _largest_divisor(N, (COLS_MOD, 512, 256, 128))
    return pl.pallas_call(
        _mod_kernel,
        grid=(L, N // tn),
        in_specs=[pl.BlockSpec((B, D), lambda l, j: (0, 0)),
                  pl.BlockSpec((None, D, tn), lambda l, j: (l, 0, j)),
                  pl.BlockSpec((None, 1, tn), lambda l, j: (l, 0, j))],
        out_specs=pl.BlockSpec((None, B, tn), lambda l, j: (l, 0, j)),
        out_shape=jax.ShapeDtypeStruct((L, B, N), F32),
        compiler_params=_params("parallel", "parallel"),
        name="modulation",
    )(c, w_mod, b_mod.reshape(L, 1, N))


def _in_proj_kernel(x_ref, g_ref, sc_ref, sh_ref, w_ref, *rest, R, blocks_per_out, n_out, out_scales):
    outs, h_ref = rest[-1 - n_out:-1], rest[-1]
    j = pl.program_id(1)

    @pl.when(j == 0)
    def _():
        h = _rms(x_ref[...], g_ref[...]) * (1.0 + _per_row(sc_ref[...], R)) + _per_row(sh_ref[...], R)
        h_ref[...] = h.astype(BF16)

    y = jnp.dot(h_ref[...], w_ref[...], preferred_element_type=F32)
    if n_out == 1:
        outs[0][...] = (y * out_scales[0]).astype(outs[0].dtype)
    else:
        for k, o_ref in enumerate(outs):
            @pl.when(j // blocks_per_out == k)
            def _(o_ref=o_ref, k=k):
                o_ref[...] = (y if out_scales[k] == 1.0 else y * out_scales[k]).astype(o_ref.dtype)


def _in_proj(x, n_seq, T, g, mod, layer, sc_chunk, sh_chunk, w, wi, out_dtypes, out_scales, stacks=None, slot=0,
             n_slots=1):
    N, D = x.shape
    n_total = w.shape[2]
    n_out = len(out_dtypes)
    S, R = _split_rows(n_seq, T, ROWS_IN_PROJ)
    rows = S * R
    width = n_total // n_out
    tn = _largest_divisor(width, (COLS_IN_PROJ, 256, 128))
    bpo = width // tn

    def mod_spec(chunk):
        return pl.BlockSpec((None, S, 1, D), lambda i, j: (layer, _seq_block(i, S, R, T), 0, chunk))

    def col(j, k):
        return jnp.clip(j - k * bpo, 0, bpo - 1)

    out_specs = [pl.BlockSpec((rows, tn), lambda i, j: (i, col(j, 0)))]
    out_shape = [jax.ShapeDtypeStruct((N, width), out_dtypes[0])]
    for k in range(1, n_out):
        out_specs.append(pl.BlockSpec((None, rows, tn), functools.partial(lambda i, j, k: (slot, i, col(j, k)), k=k)))
        out_shape.append(jax.ShapeDtypeStruct((n_slots, N, width), out_dtypes[k]))
    in_specs = [pl.BlockSpec((rows, D), lambda i, j: (i, 0)),
                pl.BlockSpec((1, D), lambda i, j: (0, 0)),
                mod_spec(sc_chunk), mod_spec(sh_chunk),
                pl.BlockSpec((None, D, tn), lambda i, j: (wi, 0, j))]
    args = [x, g.reshape(1, D), mod, mod, w]
    aliases = {}
    if stacks is not None:
        for k, st in enumerate(stacks):
            aliases[len(args)] = k + 1
            in_specs.append(pl.BlockSpec(memory_space=pl.ANY))
            args.append(st)
    return pl.pallas_call(
        functools.partial(_in_proj_kernel, R=R, blocks_per_out=bpo, n_out=n_out, out_scales=tuple(out_scales)),
        grid=(N // rows, n_total // tn),
        in_specs=in_specs, out_specs=out_specs, out_shape=out_shape,
        scratch_shapes=[pltpu.VMEM((rows, D), BF16)],
        input_output_aliases=aliases,
        compiler_params=_params("parallel", "arbitrary"),
        name="in_proj",
    )(*args)


def _sgu_matrices(ws_ref, bs_ref, g, T, chunk, whole_chunks):
    w = ws_ref[g]
    b = jnp.broadcast_to(bs_ref[g], (chunk, chunk))
    ii = lax.broadcasted_iota(jnp.int32, (chunk, chunk), 0)
    jj = lax.broadcasted_iota(jnp.int32, (chunk, chunk), 1)
    if whole_chunks:
        return jnp.where(jj // GMLP_CAUSAL_BLOCK <= ii // GMLP_CAUSAL_BLOCK, w, 0.0), b
    w_tl = jnp.where((ii < T) & (jj < T), w, 0.0)
    b_tl = jnp.where(ii < T, b, 0.0)
    wm, bm = w_tl, b_tl
    for k in range(1, chunk // T):
        wm = wm + pltpu.roll(pltpu.roll(w_tl, k * T, 0), k * T, 1)
        bm = bm + pltpu.roll(b_tl, k * T, 0)
    return wm, bm


def _mix_even_kernel(*refs, S, R, T, chunk, groups, halo, emit_vn):
    refs = list(refs)
    xa_ref, gb_ref, gc_ref, u_ref, v_ref = refs[:5]
    del refs[:5]
    if halo:
        hxa_ref, hgc_ref = refs[:2]
        del refs[:2]
    st_ref, x_ref, g1_ref, wc_ref, gs_ref, ws_ref, bs_ref, wo_ref, gn_ref = refs[:9]
    del refs[:9]
    xo_ref, tail_ref = refs[:2]
    del refs[:2]
    if emit_vn:
        vn_out_ref = refs.pop(0)
    cat_ref, vn_ref = refs

    i = pl.program_id(0)
    rows = S * R
    C = xa_ref.shape[1]

    ca = gc_ref[...] * xa_ref[...]
    st = st_ref[...]
    p0, p1 = st[:, 0:1, :], st[:, 1:2, :]
    if halo:
        hca = hgc_ref[...] * hxa_ref[...]
        first = (i % (T // R)) == 0
        p0 = jnp.where(first, p0, hca[SUBLANES - 2:SUBLANES - 1][None])
        p1 = jnp.where(first, p1, hca[SUBLANES - 1:SUBLANES][None])
    conv = _causal_conv(ca, _per_row(p0, R), _per_row(p1, R), wc_ref[...], R)
    cat_ref[:, 0:C] = (gb_ref[...] * conv).astype(BF16)
    tail_ref[...] = ca.reshape(S, R, C)[:, R - SUBLANES:R, :]

    vn = _rms(v_ref[...], gs_ref[...])
    if emit_vn:
        vn_out_ref[...] = vn
    vn_ref[...] = vn.astype(BF16)
    gd = v_ref.shape[1] // groups
    for g in range(groups):
        wm, bm = _sgu_matrices(ws_ref, bs_ref, g, T, chunk, whole_chunks=(R % chunk == 0))
        wm = wm.astype(BF16)
        cs = slice(g * gd, (g + 1) * gd)
        for c in range(rows // chunk):
            rs = slice(c * chunk, (c + 1) * chunk)
            mixed = jnp.dot(wm, vn_ref[rs, cs], preferred_element_type=F32) + bm
            cat_ref[rs, C + g * gd:C + (g + 1) * gd] = (u_ref[rs, cs] * mixed).astype(BF16)

    y = jnp.dot(cat_ref[...], wo_ref[...], preferred_element_type=F32)
    xo_ref[...] = x_ref[...] + _per_row(g1_ref[...], R) * _rms(y, gn_ref[...])


def _mix_even(z, x, n_seq, T, state, mod, layer, wi, w_conv, g_sgu, w_sgu, b_sgu, w_out, g_norm, emit_vn):
    N, D = x.shape
    C = w_conv.shape[2]
    groups, chunk = w_sgu.shape[1], w_sgu.shape[2]
    Cb = g_sgu.shape[1]
    S, R = _split_rows(n_seq, T, ROWS_MIX)
    rows = S * R
    halo = R < T
    assert rows % chunk == 0 and (R % chunk == 0 or (chunk % T == 0 and T <= GMLP_CAUSAL_BLOCK))
    assert z.shape[1] == 3 * C + 2 * Cb and C == Cb

    in_specs = [pl.BlockSpec((rows, C), functools.partial(lambda i, k: (i, k), k=k)) for k in range(5)]
    args = [z] * 5
    if halo:
        per = R // SUBLANES
        in_specs += [pl.BlockSpec((SUBLANES, C), functools.partial(
            lambda i, k: (jnp.maximum(i * per - 1, 0), k), k=k)) for k in (0, 2)]
        args += [z, z]
    in_specs += [
        pl.BlockSpec((None, S, CONV_TAPS - 1, C), lambda i: (wi, _seq_block(i, S, R, T), 0, 0)),
        pl.BlockSpec((rows, D), lambda i: (i, 0)),
        pl.BlockSpec((None, S, 1, D), lambda i: (layer, _seq_block(i, S, R, T), 0, 2)),
        pl.BlockSpec((None, CONV_TAPS, C), lambda i: (wi, 0, 0)),
        pl.BlockSpec((None, 1, Cb), lambda i: (wi, 0, 0)),
        pl.BlockSpec((None, groups, chunk, chunk), lambda i: (wi, 0, 0, 0)),
        pl.BlockSpec((None, groups, chunk, 1), lambda i: (wi, 0, 0, 0)),
        pl.BlockSpec((None, C + Cb, D), lambda i: (wi, 0, 0)),
        pl.BlockSpec((1, D), lambda i: (0, 0)),
    ]
    n_even = w_conv.shape[0]
    args += [state, x, mod, w_conv, g_sgu.reshape(n_even, 1, Cb), w_sgu, b_sgu.reshape(n_even, groups, chunk, 1),
             w_out, g_norm.reshape(1, D)]
    out_specs = [pl.BlockSpec((rows, D), lambda i: (i, 0)),
                 pl.BlockSpec((S, SUBLANES, C), lambda i: (i, 0, 0))]
    out_shape = [jax.ShapeDtypeStruct((N, D), F32),
                 jax.ShapeDtypeStruct((N // rows * S, SUBLANES, C), F32)]
    if emit_vn:
        out_specs.append(pl.BlockSpec((rows, Cb), lambda i: (i, 0)))
        out_shape.append(jax.ShapeDtypeStruct((N, Cb), F32))
    return pl.pallas_call(
        functools.partial(_mix_even_kernel, S=S, R=R, T=T, chunk=chunk, groups=groups, halo=halo, emit_vn=emit_vn),
        grid=(N // rows,),
        in_specs=in_specs, out_specs=out_specs, out_shape=out_shape,
        scratch_shapes=[pltpu.VMEM((rows, C + Cb), BF16), pltpu.VMEM((rows, Cb), BF16)],
        compiler_params=_params("parallel"),
        name="mix_even",
    )(*args)


def _upper_inclusive(n):
    r = lax.broadcasted_iota(jnp.int32, (n, n), 0)
    c = lax.broadcasted_iota(jnp.int32, (n, n), 1)
    return jnp.where(r >= c, 1.0, 0.0).astype(BF16)


def _strictly_earlier(n):
    t = lax.broadcasted_iota(jnp.int32, (n, n), 0)
    s = lax.broadcasted_iota(jnp.int32, (n, n), 1)
    return s < t


def _sb_blocks(qs, ks, vs, u_inc, cs, accs, mask):
    tq = qs[0].shape[0]
    zs, parts = [], []
    for q, k in zip(qs, ks):
        z = jnp.minimum(lax.dot_general(q, k, (((1,), (1,)), ((), ())), preferred_element_type=F32), SB_Z_MAX)
        fail = jnp.log(1.0 + jnp.exp2(z))
        if mask is not None:
            fail = jnp.where(mask, fail, 0.0)
        hi = fail.astype(BF16)
        parts += [hi, (fail - hi.astype(F32)).astype(BF16)]
        zs.append(z)
    sums = jnp.dot(jnp.concatenate(parts, axis=0), u_inc, preferred_element_type=F32)
    cs_out, accs_out = [], []
    for h, (z, v, c, acc) in enumerate(zip(zs, vs, cs, accs)):
        incl = sums[2 * h * tq:(2 * h + 1) * tq] + sums[(2 * h + 1) * tq:(2 * h + 2) * tq]
        w = jnp.exp2(z - (incl + c) * LOG2E)
        if mask is not None:
            w = jnp.where(mask, w, 0.0)
        accs_out.append(acc + jnp.dot(w.astype(BF16), v, preferred_element_type=F32))
        cs_out.append(c + incl[:, 0:1])
    return tuple(cs_out), tuple(accs_out)


def _unfinished(cs):
    return (jnp.min(functools.reduce(jnp.minimum, cs)) < SB_DONE).astype(jnp.int32)


def _attn_prompt_kernel(q_ref, k_ref, v_ref, o_ref, kb_ref, vb_ref, u_ref, *, tq, heads, dh):
    qi = pl.program_id(2)

    @pl.when(qi == 0)
    def _():
        kb_ref[...] = k_ref[...].astype(BF16)
        vb_ref[...] = v_ref[...].astype(BF16)
        u_ref[...] = _upper_inclusive(tq)

    u = u_ref[...]
    cols = [slice(h * dh, (h + 1) * dh) for h in range(heads)]
    qs = [q_ref[:, cs] for cs in cols]

    def sweep(n, cs, accs, mask):
        rows = pl.ds(pl.multiple_of((qi - n) * tq, tq), tq)
        return _sb_blocks(qs, [kb_ref[rows, cs_] for cs_ in cols], [vb_ref[rows, cs_] for cs_ in cols], u, cs, accs,
                          mask)

    cs, accs = sweep(0, (jnp.zeros((tq, 1), F32),) * heads, (jnp.zeros((tq, dh), F32),) * heads,
                     _strictly_earlier(tq))

    def body(carry):
        n, _, cs, accs = carry
        cs, accs = sweep(n, cs, accs, None)
        return n + 1, _unfinished(cs), cs, accs

    _, _, _, accs = lax.while_loop(lambda carry: (carry[0] <= qi) & (carry[1] > 0), body,
                                   (jnp.int32(1), _unfinished(cs), cs, accs))
    for h in range(heads):
        o_ref[:, cols[h]] = accs[h].astype(o_ref.dtype)


def _attn_prompt(q, k_stack, v_stack, slot, n_seq, T, heads):
    N, D = q.shape
    dh = D // heads
    tq = min(ATTN_BLOCK, T)
    hp = _largest_divisor(heads, (ATTN_PROMPT_HEADS, 1))
    assert T % tq == 0
    n_slots = k_stack.shape[0]
    kv_spec = pl.BlockSpec((None, None, T, hp * dh), lambda b, h, i: (slot, b, 0, h))
    o = pl.pallas_call(
        functools.partial(_attn_prompt_kernel, tq=tq, heads=hp, dh=dh),
        grid=(n_seq, heads // hp, T // tq),
        in_specs=[pl.BlockSpec((None, tq, hp * dh), lambda b, h, i: (b, i, h)), kv_spec, kv_spec],
        out_specs=pl.BlockSpec((None, tq, hp * dh), lambda b, h, i: (b, i, h)),
        out_shape=jax.ShapeDtypeStruct((n_seq, T, D), BF16),
        scratch_shapes=[pltpu.VMEM((T, hp * dh), BF16), pltpu.VMEM((T, hp * dh), BF16),
                        pltpu.VMEM((tq, tq), BF16)],
        compiler_params=_params("parallel", "parallel", "arbitrary"),
        name="attn_prompt",
    )(q.reshape(n_seq, T, D), k_stack.reshape(n_slots, n_seq, T, D), v_stack.reshape(n_slots, n_seq, T, D))
    return o.reshape(N, D)


def _attn_sample_kernel(q_ref, kn_ref, vn_ref, ck_ref, cv_ref, o_ref, *, T, P, tk, heads, all_heads, dh):
    u_past = _upper_inclusive(tk)
    cols = [slice(h * dh, (h + 1) * dh) for h in range(heads)]
    qs = [q_ref[:, cs] for cs in cols]

    cs, accs = _sb_blocks(qs, [kn_ref[:, c].astype(BF16) for c in cols], [vn_ref[:, c].astype(BF16) for c in cols],
                          _upper_inclusive(T), (jnp.zeros((T, 1), F32),) * heads,
                          (jnp.zeros((T, dh), F32),) * heads, _strictly_earlier(T))

    h0 = pl.program_id(1) * heads

    def body(carry):
        n, _, cs, accs = carry
        r0 = (P - (n + 1) * tk) * all_heads + h0
        ks = [ck_ref[pl.ds(r0 + h, tk, stride=all_heads), :].astype(BF16) for h in range(heads)]
        vs = [cv_ref[pl.ds(r0 + h, tk, stride=all_heads), :].astype(BF16) for h in range(heads)]
        cs, accs = _sb_blocks(qs, ks, vs, u_past, cs, accs, None)
        return n + 1, _unfinished(cs), cs, accs

    _, _, _, accs = lax.while_loop(lambda carry: (carry[0] < P // tk) & (carry[1] > 0), body,
                                   (jnp.int32(0), _unfinished(cs), cs, accs))
    for h in range(heads):
        o_ref[:, cols[h]] = accs[h].astype(o_ref.dtype)


def _attn_sample(q, k_stack, v_stack, slot, cache_k, cache_v, n_seq, T, heads):
    N, D = q.shape
    dh = D // heads
    P = cache_k.shape[2]
    tk = _largest_divisor(P, (ATTN_BLOCK, 128))
    hg = _largest_divisor(heads, (ATTN_SAMPLE_HEADS, heads))
    n_slots = k_stack.shape[0]
    q_spec = pl.BlockSpec((None, T, hg * dh), lambda b, g: (b, 0, g))
    new_spec = pl.BlockSpec((None, None, T, hg * dh), lambda b, g: (slot, b, 0, g))
    past_spec = pl.BlockSpec((None, None, P * heads, dh), lambda b, g: (slot, b, 0, 0))
    past_shape = cache_k.shape[:2] + (P * heads, dh)
    o = pl.pallas_call(
        functools.partial(_attn_sample_kernel, T=T, P=P, tk=tk, heads=hg, all_heads=heads, dh=dh),
        grid=(n_seq, heads // hg),
        in_specs=[q_spec, new_spec, new_spec, past_spec, past_spec],
        out_specs=q_spec,
        out_shape=jax.ShapeDtypeStruct((n_seq, T, D), BF16),
        compiler_params=_params("parallel", "arbitrary"),
        name="attn_sample",
    )(q.reshape(n_seq, T, D), k_stack.reshape(n_slots, n_seq, T, D), v_stack.reshape(n_slots, n_seq, T, D),
      cache_k.reshape(past_shape), cache_v.reshape(past_shape))
    return o.reshape(N, D)


def _out_proj_kernel(o_ref, w_ref, x_ref, g1_ref, gn_ref, xo_ref, *, R):
    y = jnp.dot(o_ref[...], w_ref[...], preferred_element_type=F32)
    xo_ref[...] = x_ref[...] + _per_row(g1_ref[...], R) * _rms(y, gn_ref[...])


def _out_proj(o, w, wi, x, n_seq, T, mod, layer, g_norm):
    N, D = x.shape
    S, R = _split_rows(n_seq, T, ROWS_OUT_PROJ)
    rows = S * R
    return pl.pallas_call(
        functools.partial(_out_proj_kernel, R=R),
        grid=(N // rows,),
        in_specs=[pl.BlockSpec((rows, D), lambda i: (i, 0)),
                  pl.BlockSpec((None, D, D), lambda i: (wi, 0, 0)),
                  pl.BlockSpec((rows, D), lambda i: (i, 0)),
                  pl.BlockSpec((None, S, 1, D), lambda i: (layer, _seq_block(i, S, R, T), 0, 2)),
                  pl.BlockSpec((1, D), lambda i: (0, 0))],
        out_specs=pl.BlockSpec((rows, D), lambda i: (i, 0)),
        out_shape=jax.ShapeDtypeStruct((N, D), F32),
        compiler_params=_params("parallel"),
        name="out_proj",
    )(o, w, x, mod, g_norm.reshape(1, D))


def _ffn_kernel(x_ref, gn_in_ref, sc_ref, sh_ref, wa_ref, wg_ref, wc_ref, bc_ref, wd_ref, st_ref, g2_ref,
                gn_out_ref, xo_ref, tail_ref, h_ref, acc_ref, carry_ref, *, S, R, T, piece):
    i, j = pl.program_id(0), pl.program_id(1)
    tf = wa_ref.shape[1]

    @pl.when(j == 0)
    def _():
        h = _rms(x_ref[...], gn_in_ref[...]) * (1.0 + _per_row(sc_ref[...], R)) + _per_row(sh_ref[...], R)
        h_ref[...] = h.astype(BF16)
        acc_ref[...] = jnp.zeros_like(acc_ref)

    if R < T:
        @pl.when(i == 0)
        def _():
            carry_ref[j] = jnp.zeros((SUBLANES, tf), F32)

    h = h_ref[...]
    pieces = [slice(p, p + piece) for p in range(0, tf, piece)]
    ups = [(jnp.dot(h, wa_ref[:, cs], preferred_element_type=F32),
            jnp.dot(h, wg_ref[:, cs], preferred_element_type=F32)) for cs in pieces]
    contrib = None
    for cs, (a, gate) in zip(pieces, ups):
        tail = a.reshape(S, R, piece)[:, R - SUBLANES:R, :]
        tail_ref[:, :, cs] = tail
        p0, p1 = st_ref[:, 0:1, cs], st_ref[:, 1:2, cs]
        if R < T:
            first = (i % (T // R)) == 0
            prev = carry_ref[j, :, cs]
            p0 = jnp.where(first, p0, prev[SUBLANES - 2:SUBLANES - 1][None])
            p1 = jnp.where(first, p1, prev[SUBLANES - 1:SUBLANES][None])
            carry_ref[j, :, cs] = tail[0]
        conv = _causal_conv(a, _per_row(p0, R), _per_row(p1, R), wc_ref[:, cs], R) + bc_ref[:, cs]
        act = (conv * _sigmoid(conv) * gate).astype(BF16)
        part = jnp.dot(act, wd_ref[cs, :], preferred_element_type=F32)
        contrib = part if contrib is None else contrib + part

    acc_ref[...] += contrib

    @pl.when(j == pl.num_programs(1) - 1)
    def _():
        xo_ref[...] = x_ref[...] + _per_row(g2_ref[...], R) * _rms(acc_ref[...], gn_out_ref[...])


def _ffn(x, n_seq, T, state, mod, layer, g_in, w_up, w_conv, b_conv, w_down, g_out):
    N, D = x.shape
    L, F = w_down.shape[0], w_down.shape[1]
    S, R = _split_rows(n_seq, T, ROWS_FFN)
    rows = S * R
    tf = _largest_divisor(F, (COLS_FFN, 256, 128))
    nf = F // tf
    piece = min(MXU_COLS, tf)

    def mod_spec(chunk):
        return pl.BlockSpec((None, S, 1, D), lambda i, j: (layer, _seq_block(i, S, R, T), 0, chunk))

    return pl.pallas_call(
        functools.partial(_ffn_kernel, S=S, R=R, T=T, piece=piece),
        grid=(N // rows, nf),
        in_specs=[pl.BlockSpec((rows, D), lambda i, j: (i, 0)),
                  pl.BlockSpec((1, D), lambda i, j: (0, 0)),
                  mod_spec(4), mod_spec(3),
                  pl.BlockSpec((None, D, tf), lambda i, j: (layer, 0, j)),
                  pl.BlockSpec((None, D, tf), lambda i, j: (layer, 0, nf + j)),
                  pl.BlockSpec((None, CONV_TAPS, tf), lambda i, j: (layer, 0, j)),
                  pl.BlockSpec((None, 1, tf), lambda i, j: (layer, 0, j)),
                  pl.BlockSpec((None, tf, D), lambda i, j: (layer, j, 0)),
                  pl.BlockSpec((None, S, CONV_TAPS - 1, tf), lambda i, j: (layer, _seq_block(i, S, R, T), 0, j)),
                  mod_spec(5),
                  pl.BlockSpec((1, D), lambda i, j: (0, 0))],
        out_specs=[pl.BlockSpec((rows, D), lambda i, j: (i, 0)),
                   pl.BlockSpec((S, SUBLANES, tf), lambda i, j: (i, 0, j))],
        out_shape=[jax.ShapeDtypeStruct((N, D), F32),
                   jax.ShapeDtypeStruct((N // rows * S, SUBLANES, F), F32)],
        scratch_shapes=[pltpu.VMEM((rows, D), BF16), pltpu.VMEM((rows, D), F32),
                        pltpu.VMEM((nf, SUBLANES, tf), F32)],
        compiler_params=_params("arbitrary", "arbitrary"),
        name="ffn",
    )(x, g_in.reshape(1, D), mod, mod, w_up, w_up, w_conv, b_conv.reshape(L, 1, F), w_down, state, mod,
      g_out.reshape(1, D))


def _last_rows(tails, n_seq):
    t = tails.reshape(n_seq, tails.shape[0] // n_seq, SUBLANES, tails.shape[2])
    return t[:, -1, SUBLANES - (CONV_TAPS - 1):, :]


def _trunk(x, mod, cache_k, cache_v, conv_a_prev, ffn_prev, norm_g, w_in_ab, w_conv_a, g_sgu, w_sgu, b_sgu,
           w_out_ab, w_qkv_sb, w_o_sb, w_ffn_up, w_ffn_conv, b_ffn_conv, w_ffn_down, heads, emit_vn):
    n_seq, T, D = x.shape
    x = x.reshape(n_seq * T, D)
    depth = norm_g.shape[0]
    n_odd = w_qkv_sb.shape[0]
    q_scale = (D // heads) ** -0.5 * LOG2E
    kv = None
    conv_a_new, ffn_new, sgu_v_new = [], [], []
    for l in range(depth):
        i = l // 2
        if l % 2 == 0:
            (z,) = _in_proj(x, n_seq, T, norm_g[l, 0], mod, l, 1, 0, w_in_ab, i, (F32,), (1.0,))
            res = _mix_even(z, x, n_seq, T, conv_a_prev, mod, l, i, w_conv_a, g_sgu, w_sgu, b_sgu, w_out_ab,
                            norm_g[l, 1], emit_vn)
            x = res[0]
            conv_a_new.append(_last_rows(res[1], n_seq))
            if emit_vn:
                sgu_v_new.append(res[2].reshape(n_seq, T, -1))
        else:
            q, *kv = _in_proj(x, n_seq, T, norm_g[l, 0], mod, l, 1, 0, w_qkv_sb, i, (BF16, F32, F32),
                              (q_scale, 1.0, 1.0), stacks=kv, slot=i, n_slots=n_odd)
            if cache_k is None:
                o = _attn_prompt(q, kv[0], kv[1], i, n_seq, T, heads)
            else:
                o = _attn_sample(q, kv[0], kv[1], i, cache_k, cache_v, n_seq, T, heads)
            x = _out_proj(o, w_o_sb, i, x, n_seq, T, mod, l, norm_g[l, 1])
        x, tails = _ffn(x, n_seq, T, ffn_prev, mod, l, norm_g[l, 2], w_ffn_up, w_ffn_conv, b_ffn_conv, w_ffn_down,
                        norm_g[l, 3])
        ffn_new.append(_last_rows(tails, n_seq))
    k_new, v_new = (a.reshape(n_odd, n_seq, T, heads, D // heads) for a in kv)
    return x.reshape(n_seq, T, D), k_new, v_new, jnp.stack(conv_a_new), jnp.stack(ffn_new), sgu_v_new


def kernel(x_prompt, x_sample, c_prompt, c_sample, cache_sb_k, cache_sb_v, state_conv_a, state_ffn_conv, w_mod, b_mod,
           norm_g, w_in_ab, w_conv_a, g_sgu, w_sgu, b_sgu, w_out_ab, w_qkv_sb, w_o_sb, w_ffn_up, w_ffn_conv,
           b_ffn_conv, w_ffn_down):
    bp, bs = x_prompt.shape[0], x_sample.shape[0]
    depth = norm_g.shape[0]
    heads = cache_sb_k.shape[3]

    mod = _modulation(jnp.concatenate([c_prompt, c_sample], axis=0), w_mod, b_mod)
    mod_p = mod[:, :bp].reshape(depth, bp, 1, -1)
    mod_s = mod[:, bp:].reshape(depth, bs, 1, -1)

    weights = (norm_g, w_in_ab.astype(BF16), w_conv_a, g_sgu, w_sgu, b_sgu, w_out_ab.astype(BF16),
               w_qkv_sb.astype(BF16), w_o_sb.astype(BF16), w_ffn_up.astype(BF16), w_ffn_conv, b_ffn_conv,
               w_ffn_down.astype(BF16))

    zeros_a = jnp.zeros((state_conv_a.shape[0], bp) + state_conv_a.shape[2:], F32)
    zeros_f = jnp.zeros((depth, bp) + state_ffn_conv.shape[2:], F32)
    y_p, k_p, v_p, conv_a_p, ffn_p, _ = _trunk(x_prompt, mod_p, None, None, zeros_a, zeros_f, *weights,
                                               heads=heads, emit_vn=False)
    y_s, k_s, v_s, conv_a_s, ffn_s, sgu_v = _trunk(x_sample, mod_s, cache_sb_k, cache_sb_v, state_conv_a,
                                                   state_ffn_conv, *weights, heads=heads, emit_vn=True)
    return (y_p, y_s, k_p, v_p, conv_a_p, ffn_p, k_s, v_s, conv_a_s, ffn_s, jnp.stack(sgu_v))
```

```python
import functools
import math

import jax
import jax.numpy as jnp
from jax import lax
from jax.experimental import pallas as pl
from jax.experimental.pallas import tpu as pltpu

F32 = jnp.float32
BF16 = jnp.bfloat16

EPS = 1e-6
CONV_TAPS = 3
GMLP_CAUSAL_BLOCK = 64
SUBLANES = 8
VMEM_LIMIT_BYTES = 56 * 1024 * 1024

ROWS_IN_PROJ = 1024
ROWS_MIX = 256
ROWS_OUT_PROJ = 512
ROWS_FFN = 512
COLS_IN_PROJ = 512
COLS_FFN = 512
COLS_MOD = 1024
ATTN_BLOCK = 256
ATTN_PROMPT_HEADS = 4
ATTN_SAMPLE_HEADS = 8

LOG2E = math.log2(math.e)
SB_Z_MAX = 126.0
SB_DONE = 150.0 / LOG2E


def _params(*semantics):
    return pltpu.CompilerParams(dimension_semantics=semantics, vmem_limit_bytes=VMEM_LIMIT_BYTES)


def _split_rows(n_seq, seq_len, rows):
    if seq_len >= rows:
        assert seq_len % rows == 0
        return 1, rows
    s = min(rows // seq_len, n_seq)
    assert n_seq % s == 0 and seq_len % SUBLANES == 0
    return s, seq_len


def _seq_block(i, S, R, T):
    return i // (T // R) if S == 1 else i


def _largest_divisor(n, candidates):
    for c in candidates:
        if n % c == 0:
            return c
    raise ValueError(f"no block size among {candidates} divides {n}")


def _rms(x, g):
    return x * lax.rsqrt(jnp.mean(x * x, axis=-1, keepdims=True) + EPS) * g


def _per_row(v, R):
    S, _, C = v.shape
    if S == 1:
        return v[0]
    return jnp.broadcast_to(v, (S, R, C)).reshape(S * R, C)


def _sigmoid(x):
    return 1.0 / (1.0 + jnp.exp(-x))


def _conv_taps(a, p0, p1, w, R):
    rt = lax.broadcasted_iota(jnp.int32, a.shape, 0) & (R - 1)
    s1 = jnp.where(rt == 0, p1, pltpu.roll(a, 1, 0))
    s2 = jnp.where(rt == 0, p0, jnp.where(rt == 1, p1, pltpu.roll(a, 2, 0)))
    return s2 * w[0:1] + s1 * w[1:2] + a * w[2:3]


def _causal_conv(a, p0, p1, w, R, S):
    if S > 1:
        return _conv_taps(a, p0, p1, w, R)
    body = pltpu.roll(a, 2, 0) * w[0:1] + pltpu.roll(a, 1, 0) * w[1:2] + a * w[2:3]
    head = _conv_taps(a[0:SUBLANES], p0, p1, w, R)
    return jnp.concatenate([head, body[SUBLANES:]], axis=0)


NORM_ROWS = 128


def _row_chunks(ref_sc, c, S, R, step):
    if S == 1:
        return ref_sc[0]
    k = step // R
    return _per_row(ref_sc[pl.ds(c * k, k)], R)


def _norm_mod_rows(x_ref, g, sc_ref, sh_ref, h_ref, S, R):
    rows = S * R
    step = min(NORM_ROWS, rows)
    assert rows % step == 0 and (S == 1 or step % R == 0)

    def body(c, carry):
        rs = pl.ds(pl.multiple_of(c * step, step), step)
        h = _rms(x_ref[rs, :], g) * (1.0 + _row_chunks(sc_ref, c, S, R, step)) + _row_chunks(sh_ref, c, S, R, step)
        h_ref[rs, :] = h.astype(BF16)
        return carry

    lax.fori_loop(0, rows // step, body, 0)


def _residual_norm_rows(x_ref, y_ref, gate_ref, g, xo_ref, S, R):
    rows = S * R
    step = min(NORM_ROWS, rows)
    assert rows % step == 0 and (S == 1 or step % R == 0)

    def body(c, carry):
        rs = pl.ds(pl.multiple_of(c * step, step), step)
        xo_ref[rs, :] = x_ref[rs, :] + _row_chunks(gate_ref, c, S, R, step) * _rms(y_ref[rs, :], g)
        return carry

    lax.fori_loop(0, rows // step, body, 0)


def _mod_kernel(c_ref, w_ref, b_ref, o_ref):
    c = c_ref[...]
    s = (c * _sigmoid(c)).astype(BF16)
    o_ref[...] = jnp.dot(s, w_ref[...].astype(BF16), preferred_element_type=F32) + b_ref[...]


def _modulation(c, w_mod, b_mod):
    L, D, N = w_mod.shape
    B = c.shape[0]
    tn = _largest_divisor(N, (COLS_MOD, 512, 256, 128))
    return pl.pallas_call(
        _mod_kernel,
        grid=(L, N // tn),
        in_specs=[pl.BlockSpec((B, D), lambda l, j: (0, 0)),
                  pl.BlockSpec((None, D, tn), lambda l, j: (l, 0, j)),
                  pl.BlockSpec((None, 1, tn), lambda l, j: (l, 0, j))],
        out_specs=pl.BlockSpec((None, B, tn), lambda l, j: (l, 0, j)),
        out_shape=jax.ShapeDtypeStruct((L, B, N), F32),
        compiler_params=_params("parallel", "parallel"),
        name="modulation",
    )(c, w_mod, b_mod.reshape(L, 1, N))


def _in_proj_kernel(x_ref, g_ref, sc_ref, sh_ref, w_ref, *rest, S, R, blocks_per_out, n_out, out_scales):
    outs, h_ref = rest[-1 - n_out:-1], rest[-1]
    j = pl.program_id(1)

    @pl.when(j == 0)
    def _():
        _norm_mod_rows(x_ref, g_ref[...], sc_ref, sh_ref, h_ref, S, R)

    y = jnp.dot(h_ref[...], w_ref[...], preferred_element_type=F32)
    if n_out == 1:
        outs[0][...] = (y * out_scales[0]).astype(outs[0].dtype)
    else:
        for k, o_ref in enumerate(outs):
            @pl.when(j // blocks_per_out == k)
            def _(o_ref=o_ref, k=k):
                o_ref[...] = (y if out_scales[k] == 1.0 else y * out_scales[k]).astype(o_ref.dtype)


def _in_proj(x, n_seq, T, g, mod, layer, sc_chunk, sh_chunk, w, wi, out_dtypes, out_scales, stacks=None, slot=0,
             n_slots=1):
    N, D = x.shape
    n_total = w.shape[2]
    n_out = len(out_dtypes)
    S, R = _split_rows(n_seq, T, ROWS_IN_PROJ)
    rows = S * R
    width = n_total // n_out
    tn = _largest_divisor(width, (COLS_IN_PROJ, 256, 128))
    bpo = width // tn

    def mod_spec(chunk):
        return pl.BlockSpec((None, S, 1, D), lambda i, j: (layer, _seq_block(i, S, R, T), 0, chunk))

    def col(j, k):
        return jnp.clip(j - k * bpo, 0, bpo - 1)

    out_specs = [pl.BlockSpec((rows, tn), lambda i, j: (i, col(j, 0)))]
    out_shape = [jax.ShapeDtypeStruct((N, width), out_dtypes[0])]
    for k in range(1, n_out):
        out_specs.append(pl.BlockSpec((None, rows, tn), functools.partial(lambda i, j, k: (slot, i, col(j, k)), k=k)))
        out_shape.append(jax.ShapeDtypeStruct((n_slots, N, width), out_dtypes[k]))
    in_specs = [pl.BlockSpec((rows, D), lambda i, j: (i, 0)),
                pl.BlockSpec((1, D), lambda i, j: (0, 0)),
                mod_spec(sc_chunk), mod_spec(sh_chunk),
                pl.BlockSpec((None, D, tn), lambda i, j: (wi, 0, j))]
    args = [x, g.reshape(1, D), mod, mod, w]
    aliases = {}
    if stacks is not None:
        for k, st in enumerate(stacks):
            aliases[len(args)] = k + 1
            in_specs.append(pl.BlockSpec(memory_space=pl.ANY))
            args.append(st)
    return pl.pallas_call(
        functools.partial(_in_proj_kernel, S=S, R=R, blocks_per_out=bpo, n_out=n_out, out_scales=tuple(out_scales)),
        grid=(N // rows, n_total // tn),
        in_specs=in_specs, out_specs=out_specs, out_shape=out_shape,
        scratch_shapes=[pltpu.VMEM((rows, D), BF16)],
        input_output_aliases=aliases,
        compiler_params=_params("parallel", "arbitrary"),
        name="in_proj",
    )(*args)


def _sgu_matrices(ws_ref, bs_ref, g, T, chunk, whole_chunks):
    w = ws_ref[g]
    b = jnp.broadcast_to(bs_ref[g], (chunk, chunk))
    ii = lax.broadcasted_iota(jnp.int32, (chunk, chunk), 0)
    jj = lax.broadcasted_iota(jnp.int32, (chunk, chunk), 1)
    if whole_chunks:
        return jnp.where(jj // GMLP_CAUSAL_BLOCK <= ii // GMLP_CAUSAL_BLOCK, w, 0.0), b
    w_tl = jnp.where((ii < T) & (jj < T), w, 0.0)
    b_tl = jnp.where(ii < T, b, 0.0)
    wm, bm = w_tl, b_tl
    for k in range(1, chunk // T):
        wm = wm + pltpu.roll(pltpu.roll(w_tl, k * T, 0), k * T, 1)
        bm = bm + pltpu.roll(b_tl, k * T, 0)
    return wm, bm


def _mix_even_kernel(*refs, S, R, T, chunk, groups, halo, emit_vn):
    refs = list(refs)
    xa_ref, gb_ref, gc_ref, u_ref, v_ref = refs[:5]
    del refs[:5]
    if halo:
        hxa_ref, hgc_ref = refs[:2]
        del refs[:2]
    st_ref, x_ref, g1_ref, wc_ref, gs_ref, ws_ref, bs_ref, wo_ref, gn_ref = refs[:9]
    del refs[:9]
    xo_ref, tail_ref = refs[:2]
    del refs[:2]
    if emit_vn:
        vn_out_ref = refs.pop(0)
    cat_ref, vn_ref = refs

    i = pl.program_id(0)
    rows = S * R
    C = xa_ref.shape[1]

    ca = gc_ref[...] * xa_ref[...]
    st = st_ref[...]
    p0, p1 = st[:, 0:1, :], st[:, 1:2, :]
    if halo:
        hca = hgc_ref[...] * hxa_ref[...]
        first = (i % (T // R)) == 0
        p0 = jnp.where(first, p0, hca[SUBLANES - 2:SUBLANES - 1][None])
        p1 = jnp.where(first, p1, hca[SUBLANES - 1:SUBLANES][None])
    conv = _causal_conv(ca, _per_row(p0, R), _per_row(p1, R), wc_ref[...], R, S)
    cat_ref[:, 0:C] = (gb_ref[...] * conv).astype(BF16)
    tail_ref[...] = ca.reshape(S, R, C)[:, R - SUBLANES:R, :]

    vn = _rms(v_ref[...], gs_ref[...])
    if emit_vn:
        vn_out_ref[...] = vn
    vn_ref[...] = vn.astype(BF16)
    gd = v_ref.shape[1] // groups
    for g in range(groups):
        wm, bm = _sgu_matrices(ws_ref, bs_ref, g, T, chunk, whole_chunks=(R % chunk == 0))
        wm = wm.astype(BF16)
        cs = slice(g * gd, (g + 1) * gd)
        for c in range(rows // chunk):
            rs = slice(c * chunk, (c + 1) * chunk)
            mixed = jnp.dot(wm, vn_ref[rs, cs], preferred_element_type=F32) + bm
            cat_ref[rs, C + g * gd:C + (g + 1) * gd] = (u_ref[rs, cs] * mixed).astype(BF16)

    y = jnp.dot(cat_ref[...], wo_ref[...], preferred_element_type=F32)
    xo_ref[...] = x_ref[...] + _per_row(g1_ref[...], R) * _rms(y, gn_ref[...])


def _mix_even(z, x, n_seq, T, state, mod, layer, wi, w_conv, g_sgu, w_sgu, b_sgu, w_out, g_norm, emit_vn):
    N, D = x.shape
    C = w_conv.shape[2]
    groups, chunk = w_sgu.shape[1], w_sgu.shape[2]
    Cb = g_sgu.shape[1]
    S, R = _split_rows(n_seq, T, ROWS_MIX)
    rows = S * R
    halo = R < T
    assert rows % chunk == 0 and (R % chunk == 0 or (chunk % T == 0 and T <= GMLP_CAUSAL_BLOCK))
    assert z.shape[1] == 3 * C + 2 * Cb and C == Cb

    in_specs = [pl.BlockSpec((rows, C), functools.partial(lambda i, k: (i, k), k=k)) for k in range(5)]
    args = [z] * 5
    if halo:
        per = R // SUBLANES
        in_specs += [pl.BlockSpec((SUBLANES, C), functools.partial(
            lambda i, k: (jnp.maximum(i * per - 1, 0), k), k=k)) for k in (0, 2)]
        args += [z, z]
    in_specs += [
        pl.BlockSpec((None, S, CONV_TAPS - 1, C), lambda i: (wi, _seq_block(i, S, R, T), 0, 0)),
        pl.BlockSpec((rows, D), lambda i: (i, 0)),
        pl.BlockSpec((None, S, 1, D), lambda i: (layer, _seq_block(i, S, R, T), 0, 2)),
        pl.BlockSpec((None, CONV_TAPS, C), lambda i: (wi, 0, 0)),
        pl.BlockSpec((None, 1, Cb), lambda i: (wi, 0, 0)),
        pl.BlockSpec((None, groups, chunk, chunk), lambda i: (wi, 0, 0, 0)),
        pl.BlockSpec((None, groups, chunk, 1), lambda i: (wi, 0, 0, 0)),
        pl.BlockSpec((None, C + Cb, D), lambda i: (wi, 0, 0)),
        pl.BlockSpec((1, D), lambda i: (0, 0)),
    ]
    n_even = w_conv.shape[0]
    args += [state, x, mod, w_conv, g_sgu.reshape(n_even, 1, Cb), w_sgu, b_sgu.reshape(n_even, groups, chunk, 1),
             w_out, g_norm.reshape(1, D)]
    out_specs = [pl.BlockSpec((rows, D), lambda i: (i, 0)),
                 pl.BlockSpec((S, SUBLANES, C), lambda i: (i, 0, 0))]
    out_shape = [jax.ShapeDtypeStruct((N, D), F32),
                 jax.ShapeDtypeStruct((N // rows * S, SUBLANES, C), F32)]
    if emit_vn:
        out_specs.append(pl.BlockSpec((rows, Cb), lambda i: (i, 0)))
        out_shape.append(jax.ShapeDtypeStruct((N, Cb), F32))
    return pl.pallas_call(
        functools.partial(_mix_even_kernel, S=S, R=R, T=T, chunk=chunk, groups=groups, halo=halo, emit_vn=emit_vn),
        grid=(N // rows,),
        in_specs=in_specs, out_specs=out_specs, out_shape=out_shape,
        scratch_shapes=[pltpu.VMEM((rows, C + Cb), BF16), pltpu.VMEM((rows, Cb), BF16)],
        compiler_params=_params("parallel"),
        name="mix_even",
    )(*args)


def _upper_inclusive(n):
    r = lax.broadcasted_iota(jnp.int32, (n, n), 0)
    c = lax.broadcasted_iota(jnp.int32, (n, n), 1)
    return jnp.where(r >= c, 1.0, 0.0).astype(BF16)


def _strictly_earlier(n):
    t = lax.broadcasted_iota(jnp.int32, (n, n), 0)
    s = lax.broadcasted_iota(jnp.int32, (n, n), 1)
    return s < t


def _sb_blocks(qs, ks, vs, u_inc, cs, accs, mask):
    tq = qs[0].shape[0]
    zs, parts = [], []
    for q, k in zip(qs, ks):
        z = jnp.minimum(lax.dot_general(q, k, (((1,), (1,)), ((), ())), preferred_element_type=F32), SB_Z_MAX)
        fail = jnp.log(1.0 + jnp.exp2(z))
        if mask is not None:
            fail = jnp.where(mask, fail, 0.0)
        hi = fail.astype(BF16)
        parts += [hi, (fail - hi.astype(F32)).astype(BF16)]
        zs.append(z)
    sums = jnp.dot(jnp.concatenate(parts, axis=0), u_inc, preferred_element_type=F32)
    cs_out, accs_out = [], []
    for h, (z, v, c, acc) in enumerate(zip(zs, vs, cs, accs)):
        incl = sums[2 * h * tq:(2 * h + 1) * tq] + sums[(2 * h + 1) * tq:(2 * h + 2) * tq]
        w = jnp.exp2(z - (incl + c) * LOG2E)
        if mask is not None:
            w = jnp.where(mask, w, 0.0)
        accs_out.append(acc + jnp.dot(w.astype(BF16), v, preferred_element_type=F32))
        cs_out.append(c + incl[:, 0:1])
    return tuple(cs_out), tuple(accs_out)


def _unfinished(cs):
    return (jnp.min(functools.reduce(jnp.minimum, cs)) < SB_DONE).astype(jnp.int32)


def _attn_prompt_kernel(q_ref, k_ref, v_ref, o_ref, kb_ref, vb_ref, u_ref, *, tq, heads, dh):
    qi = pl.program_id(2)

    @pl.when(qi == 0)
    def _():
        kb_ref[...] = k_ref[...].astype(BF16)
        vb_ref[...] = v_ref[...].astype(BF16)
        u_ref[...] = _upper_inclusive(tq)

    u = u_ref[...]
    cols = [slice(h * dh, (h + 1) * dh) for h in range(heads)]
    qs = [q_ref[:, cs] for cs in cols]

    def sweep(n, cs, accs, mask):
        rows = pl.ds(pl.multiple_of((qi - n) * tq, tq), tq)
        return _sb_blocks(qs, [kb_ref[rows, cs_] for cs_ in cols], [vb_ref[rows, cs_] for cs_ in cols], u, cs, accs,
                          mask)

    cs, accs = sweep(0, (jnp.zeros((tq, 1), F32),) * heads, (jnp.zeros((tq, dh), F32),) * heads,
                     _strictly_earlier(tq))

    def body(carry):
        n, _, cs, accs = carry
        cs, accs = sweep(n, cs, accs, None)
        return n + 1, _unfinished(cs), cs, accs

    _, _, _, accs = lax.while_loop(lambda carry: (carry[0] <= qi) & (carry[1] > 0), body,
                                   (jnp.int32(1), _unfinished(cs), cs, accs))
    for h in range(heads):
        o_ref[:, cols[h]] = accs[h].astype(o_ref.dtype)


def _attn_prompt(q, k_stack, v_stack, slot, n_seq, T, heads):
    N, D = q.shape
    dh = D // heads
    tq = min(ATTN_BLOCK, T)
    hp = _largest_divisor(heads, (ATTN_PROMPT_HEADS, 1))
    assert T % tq == 0
    n_slots = k_stack.shape[0]
    kv_spec = pl.BlockSpec((None, None, T, hp * dh), lambda b, h, i: (slot, b, 0, h))
    o = pl.pallas_call(
        functools.partial(_attn_prompt_kernel, tq=tq, heads=hp, dh=dh),
        grid=(n_seq, heads // hp, T // tq),
        in_specs=[pl.BlockSpec((None, tq, hp * dh), lambda b, h, i: (b, i, h)), kv_spec, kv_spec],
        out_specs=pl.BlockSpec((None, tq, hp * dh), lambda b, h, i: (b, i, h)),
        out_shape=jax.ShapeDtypeStruct((n_seq, T, D), BF16),
        scratch_shapes=[pltpu.VMEM((T, hp * dh), BF16), pltpu.VMEM((T, hp * dh), BF16),
                        pltpu.VMEM((tq, tq), BF16)],
        compiler_params=_params("parallel", "parallel", "arbitrary"),
        name="attn_prompt",
    )(q.reshape(n_seq, T, D), k_stack.reshape(n_slots, n_seq, T, D), v_stack.reshape(n_slots, n_seq, T, D))
    return o.reshape(N, D)


def _attn_sample_kernel(q_ref, kn_ref, vn_ref, ck_ref, cv_ref, o_ref, *, T, P, tk, heads, all_heads, dh):
    u_past = _upper_inclusive(tk)
    cols = [slice(h * dh, (h + 1) * dh) for h in range(heads)]
    qs = [q_ref[:, cs] for cs in cols]

    cs, accs = _sb_blocks(qs, [kn_ref[:, c].astype(BF16) for c in cols], [vn_ref[:, c].astype(BF16) for c in cols],
                          _upper_inclusive(T), (jnp.zeros((T, 1), F32),) * heads,
                          (jnp.zeros((T, dh), F32),) * heads, _strictly_earlier(T))

    h0 = pl.program_id(1) * heads

    def body(carry):
        n, _, cs, accs = carry
        r0 = (P - (n + 1) * tk) * all_heads + h0
        ks = [ck_ref[pl.ds(r0 + h, tk, stride=all_heads), :].astype(BF16) for h in range(heads)]
        vs = [cv_ref[pl.ds(r0 + h, tk, stride=all_heads), :].astype(BF16) for h in range(heads)]
        cs, accs = _sb_blocks(qs, ks, vs, u_past, cs, accs, None)
        return n + 1, _unfinished(cs), cs, accs

    _, _, _, accs = lax.while_loop(lambda carry: (carry[0] < P // tk) & (carry[1] > 0), body,
                                   (jnp.int32(0), _unfinished(cs), cs, accs))
    for h in range(heads):
        o_ref[:, cols[h]] = accs[h].astype(o_ref.dtype)


def _attn_sample(q, k_stack, v_stack, slot, cache_k, cache_v, n_seq, T, heads):
    N, D = q.shape
    dh = D // heads
    P = cache_k.shape[2]
    tk = _largest_divisor(P, (ATTN_BLOCK, 128))
    hg = _largest_divisor(heads, (ATTN_SAMPLE_HEADS, heads))
    n_slots = k_stack.shape[0]
    q_spec = pl.BlockSpec((None, T, hg * dh), lambda b, g: (b, 0, g))
    new_spec = pl.BlockSpec((None, None, T, hg * dh), lambda b, g: (slot, b, 0, g))
    past_spec = pl.BlockSpec((None, None, P * heads, dh), lambda b, g: (slot, b, 0, 0))
    past_shape = cache_k.shape[:2] + (P * heads, dh)
    o = pl.pallas_call(
        functools.partial(_attn_sample_kernel, T=T, P=P, tk=tk, heads=hg, all_heads=heads, dh=dh),
        grid=(n_seq, heads // hg),
        in_specs=[q_spec, new_spec, new_spec, past_spec, past_spec],
        out_specs=q_spec,
        out_shape=jax.ShapeDtypeStruct((n_seq, T, D), BF16),
        compiler_params=_params("parallel", "arbitrary"),
        name="attn_sample",
    )(q.reshape(n_seq, T, D), k_stack.reshape(n_slots, n_seq, T, D), v_stack.reshape(n_slots, n_seq, T, D),
      cache_k.reshape(past_shape), cache_v.reshape(past_shape))
    return o.reshape(N, D)


def _out_proj_kernel(o_ref, w_ref, x_ref, g1_ref, gn_ref, xo_ref, *, R):
    y = jnp.dot(o_ref[...], w_ref[...], preferred_element_type=F32)
    xo_ref[...] = x_ref[...] + _per_row(g1_ref[...], R) * _rms(y, gn_ref[...])


def _out_proj(o, w, wi, x, n_seq, T, mod, layer, g_norm):
    N, D = x.shape
    S, R = _split_rows(n_seq, T, ROWS_OUT_PROJ)
    rows = S * R
    return pl.pallas_call(
        functools.partial(_out_proj_kernel, R=R),
        grid=(N // rows,),
        in_specs=[pl.BlockSpec((rows, D), lambda i: (i, 0)),
                  pl.BlockSpec((None, D, D), lambda i: (wi, 0, 0)),
                  pl.BlockSpec((rows, D), lambda i: (i, 0)),
                  pl.BlockSpec((None, S, 1, D), lambda i: (layer, _seq_block(i, S, R, T), 0, 2)),
                  pl.BlockSpec((1, D), lambda i: (0, 0))],
        out_specs=pl.BlockSpec((rows, D), lambda i: (i, 0)),
        out_shape=jax.ShapeDtypeStruct((N, D), F32),
        compiler_params=_params("parallel"),
        name="out_proj",
    )(o, w, x, mod, g_norm.reshape(1, D))


def _ffn_kernel(x_ref, gn_in_ref, sc_ref, sh_ref, wa_ref, wg_ref, wc_ref, bc_ref, wd_ref, st_ref, g2_ref,
                gn_out_ref, xo_ref, tail_ref, h_ref, acc_ref, act0_ref, act1_ref, carry_ref, *, S, R, T, nf):
    i, j = pl.program_id(0), pl.program_id(1)
    acts = (act0_ref, act1_ref)

    def up_gate(dst_ref):
        h = h_ref[...]
        a = jnp.dot(h, wa_ref[...], preferred_element_type=F32)
        gate = jnp.dot(h, wg_ref[...], preferred_element_type=F32)
        tail = a.reshape(S, R, -1)[:, R - SUBLANES:R, :]
        tail_ref[...] = tail
        p0, p1 = st_ref[:, 0:1, :], st_ref[:, 1:2, :]
        if R < T:
            first = (i % (T // R)) == 0
            prev = carry_ref[j]
            p0 = jnp.where(first, p0, prev[SUBLANES - 2:SUBLANES - 1][None])
            p1 = jnp.where(first, p1, prev[SUBLANES - 1:SUBLANES][None])
            carry_ref[j] = tail[0]
        half = 0.5 * (_causal_conv(a, _per_row(p0, R), _per_row(p1, R), wc_ref[...], R, S) + bc_ref[...])
        dst_ref[...] = ((half + half * jnp.tanh(half)) * gate).astype(BF16)

    def down(src_ref):
        acc_ref[...] += jnp.dot(src_ref[...], wd_ref[...], preferred_element_type=F32)

    @pl.when(j == 0)
    def _():
        if R < T:
            @pl.when(i == 0)
            def _():
                carry_ref[...] = jnp.zeros_like(carry_ref)
        _norm_mod_rows(x_ref, gn_in_ref[...], sc_ref, sh_ref, h_ref, S, R)
        acc_ref[...] = jnp.zeros_like(acc_ref)
        up_gate(acts[0])

    for parity in range(2):
        @pl.when((j > 0) & (j < nf) & (j % 2 == parity))
        def _(parity=parity):
            up_gate(acts[parity])
            down(acts[1 - parity])

    @pl.when(j == nf)
    def _():
        down(acts[(nf - 1) % 2])
        _residual_norm_rows(x_ref, acc_ref, g2_ref, gn_out_ref[...], xo_ref, S, R)


def _ffn(x, n_seq, T, state, mod, layer, g_in, w_up, w_conv, b_conv, w_down, g_out):
    N, D = x.shape
    L, F = w_down.shape[0], w_down.shape[1]
    S, R = _split_rows(n_seq, T, ROWS_FFN)
    rows = S * R
    tf = _largest_divisor(F, (COLS_FFN, 256, 128))
    nf = F // tf

    def mod_spec(chunk):
        return pl.BlockSpec((None, S, 1, D), lambda i, j: (layer, _seq_block(i, S, R, T), 0, chunk))

    def up_chunk(j):
        return jnp.minimum(j, nf - 1)

    def down_chunk(j):
        return jnp.maximum(j - 1, 0)

    return pl.pallas_call(
        functools.partial(_ffn_kernel, S=S, R=R, T=T, nf=nf),
        grid=(N // rows, nf + 1),
        in_specs=[pl.BlockSpec((rows, D), lambda i, j: (i, 0)),
                  pl.BlockSpec((1, D), lambda i, j: (0, 0)),
                  mod_spec(4), mod_spec(3),
                  pl.BlockSpec((None, D, tf), lambda i, j: (layer, 0, up_chunk(j))),
                  pl.BlockSpec((None, D, tf), lambda i, j: (layer, 0, nf + up_chunk(j))),
                  pl.BlockSpec((None, CONV_TAPS, tf), lambda i, j: (layer, 0, up_chunk(j))),
                  pl.BlockSpec((None, 1, tf), lambda i, j: (layer, 0, up_chunk(j))),
                  pl.BlockSpec((None, tf, D), lambda i, j: (layer, down_chunk(j), 0)),
                  pl.BlockSpec((None, S, CONV_TAPS - 1, tf),
                               lambda i, j: (layer, _seq_block(i, S, R, T), 0, up_chunk(j))),
                  mod_spec(5),
                  pl.BlockSpec((1, D), lambda i, j: (0, 0))],
        out_specs=[pl.BlockSpec((rows, D), lambda i, j: (i, 0)),
                   pl.BlockSpec((S, SUBLANES, tf), lambda i, j: (i, 0, up_chunk(j)))],
        out_shape=[jax.ShapeDtypeStruct((N, D), F32),
                   jax.ShapeDtypeStruct((N // rows * S, SUBLANES, F), F32)],
        scratch_shapes=[pltpu.VMEM((rows, D), BF16), pltpu.VMEM((rows, D), F32),
                        pltpu.VMEM((rows, tf), BF16), pltpu.VMEM((rows, tf), BF16),
                        pltpu.VMEM((nf, SUBLANES, tf), F32)],
        compiler_params=_params("arbitrary", "arbitrary"),
        name="ffn",
    )(x, g_in.reshape(1, D), mod, mod, w_up, w_up, w_conv, b_conv.reshape(L, 1, F), w_down, state, mod,
      g_out.reshape(1, D))


def _last_rows(tails, n_seq):
    t = tails.reshape(n_seq, tails.shape[0] // n_seq, SUBLANES, tails.shape[2])
    return t[:, -1, SUBLANES - (CONV_TAPS - 1):, :]


def _trunk(x, mod, cache_k, cache_v, conv_a_prev, ffn_prev, norm_g, w_in_ab, w_conv_a, g_sgu, w_sgu, b_sgu,
           w_out_ab, w_qkv_sb, w_o_sb, w_ffn_up, w_ffn_conv, b_ffn_conv, w_ffn_down, heads, emit_vn):
    n_seq, T, D = x.shape
    x = x.reshape(n_seq * T, D)
    depth = norm_g.shape[0]
    n_odd = w_qkv_sb.shape[0]
    q_scale = (D // heads) ** -0.5 * LOG2E
    kv = None
    conv_a_new, ffn_new, sgu_v_new = [], [], []
    for l in range(depth):
        i = l // 2
        if l % 2 == 0:
            (z,) = _in_proj(x, n_seq, T, norm_g[l, 0], mod, l, 1, 0, w_in_ab, i, (F32,), (1.0,))
            res = _mix_even(z, x, n_seq, T, conv_a_prev, mod, l, i, w_conv_a, g_sgu, w_sgu, b_sgu, w_out_ab,
                            norm_g[l, 1], emit_vn)
            x = res[0]
            conv_a_new.append(_last_rows(res[1], n_seq))
            if emit_vn:
                sgu_v_new.append(res[2].reshape(n_seq, T, -1))
        else:
            q, *kv = _in_proj(x, n_seq, T, norm_g[l, 0], mod, l, 1, 0, w_qkv_sb, i, (BF16, F32, F32),
                              (q_scale, 1.0, 1.0), stacks=kv, slot=i, n_slots=n_odd)
            if cache_k is None:
                o = _attn_prompt(q, kv[0], kv[1], i, n_seq, T, heads)
            else:
                o = _attn_sample(q, kv[0], kv[1], i, cache_k, cache_v, n_seq, T, heads)
            x = _out_proj(o, w_o_sb, i, x, n_seq, T, mod, l, norm_g[l, 1])
        x, tails = _ffn(x, n_seq, T, ffn_prev, mod, l, norm_g[l, 2], w_ffn_up, w_ffn_conv, b_ffn_conv, w_ffn_down,
                        norm_g[l, 3])
        ffn_new.append(_last_rows(tails, n_seq))
    k_new, v_new = (a.reshape(n_odd, n_seq, T, heads, D // heads) for a in kv)
    return x.reshape(n_seq, T, D), k_new, v_new, jnp.stack(conv_a_new), jnp.stack(ffn_new), sgu_v_new


def kernel(x_prompt, x_sample, c_prompt, c_sample, cache_sb_k, cache_sb_v, state_conv_a, state_ffn_conv, w_mod, b_mod,
           norm_g, w_in_ab, w_conv_a, g_sgu, w_sgu, b_sgu, w_out_ab, w_qkv_sb, w_o_sb, w_ffn_up, w_ffn_conv,
           b_ffn_conv, w_ffn_down):
    bp, bs = x_prompt.shape[0], x_sample.shape[0]
    depth = norm_g.shape[0]
    heads = cache_sb_k.shape[3]

    mod = _modulation(jnp.concatenate([c_prompt, c_sample], axis=0), w_mod, b_mod)
    mod_p = mod[:, :bp].reshape(depth, bp, 1, -1)
    mod_s = mod[:, bp:].reshape(depth, bs, 1, -1)

    weights = (norm_g, w_in_ab.astype(BF16), w_conv_a, g_sgu, w_sgu, b_sgu, w_out_ab.astype(BF16),
               w_qkv_sb.astype(BF16), w_o_sb.astype(BF16), w_ffn_up.astype(BF16), w_ffn_conv, b_ffn_conv,
               w_ffn_down.astype(BF16))

    zeros_a = jnp.zeros((state_conv_a.shape[0], bp) + state_conv_a.shape[2:], F32)
    zeros_f = jnp.zeros((depth, bp) + state_ffn_conv.shape[2:], F32)
    y_s, k_s, v_s, conv_a_s, ffn_s, sgu_v = _trunk(x_sample, mod_s, cache_sb_k, cache_sb_v, state_conv_a,
                                                   state_ffn_conv, *weights, heads=heads, emit_vn=True)
    y_p, k_p, v_p, conv_a_p, ffn_p, _ = _trunk(x_prompt, mod_p, None, None, zeros_a, zeros_f, *weights,
                                               heads=heads, emit_vn=False)
    return (y_p, y_s, k_p, v_p, conv_a_p, ffn_p, k_s, v_s, conv_a_s, ffn_s, jnp.stack(sgu_v))
```

```python
import functools
import math

import jax
import jax.numpy as jnp
from jax import lax
from jax.experimental import pallas as pl
from jax.experimental.pallas import tpu as pltpu

F32 = jnp.float32
BF16 = jnp.bfloat16

EPS = 1e-6
CONV_TAPS = 3
GMLP_CAUSAL_BLOCK = 64
SUBLANES = 8
VMEM_LIMIT_BYTES = 56 * 1024 * 1024

ROWS_IN_PROJ = 1024
ROWS_MIX = 256
ROWS_OUT_PROJ = 512
ROWS_FFN = 512
COLS_IN_PROJ = 512
COLS_FFN = 512
COLS_MOD = 1024
ATTN_BLOCK = 256
ATTN_PROMPT_HEADS = 4
ATTN_SAMPLE_HEADS = 8

LOG2E = math.log2(math.e)
SB_Z_MAX = 126.0
SB_DONE = 150.0 / LOG2E


def _params(*semantics):
    return pltpu.CompilerParams(dimension_semantics=semantics, vmem_limit_bytes=VMEM_LIMIT_BYTES)


def _split_rows(n_seq, seq_len, rows):
    if seq_len >= rows:
        assert seq_len % rows == 0
        return 1, rows
    s = min(rows // seq_len, n_seq)
    assert n_seq % s == 0 and seq_len % SUBLANES == 0
    return s, seq_len


def _seq_block(i, S, R, T):
    return i // (T // R) if S == 1 else i


def _largest_divisor(n, candidates):
    for c in candidates:
        if n % c == 0:
            return c
    raise ValueError(f"no block size among {candidates} divides {n}")


def _rms(x, g):
    return x * lax.rsqrt(jnp.mean(x * x, axis=-1, keepdims=True) + EPS) * g


def _per_row(v, R):
    S, _, C = v.shape
    if S == 1:
        return v[0]
    return jnp.broadcast_to(v, (S, R, C)).reshape(S * R, C)


def _sigmoid(x):
    return 1.0 / (1.0 + jnp.exp(-x))


def _conv_taps(a, p0, p1, w, R):
    rt = lax.broadcasted_iota(jnp.int32, a.shape, 0) & (R - 1)
    s1 = jnp.where(rt == 0, p1, pltpu.roll(a, 1, 0))
    s2 = jnp.where(rt == 0, p0, jnp.where(rt == 1, p1, pltpu.roll(a, 2, 0)))
    return s2 * w[0:1] + s1 * w[1:2] + a * w[2:3]


def _causal_conv(a, p0, p1, w, R, S):
    if S > 1:
        return _conv_taps(a, p0, p1, w, R)
    body = pltpu.roll(a, 2, 0) * w[0:1] + pltpu.roll(a, 1, 0) * w[1:2] + a * w[2:3]
    head = _conv_taps(a[0:SUBLANES], p0, p1, w, R)
    return jnp.concatenate([head, body[SUBLANES:]], axis=0)


NORM_ROWS = 128


def _row_chunks(ref_sc, c, S, R, step):
    if S == 1:
        return ref_sc[0]
    k = step // R
    return _per_row(ref_sc[pl.ds(c * k, k)], R)


def _norm_mod_rows(x_ref, g, sc_ref, sh_ref, h_ref, S, R, straight_line=False):
    rows = S * R
    step = min(NORM_ROWS, rows)
    assert rows % step == 0 and (S == 1 or step % R == 0)

    def body(c, carry):
        rs = pl.ds(c * step, step) if straight_line else pl.ds(pl.multiple_of(c * step, step), step)
        h = _rms(x_ref[rs, :], g) * (1.0 + _row_chunks(sc_ref, c, S, R, step)) + _row_chunks(sh_ref, c, S, R, step)
        h_ref[rs, :] = h.astype(BF16)
        return carry

    _for_chunks(rows // step, body, straight_line)


def _residual_norm_rows(x_ref, y_ref, gate_ref, g, xo_ref, S, R, straight_line=False):
    rows = S * R
    step = min(NORM_ROWS, rows)
    assert rows % step == 0 and (S == 1 or step % R == 0)

    def body(c, carry):
        rs = pl.ds(c * step, step) if straight_line else pl.ds(pl.multiple_of(c * step, step), step)
        xo_ref[rs, :] = x_ref[rs, :] + _row_chunks(gate_ref, c, S, R, step) * _rms(y_ref[rs, :], g)
        return carry

    _for_chunks(rows // step, body, straight_line)


def _for_chunks(n, body, straight_line):
    if straight_line:
        for c in range(n):
            body(c, 0)
    else:
        lax.fori_loop(0, n, body, 0)


def _mod_kernel(c_ref, w_ref, b_ref, o_ref):
    c = c_ref[...]
    s = (c * _sigmoid(c)).astype(BF16)
    o_ref[...] = jnp.dot(s, w_ref[...].astype(BF16), preferred_element_type=F32) + b_ref[...]


def _modulation(c, w_mod, b_mod):
    L, D, N = w_mod.shape
    B = c.shape[0]
    tn = _largest_divisor(N, (COLS_MOD, 512, 256, 128))
    return pl.pallas_call(
        _mod_kernel,
        grid=(L, N // tn),
        in_specs=[pl.BlockSpec((B, D), lambda l, j: (0, 0)),
                  pl.BlockSpec((None, D, tn), lambda l, j: (l, 0, j)),
                  pl.BlockSpec((None, 1, tn), lambda l, j: (l, 0, j))],
        out_specs=pl.BlockSpec((None, B, tn), lambda l, j: (l, 0, j)),
        out_shape=jax.ShapeDtypeStruct((L, B, N), F32),
        compiler_params=_params("parallel", "parallel"),
        name="modulation",
    )(c, w_mod, b_mod.reshape(L, 1, N))


def _in_proj_kernel(x_ref, g_ref, sc_ref, sh_ref, w_ref, *rest, S, R, blocks_per_out, n_out, out_scales):
    outs, h_ref = rest[-1 - n_out:-1], rest[-1]
    j = pl.program_id(1)

    @pl.when(j == 0)
    def _():
        _norm_mod_rows(x_ref, g_ref[...], sc_ref, sh_ref, h_ref, S, R)

    y = jnp.dot(h_ref[...], w_ref[...], preferred_element_type=F32)
    if n_out == 1:
        outs[0][...] = (y * out_scales[0]).astype(outs[0].dtype)
    else:
        for k, o_ref in enumerate(outs):
            @pl.when(j // blocks_per_out == k)
            def _(o_ref=o_ref, k=k):
                o_ref[...] = (y if out_scales[k] == 1.0 else y * out_scales[k]).astype(o_ref.dtype)


def _in_proj(x, n_seq, T, g, mod, layer, sc_chunk, sh_chunk, w, wi, out_dtypes, out_scales, stacks=None, slot=0,
             n_slots=1):
    N, D = x.shape
    n_total = w.shape[2]
    n_out = len(out_dtypes)
    S, R = _split_rows(n_seq, T, ROWS_IN_PROJ)
    rows = S * R
    width = n_total // n_out
    tn = _largest_divisor(width, (COLS_IN_PROJ, 256, 128))
    bpo = width // tn

    def mod_spec(chunk):
        return pl.BlockSpec((None, S, 1, D), lambda i, j: (layer, _seq_block(i, S, R, T), 0, chunk))

    def col(j, k):
        return jnp.clip(j - k * bpo, 0, bpo - 1)

    out_specs = [pl.BlockSpec((rows, tn), lambda i, j: (i, col(j, 0)))]
    out_shape = [jax.ShapeDtypeStruct((N, width), out_dtypes[0])]
    for k in range(1, n_out):
        out_specs.append(pl.BlockSpec((None, rows, tn), functools.partial(lambda i, j, k: (slot, i, col(j, k)), k=k)))
        out_shape.append(jax.ShapeDtypeStruct((n_slots, N, width), out_dtypes[k]))
    in_specs = [pl.BlockSpec((rows, D), lambda i, j: (i, 0)),
                pl.BlockSpec((1, D), lambda i, j: (0, 0)),
                mod_spec(sc_chunk), mod_spec(sh_chunk),
                pl.BlockSpec((None, D, tn), lambda i, j: (wi, 0, j))]
    args = [x, g.reshape(1, D), mod, mod, w]
    aliases = {}
    if stacks is not None:
        for k, st in enumerate(stacks):
            aliases[len(args)] = k + 1
            in_specs.append(pl.BlockSpec(memory_space=pl.ANY))
            args.append(st)
    return pl.pallas_call(
        functools.partial(_in_proj_kernel, S=S, R=R, blocks_per_out=bpo, n_out=n_out, out_scales=tuple(out_scales)),
        grid=(N // rows, n_total // tn),
        in_specs=in_specs, out_specs=out_specs, out_shape=out_shape,
        scratch_shapes=[pltpu.VMEM((rows, D), BF16)],
        input_output_aliases=aliases,
        compiler_params=_params("parallel", "arbitrary"),
        name="in_proj",
    )(*args)


def _sgu_matrices(ws_ref, bs_ref, g, T, chunk, whole_chunks):
    w = ws_ref[g]
    b = jnp.broadcast_to(bs_ref[g], (chunk, chunk))
    ii = lax.broadcasted_iota(jnp.int32, (chunk, chunk), 0)
    jj = lax.broadcasted_iota(jnp.int32, (chunk, chunk), 1)
    if whole_chunks:
        return jnp.where(jj // GMLP_CAUSAL_BLOCK <= ii // GMLP_CAUSAL_BLOCK, w, 0.0), b
    w_tl = jnp.where((ii < T) & (jj < T), w, 0.0)
    b_tl = jnp.where(ii < T, b, 0.0)
    wm, bm = w_tl, b_tl
    for k in range(1, chunk // T):
        wm = wm + pltpu.roll(pltpu.roll(w_tl, k * T, 0), k * T, 1)
        bm = bm + pltpu.roll(b_tl, k * T, 0)
    return wm, bm


def _mix_even_kernel(*refs, S, R, T, chunk, groups, halo, emit_vn):
    refs = list(refs)
    xa_ref, gb_ref, gc_ref, u_ref, v_ref = refs[:5]
    del refs[:5]
    if halo:
        hxa_ref, hgc_ref = refs[:2]
        del refs[:2]
    st_ref, x_ref, g1_ref, wc_ref, gs_ref, ws_ref, bs_ref, wo_ref, gn_ref = refs[:9]
    del refs[:9]
    xo_ref, tail_ref = refs[:2]
    del refs[:2]
    if emit_vn:
        vn_out_ref = refs.pop(0)
    cat_ref, vn_ref = refs

    i = pl.program_id(0)
    rows = S * R
    C = xa_ref.shape[1]

    ca = gc_ref[...] * xa_ref[...]
    st = st_ref[...]
    p0, p1 = st[:, 0:1, :], st[:, 1:2, :]
    if halo:
        hca = hgc_ref[...] * hxa_ref[...]
        first = (i % (T // R)) == 0
        p0 = jnp.where(first, p0, hca[SUBLANES - 2:SUBLANES - 1][None])
        p1 = jnp.where(first, p1, hca[SUBLANES - 1:SUBLANES][None])
    conv = _causal_conv(ca, _per_row(p0, R), _per_row(p1, R), wc_ref[...], R, S)
    cat_ref[:, 0:C] = (gb_ref[...] * conv).astype(BF16)
    tail_ref[...] = ca.reshape(S, R, C)[:, R - SUBLANES:R, :]

    vn = _rms(v_ref[...], gs_ref[...])
    if emit_vn:
        vn_out_ref[...] = vn
    vn_ref[...] = vn.astype(BF16)
    gd = v_ref.shape[1] // groups
    for g in range(groups):
        wm, bm = _sgu_matrices(ws_ref, bs_ref, g, T, chunk, whole_chunks=(R % chunk == 0))
        wm = wm.astype(BF16)
        cs = slice(g * gd, (g + 1) * gd)
        for c in range(rows // chunk):
            rs = slice(c * chunk, (c + 1) * chunk)
            mixed = jnp.dot(wm, vn_ref[rs, cs], preferred_element_type=F32) + bm
            cat_ref[rs, C + g * gd:C + (g + 1) * gd] = (u_ref[rs, cs] * mixed).astype(BF16)

    y = jnp.dot(cat_ref[...], wo_ref[...], preferred_element_type=F32)
    xo_ref[...] = x_ref[...] + _per_row(g1_ref[...], R) * _rms(y, gn_ref[...])


def _mix_even(z, x, n_seq, T, state, mod, layer, wi, w_conv, g_sgu, w_sgu, b_sgu, w_out, g_norm, emit_vn):
    N, D = x.shape
    C = w_conv.shape[2]
    groups, chunk = w_sgu.shape[1], w_sgu.shape[2]
    Cb = g_sgu.shape[1]
    S, R = _split_rows(n_seq, T, ROWS_MIX)
    rows = S * R
    halo = R < T
    assert rows % chunk == 0 and (R % chunk == 0 or (chunk % T == 0 and T <= GMLP_CAUSAL_BLOCK))
    assert z.shape[1] == 3 * C + 2 * Cb and C == Cb

    in_specs = [pl.BlockSpec((rows, C), functools.partial(lambda i, k: (i, k), k=k)) for k in range(5)]
    args = [z] * 5
    if halo:
        per = R // SUBLANES
        in_specs += [pl.BlockSpec((SUBLANES, C), functools.partial(
            lambda i, k: (jnp.maximum(i * per - 1, 0), k), k=k)) for k in (0, 2)]
        args += [z, z]
    in_specs += [
        pl.BlockSpec((None, S, CONV_TAPS - 1, C), lambda i: (wi, _seq_block(i, S, R, T), 0, 0)),
        pl.BlockSpec((rows, D), lambda i: (i, 0)),
        pl.BlockSpec((None, S, 1, D), lambda i: (layer, _seq_block(i, S, R, T), 0, 2)),
        pl.BlockSpec((None, CONV_TAPS, C), lambda i: (wi, 0, 0)),
        pl.BlockSpec((None, 1, Cb), lambda i: (wi, 0, 0)),
        pl.BlockSpec((None, groups, chunk, chunk), lambda i: (wi, 0, 0, 0)),
        pl.BlockSpec((None, groups, chunk, 1), lambda i: (wi, 0, 0, 0)),
        pl.BlockSpec((None, C + Cb, D), lambda i: (wi, 0, 0)),
        pl.BlockSpec((1, D), lambda i: (0, 0)),
    ]
    n_even = w_conv.shape[0]
    args += [state, x, mod, w_conv, g_sgu.reshape(n_even, 1, Cb), w_sgu, b_sgu.reshape(n_even, groups, chunk, 1),
             w_out, g_norm.reshape(1, D)]
    out_specs = [pl.BlockSpec((rows, D), lambda i: (i, 0)),
                 pl.BlockSpec((S, SUBLANES, C), lambda i: (i, 0, 0))]
    out_shape = [jax.ShapeDtypeStruct((N, D), F32),
                 jax.ShapeDtypeStruct((N // rows * S, SUBLANES, C), F32)]
    if emit_vn:
        out_specs.append(pl.BlockSpec((rows, Cb), lambda i: (i, 0)))
        out_shape.append(jax.ShapeDtypeStruct((N, Cb), F32))
    return pl.pallas_call(
        functools.partial(_mix_even_kernel, S=S, R=R, T=T, chunk=chunk, groups=groups, halo=halo, emit_vn=emit_vn),
        grid=(N // rows,),
        in_specs=in_specs, out_specs=out_specs, out_shape=out_shape,
        scratch_shapes=[pltpu.VMEM((rows, C + Cb), BF16), pltpu.VMEM((rows, Cb), BF16)],
        compiler_params=_params("parallel"),
        name="mix_even",
    )(*args)


def _upper_inclusive(n):
    r = lax.broadcasted_iota(jnp.int32, (n, n), 0)
    c = lax.broadcasted_iota(jnp.int32, (n, n), 1)
    return jnp.where(r >= c, 1.0, 0.0).astype(BF16)


def _strictly_earlier(n):
    t = lax.broadcasted_iota(jnp.int32, (n, n), 0)
    s = lax.broadcasted_iota(jnp.int32, (n, n), 1)
    return s < t


def _sb_blocks(qs, ks, vs, u_inc, cs, accs, mask):
    tq = qs[0].shape[0]
    zs, parts = [], []
    for q, k in zip(qs, ks):
        z = jnp.minimum(lax.dot_general(q, k, (((1,), (1,)), ((), ())), preferred_element_type=F32), SB_Z_MAX)
        fail = jnp.log(1.0 + jnp.exp2(z))
        if mask is not None:
            fail = jnp.where(mask, fail, 0.0)
        hi = fail.astype(BF16)
        parts += [hi, (fail - hi.astype(F32)).astype(BF16)]
        zs.append(z)
    sums = jnp.dot(jnp.concatenate(parts, axis=0), u_inc, preferred_element_type=F32)
    cs_out, accs_out = [], []
    for h, (z, v, c, acc) in enumerate(zip(zs, vs, cs, accs)):
        incl = sums[2 * h * tq:(2 * h + 1) * tq] + sums[(2 * h + 1) * tq:(2 * h + 2) * tq]
        w = jnp.exp2(z - (incl + c) * LOG2E)
        if mask is not None:
            w = jnp.where(mask, w, 0.0)
        accs_out.append(acc + jnp.dot(w.astype(BF16), v, preferred_element_type=F32))
        cs_out.append(c + incl[:, 0:1])
    return tuple(cs_out), tuple(accs_out)


def _unfinished(cs):
    return (jnp.min(functools.reduce(jnp.minimum, cs)) < SB_DONE).astype(jnp.int32)


def _attn_prompt_kernel(q_ref, k_ref, v_ref, o_ref, kb_ref, vb_ref, u_ref, *, tq, heads, dh):
    qi = pl.program_id(2)

    @pl.when(qi == 0)
    def _():
        kb_ref[...] = k_ref[...].astype(BF16)
        vb_ref[...] = v_ref[...].astype(BF16)
        u_ref[...] = _upper_inclusive(tq)

    u = u_ref[...]
    cols = [slice(h * dh, (h + 1) * dh) for h in range(heads)]
    qs = [q_ref[:, cs] for cs in cols]

    def sweep(n, cs, accs, mask):
        rows = pl.ds(pl.multiple_of((qi - n) * tq, tq), tq)
        return _sb_blocks(qs, [kb_ref[rows, cs_] for cs_ in cols], [vb_ref[rows, cs_] for cs_ in cols], u, cs, accs,
                          mask)

    cs, accs = sweep(0, (jnp.zeros((tq, 1), F32),) * heads, (jnp.zeros((tq, dh), F32),) * heads,
                     _strictly_earlier(tq))

    def body(carry):
        n, _, cs, accs = carry
        cs, accs = sweep(n, cs, accs, None)
        return n + 1, _unfinished(cs), cs, accs

    _, _, _, accs = lax.while_loop(lambda carry: (carry[0] <= qi) & (carry[1] > 0), body,
                                   (jnp.int32(1), _unfinished(cs), cs, accs))
    for h in range(heads):
        o_ref[:, cols[h]] = accs[h].astype(o_ref.dtype)


def _attn_prompt(q, k_stack, v_stack, slot, n_seq, T, heads):
    N, D = q.shape
    dh = D // heads
    tq = min(ATTN_BLOCK, T)
    hp = _largest_divisor(heads, (ATTN_PROMPT_HEADS, 1))
    assert T % tq == 0
    n_slots = k_stack.shape[0]
    kv_spec = pl.BlockSpec((None, None, T, hp * dh), lambda b, h, i: (slot, b, 0, h))
    o = pl.pallas_call(
        functools.partial(_attn_prompt_kernel, tq=tq, heads=hp, dh=dh),
        grid=(n_seq, heads // hp, T // tq),
        in_specs=[pl.BlockSpec((None, tq, hp * dh), lambda b, h, i: (b, i, h)), kv_spec, kv_spec],
        out_specs=pl.BlockSpec((None, tq, hp * dh), lambda b, h, i: (b, i, h)),
        out_shape=jax.ShapeDtypeStruct((n_seq, T, D), BF16),
        scratch_shapes=[pltpu.VMEM((T, hp * dh), BF16), pltpu.VMEM((T, hp * dh), BF16),
                        pltpu.VMEM((tq, tq), BF16)],
        compiler_params=_params("parallel", "parallel", "arbitrary"),
        name="attn_prompt",
    )(q.reshape(n_seq, T, D), k_stack.reshape(n_slots, n_seq, T, D), v_stack.reshape(n_slots, n_seq, T, D))
    return o.reshape(N, D)


def _attn_sample_kernel(q_ref, kn_ref, vn_ref, ck_ref, cv_ref, o_ref, *, T, P, tk, heads, all_heads, dh):
    u_past = _upper_inclusive(tk)
    cols = [slice(h * dh, (h + 1) * dh) for h in range(heads)]
    qs = [q_ref[:, cs] for cs in cols]

    cs, accs = _sb_blocks(qs, [kn_ref[:, c].astype(BF16) for c in cols], [vn_ref[:, c].astype(BF16) for c in cols],
                          _upper_inclusive(T), (jnp.zeros((T, 1), F32),) * heads,
                          (jnp.zeros((T, dh), F32),) * heads, _strictly_earlier(T))

    h0 = pl.program_id(1) * heads

    def body(carry):
        n, _, cs, accs = carry
        r0 = (P - (n + 1) * tk) * all_heads + h0
        ks = [ck_ref[pl.ds(r0 + h, tk, stride=all_heads), :].astype(BF16) for h in range(heads)]
        vs = [cv_ref[pl.ds(r0 + h, tk, stride=all_heads), :].astype(BF16) for h in range(heads)]
        cs, accs = _sb_blocks(qs, ks, vs, u_past, cs, accs, None)
        return n + 1, _unfinished(cs), cs, accs

    _, _, _, accs = lax.while_loop(lambda carry: (carry[0] < P // tk) & (carry[1] > 0), body,
                                   (jnp.int32(0), _unfinished(cs), cs, accs))
    for h in range(heads):
        o_ref[:, cols[h]] = accs[h].astype(o_ref.dtype)


def _attn_sample(q, k_stack, v_stack, slot, cache_k, cache_v, n_seq, T, heads):
    N, D = q.shape
    dh = D // heads
    P = cache_k.shape[2]
    tk = _largest_divisor(P, (ATTN_BLOCK, 128))
    hg = _largest_divisor(heads, (ATTN_SAMPLE_HEADS, heads))
    n_slots = k_stack.shape[0]
    q_spec = pl.BlockSpec((None, T, hg * dh), lambda b, g: (b, 0, g))
    new_spec = pl.BlockSpec((None, None, T, hg * dh), lambda b, g: (slot, b, 0, g))
    past_spec = pl.BlockSpec((None, None, P * heads, dh), lambda b, g: (slot, b, 0, 0))
    past_shape = cache_k.shape[:2] + (P * heads, dh)
    o = pl.pallas_call(
        functools.partial(_attn_sample_kernel, T=T, P=P, tk=tk, heads=hg, all_heads=heads, dh=dh),
        grid=(n_seq, heads // hg),
        in_specs=[q_spec, new_spec, new_spec, past_spec, past_spec],
        out_specs=q_spec,
        out_shape=jax.ShapeDtypeStruct((n_seq, T, D), BF16),
        compiler_params=_params("parallel", "arbitrary"),
        name="attn_sample",
    )(q.reshape(n_seq, T, D), k_stack.reshape(n_slots, n_seq, T, D), v_stack.reshape(n_slots, n_seq, T, D),
      cache_k.reshape(past_shape), cache_v.reshape(past_shape))
    return o.reshape(N, D)


def _out_proj_kernel(o_ref, w_ref, x_ref, g1_ref, gn_ref, xo_ref, *, R):
    y = jnp.dot(o_ref[...], w_ref[...], preferred_element_type=F32)
    xo_ref[...] = x_ref[...] + _per_row(g1_ref[...], R) * _rms(y, gn_ref[...])


def _out_proj(o, w, wi, x, n_seq, T, mod, layer, g_norm):
    N, D = x.shape
    S, R = _split_rows(n_seq, T, ROWS_OUT_PROJ)
    rows = S * R
    return pl.pallas_call(
        functools.partial(_out_proj_kernel, R=R),
        grid=(N // rows,),
        in_specs=[pl.BlockSpec((rows, D), lambda i: (i, 0)),
                  pl.BlockSpec((None, D, D), lambda i: (wi, 0, 0)),
                  pl.BlockSpec((rows, D), lambda i: (i, 0)),
                  pl.BlockSpec((None, S, 1, D), lambda i: (layer, _seq_block(i, S, R, T), 0, 2)),
                  pl.BlockSpec((1, D), lambda i: (0, 0))],
        out_specs=pl.BlockSpec((rows, D), lambda i: (i, 0)),
        out_shape=jax.ShapeDtypeStruct((N, D), F32),
        compiler_params=_params("parallel"),
        name="out_proj",
    )(o, w, x, mod, g_norm.reshape(1, D))


def _ffn_kernel(x_ref, xp_ref, gn_in_ref, sc_ref, sh_ref, wa_ref, wg_ref, wc_ref, bc_ref, wd_ref, st_ref, g2_ref,
                gn_out_ref, xo_ref, tail_ref, h_ref, acc_ref, act0_ref, act1_ref, carry_ref, *, S, R, T, nf, nb):
    i, j = pl.program_id(0), pl.program_id(1)
    acts = (act0_ref, act1_ref)
    last_act = acts[(nf - 1) % 2]

    def up_gate(dst_ref):
        h = h_ref[...]
        a = jnp.dot(h, wa_ref[...], preferred_element_type=F32)
        gate = jnp.dot(h, wg_ref[...], preferred_element_type=F32)
        tail = a.reshape(S, R, -1)[:, R - SUBLANES:R, :]
        tail_ref[...] = tail
        p0, p1 = st_ref[:, 0:1, :], st_ref[:, 1:2, :]
        if R < T:
            first = (i % (T // R)) == 0
            prev = carry_ref[j]
            p0 = jnp.where(first, p0, prev[SUBLANES - 2:SUBLANES - 1][None])
            p1 = jnp.where(first, p1, prev[SUBLANES - 1:SUBLANES][None])
            carry_ref[j] = tail[0]
        half = 0.5 * (_causal_conv(a, _per_row(p0, R), _per_row(p1, R), wc_ref[...], R, S) + bc_ref[...])
        dst_ref[...] = ((half + half * jnp.tanh(half)) * gate).astype(BF16)

    def down(src_ref):
        acc_ref[...] += jnp.dot(src_ref[...], wd_ref[...], preferred_element_type=F32)

    def finish_previous():
        down(last_act)
        _residual_norm_rows(xp_ref, acc_ref, g2_ref, gn_out_ref[...], xo_ref, S, R, straight_line=True)

    def start_block():
        _norm_mod_rows(x_ref, gn_in_ref[...], sc_ref, sh_ref, h_ref, S, R, straight_line=True)
        acc_ref[...] = jnp.zeros_like(acc_ref)
        up_gate(acts[0])

    @pl.when((j == 0) & (i == 0))
    def _():
        if R < T:
            carry_ref[...] = jnp.zeros_like(carry_ref)
        start_block()

    @pl.when((j == 0) & (i > 0) & (i < nb))
    def _():
        finish_previous()
        start_block()

    @pl.when((j == 0) & (i == nb))
    def _():
        finish_previous()

    for parity in range(2):
        @pl.when((j > 0) & (i < nb) & (j % 2 == parity))
        def _(parity=parity):
            up_gate(acts[parity])
            down(acts[1 - parity])


def _ffn(x, n_seq, T, state, mod, layer, g_in, w_up, w_conv, b_conv, w_down, g_out):
    N, D = x.shape
    L, F = w_down.shape[0], w_down.shape[1]
    S, R = _split_rows(n_seq, T, ROWS_FFN)
    rows = S * R
    tf = _largest_divisor(F, (COLS_FFN, 256, 128))
    nf = F // tf

    nb = N // rows

    def cur(i):
        return jnp.minimum(i, nb - 1)

    def prev(i):
        return jnp.maximum(i - 1, 0)

    def up_chunk(i, j):
        return jnp.where(i < nb, j, nf - 1)

    def down_chunk(i, j):
        return jnp.where((i == nb) & (j > 0), nf - 1, (j + nf - 1) % nf)

    def mod_spec(chunk, blk):
        return pl.BlockSpec((None, S, 1, D), lambda i, j: (layer, _seq_block(blk(i), S, R, T), 0, chunk))

    return pl.pallas_call(
        functools.partial(_ffn_kernel, S=S, R=R, T=T, nf=nf, nb=nb),
        grid=(nb + 1, nf),
        in_specs=[pl.BlockSpec((rows, D), lambda i, j: (cur(i), 0)),
                  pl.BlockSpec((rows, D), lambda i, j: (prev(i), 0)),
                  pl.BlockSpec((1, D), lambda i, j: (0, 0)),
                  mod_spec(4, cur), mod_spec(3, cur),
                  pl.BlockSpec((None, D, tf), lambda i, j: (layer, 0, up_chunk(i, j))),
                  pl.BlockSpec((None, D, tf), lambda i, j: (layer, 0, nf + up_chunk(i, j))),
                  pl.BlockSpec((None, CONV_TAPS, tf), lambda i, j: (layer, 0, up_chunk(i, j))),
                  pl.BlockSpec((None, 1, tf), lambda i, j: (layer, 0, up_chunk(i, j))),
                  pl.BlockSpec((None, tf, D), lambda i, j: (layer, down_chunk(i, j), 0)),
                  pl.BlockSpec((None, S, CONV_TAPS - 1, tf),
                               lambda i, j: (layer, _seq_block(cur(i), S, R, T), 0, up_chunk(i, j))),
                  mod_spec(5, prev),
                  pl.BlockSpec((1, D), lambda i, j: (0, 0))],
        out_specs=[pl.BlockSpec((rows, D), lambda i, j: (prev(i), 0)),
                   pl.BlockSpec((S, SUBLANES, tf), lambda i, j: (cur(i), 0, up_chunk(i, j)))],
        out_shape=[jax.ShapeDtypeStruct((N, D), F32),
                   jax.ShapeDtypeStruct((N // rows * S, SUBLANES, F), F32)],
        scratch_shapes=[pltpu.VMEM((rows, D), BF16), pltpu.VMEM((rows, D), F32),
                        pltpu.VMEM((rows, tf), BF16), pltpu.VMEM((rows, tf), BF16),
                        pltpu.VMEM((nf, SUBLANES, tf), F32)],
        compiler_params=_params("arbitrary", "arbitrary"),
        name="ffn",
    )(x, x, g_in.reshape(1, D), mod, mod, w_up, w_up, w_conv, b_conv.reshape(L, 1, F), w_down, state, mod,
      g_out.reshape(1, D))


def _last_rows(tails, n_seq):
    t = tails.reshape(n_seq, tails.shape[0] // n_seq, SUBLANES, tails.shape[2])
    return t[:, -1, SUBLANES - (CONV_TAPS - 1):, :]


def _trunk(x, mod, cache_k, cache_v, conv_a_prev, ffn_prev, norm_g, w_in_ab, w_conv_a, g_sgu, w_sgu, b_sgu,
           w_out_ab, w_qkv_sb, w_o_sb, w_ffn_up, w_ffn_conv, b_ffn_conv, w_ffn_down, heads, emit_vn):
    n_seq, T, D = x.shape
    x = x.reshape(n_seq * T, D)
    depth = norm_g.shape[0]
    n_odd = w_qkv_sb.shape[0]
    q_scale = (D // heads) ** -0.5 * LOG2E
    kv = None
    conv_a_new, ffn_new, sgu_v_new = [], [], []
    for l in range(depth):
        i = l // 2
        if l % 2 == 0:
            (z,) = _in_proj(x, n_seq, T, norm_g[l, 0], mod, l, 1, 0, w_in_ab, i, (F32,), (1.0,))
            res = _mix_even(z, x, n_seq, T, conv_a_prev, mod, l, i, w_conv_a, g_sgu, w_sgu, b_sgu, w_out_ab,
                            norm_g[l, 1], emit_vn)
            x = res[0]
            conv_a_new.append(_last_rows(res[1], n_seq))
            if emit_vn:
                sgu_v_new.append(res[2].reshape(n_seq, T, -1))
        else:
            q, *kv = _in_proj(x, n_seq, T, norm_g[l, 0], mod, l, 1, 0, w_qkv_sb, i, (BF16, F32, F32),
                              (q_scale, 1.0, 1.0), stacks=kv, slot=i, n_slots=n_odd)
            if cache_k is None:
                o = _attn_prompt(q, kv[0], kv[1], i, n_seq, T, heads)
            else:
                o = _attn_sample(q, kv[0], kv[1], i, cache_k, cache_v, n_seq, T, heads)
            x = _out_proj(o, w_o_sb, i, x, n_seq, T, mod, l, norm_g[l, 1])
        x, tails = _ffn(x, n_seq, T, ffn_prev, mod, l, norm_g[l, 2], w_ffn_up, w_ffn_conv, b_ffn_conv, w_ffn_down,
                        norm_g[l, 3])
        ffn_new.append(_last_rows(tails, n_seq))
    k_new, v_new = (a.reshape(n_odd, n_seq, T, heads, D // heads) for a in kv)
    return x.reshape(n_seq, T, D), k_new, v_new, jnp.stack(conv_a_new), jnp.stack(ffn_new), sgu_v_new


def kernel(x_prompt, x_sample, c_prompt, c_sample, cache_sb_k, cache_sb_v, state_conv_a, state_ffn_conv, w_mod, b_mod,
           norm_g, w_in_ab, w_conv_a, g_sgu, w_sgu, b_sgu, w_out_ab, w_qkv_sb, w_o_sb, w_ffn_up, w_ffn_conv,
           b_ffn_conv, w_ffn_down):
    bp, bs = x_prompt.shape[0], x_sample.shape[0]
    depth = norm_g.shape[0]
    heads = cache_sb_k.shape[3]

    mod = _modulation(jnp.concatenate([c_prompt, c_sample], axis=0), w_mod, b_mod)
    mod_p = mod[:, :bp].reshape(depth, bp, 1, -1)
    mod_s = mod[:, bp:].reshape(depth, bs, 1, -1)

    weights = (norm_g, w_in_ab.astype(BF16), w_conv_a, g_sgu, w_sgu, b_sgu, w_out_ab.astype(BF16),
               w_qkv_sb.astype(BF16), w_o_sb.astype(BF16), w_ffn_up.astype(BF16), w_ffn_conv, b_ffn_conv,
               w_ffn_down.astype(BF16))

    zeros_a = jnp.zeros((state_conv_a.shape[0], bp) + state_conv_a.shape[2:], F32)
    zeros_f = jnp.zeros((depth, bp) + state_ffn_conv.shape[2:], F32)
    y_s, k_s, v_s, conv_a_s, ffn_s, sgu_v = _trunk(x_sample, mod_s, cache_sb_k, cache_sb_v, state_conv_a,
                                                   state_ffn_conv, *weights, heads=heads, emit_vn=True)
    sample_out, (x_prompt, mod_p) = lax.optimization_barrier(
        ((y_s, k_s, v_s, conv_a_s, ffn_s, sgu_v), (x_prompt, mod_p)))
    y_s, k_s, v_s, conv_a_s, ffn_s, sgu_v = sample_out
    y_p, k_p, v_p, conv_a_p, ffn_p, _ = _trunk(x_prompt, mod_p, None, None, zeros_a, zeros_f, *weights,
                                               heads=heads, emit_vn=False)
    return (y_p, y_s, k_p, v_p, conv_a_p, ffn_p, k_s, v_s, conv_a_s, ffn_s, jnp.stack(sgu_v))
```

```python
import functools
import math

import jax
import jax.numpy as jnp
from jax import lax
from jax.experimental import pallas as pl
from jax.experimental.pallas import tpu as pltpu

F32 = jnp.float32
BF16 = jnp.bfloat16

EPS = 1e-6
CONV_TAPS = 3
GMLP_CAUSAL_BLOCK = 64
SUBLANES = 8
VMEM_LIMIT_BYTES = 56 * 1024 * 1024

ROWS_IN_PROJ = 1024
ROWS_MIX = 256
ROWS_OUT_PROJ = 512
ROWS_FFN = 1024
COLS_IN_PROJ = 512
COLS_FFN = 256
COLS_MOD = 1024
ATTN_BLOCK = 256
ATTN_PROMPT_HEADS = 4
ATTN_SAMPLE_HEADS = 8

LOG2E = math.log2(math.e)
SB_Z_MAX = 126.0
SB_DONE = 150.0 / LOG2E


def _params(*semantics):
    return pltpu.CompilerParams(dimension_semantics=semantics, vmem_limit_bytes=VMEM_LIMIT_BYTES)


def _split_rows(n_seq, seq_len, rows):
    if seq_len >= rows:
        assert seq_len % rows == 0
        return 1, rows
    s = min(rows // seq_len, n_seq)
    assert n_seq % s == 0 and seq_len % SUBLANES == 0
    return s, seq_len


def _seq_block(i, S, R, T):
    return i // (T // R) if S == 1 else i


def _largest_divisor(n, candidates):
    for c in candidates:
        if n % c == 0:
            return c
    raise ValueError(f"no block size among {candidates} divides {n}")


def _rms(x, g):
    return x * lax.rsqrt(jnp.mean(x * x, axis=-1, keepdims=True) + EPS) * g


def _per_row(v, R):
    S, _, C = v.shape
    if S == 1:
        return v[0]
    return jnp.broadcast_to(v, (S, R, C)).reshape(S * R, C)


def _sigmoid(x):
    return 1.0 / (1.0 + jnp.exp(-x))


def _conv_taps(a, p0, p1, w, R):
    rt = lax.broadcasted_iota(jnp.int32, a.shape, 0) & (R - 1)
    s1 = jnp.where(rt == 0, p1, pltpu.roll(a, 1, 0))
    s2 = jnp.where(rt == 0, p0, jnp.where(rt == 1, p1, pltpu.roll(a, 2, 0)))
    return s2 * w[0:1] + s1 * w[1:2] + a * w[2:3]


def _causal_conv(a, p0, p1, w, R, S):
    if S > 1:
        return _conv_taps(a, p0, p1, w, R)
    body = pltpu.roll(a, 2, 0) * w[0:1] + pltpu.roll(a, 1, 0) * w[1:2] + a * w[2:3]
    head = _conv_taps(a[0:SUBLANES], p0, p1, w, R)
    return jnp.concatenate([head, body[SUBLANES:]], axis=0)


NORM_ROWS = 128


def _row_chunks(ref_sc, c, S, R, step):
    if S == 1:
        return ref_sc[0]
    k = step // R
    return _per_row(ref_sc[pl.ds(c * k, k)], R)


def _norm_mod_rows(x_ref, g, sc_ref, sh_ref, h_ref, S, R, straight_line=False):
    rows = S * R
    step = min(NORM_ROWS, rows)
    assert rows % step == 0 and (S == 1 or step % R == 0)

    def body(c, carry):
        rs = pl.ds(c * step, step) if straight_line else pl.ds(pl.multiple_of(c * step, step), step)
        h = _rms(x_ref[rs, :], g) * (1.0 + _row_chunks(sc_ref, c, S, R, step)) + _row_chunks(sh_ref, c, S, R, step)
        h_ref[rs, :] = h.astype(BF16)
        return carry

    _for_chunks(rows // step, body, straight_line)


def _residual_norm_rows(x_ref, y_ref, gate_ref, g, xo_ref, S, R, straight_line=False):
    rows = S * R
    step = min(NORM_ROWS, rows)
    assert rows % step == 0 and (S == 1 or step % R == 0)

    def body(c, carry):
        rs = pl.ds(c * step, step) if straight_line else pl.ds(pl.multiple_of(c * step, step), step)
        xo_ref[rs, :] = x_ref[rs, :] + _row_chunks(gate_ref, c, S, R, step) * _rms(y_ref[rs, :], g)
        return carry

    _for_chunks(rows // step, body, straight_line)


def _for_chunks(n, body, straight_line):
    if straight_line:
        for c in range(n):
            body(c, 0)
    else:
        lax.fori_loop(0, n, body, 0)


def _mod_kernel(c_ref, w_ref, b_ref, o_ref):
    c = c_ref[...]
    s = (c * _sigmoid(c)).astype(BF16)
    o_ref[...] = jnp.dot(s, w_ref[...].astype(BF16), preferred_element_type=F32) + b_ref[...]


def _modulation(c, w_mod, b_mod):
    L, D, N = w_mod.shape
    B = c.shape[0]
    tn = _largest_divisor(N, (COLS_MOD, 512, 256, 128))
    return pl.pallas_call(
        _mod_kernel,
        grid=(L, N // tn),
        in_specs=[pl.BlockSpec((B, D), lambda l, j: (0, 0)),
                  pl.BlockSpec((None, D, tn), lambda l, j: (l, 0, j)),
                  pl.BlockSpec((None, 1, tn), lambda l, j: (l, 0, j))],
        out_specs=pl.BlockSpec((None, B, tn), lambda l, j: (l, 0, j)),
        out_shape=jax.ShapeDtypeStruct((L, B, N), F32),
        compiler_params=_params("parallel", "parallel"),
        name="modulation",
    )(c, w_mod, b_mod.reshape(L, 1, N))


def _in_proj_kernel(x_ref, g_ref, sc_ref, sh_ref, w_ref, *rest, S, R, blocks_per_out, n_out, out_scales):
    outs, h_ref = rest[-1 - n_out:-1], rest[-1]
    j = pl.program_id(1)

    @pl.when(j == 0)
    def _():
        _norm_mod_rows(x_ref, g_ref[...], sc_ref, sh_ref, h_ref, S, R)

    y = jnp.dot(h_ref[...], w_ref[...], preferred_element_type=F32)
    if n_out == 1:
        outs[0][...] = (y * out_scales[0]).astype(outs[0].dtype)
    else:
        for k, o_ref in enumerate(outs):
            @pl.when(j // blocks_per_out == k)
            def _(o_ref=o_ref, k=k):
                o_ref[...] = (y if out_scales[k] == 1.0 else y * out_scales[k]).astype(o_ref.dtype)


def _in_proj(x, n_seq, T, g, mod, layer, sc_chunk, sh_chunk, w, wi, out_dtypes, out_scales, stacks=None, slot=0,
             n_slots=1):
    N, D = x.shape
    n_total = w.shape[2]
    n_out = len(out_dtypes)
    S, R = _split_rows(n_seq, T, ROWS_IN_PROJ)
    rows = S * R
    width = n_total // n_out
    tn = _largest_divisor(width, (COLS_IN_PROJ, 256, 128))
    bpo = width // tn

    def mod_spec(chunk):
        return pl.BlockSpec((None, S, 1, D), lambda i, j: (layer, _seq_block(i, S, R, T), 0, chunk))

    def col(j, k):
        return jnp.clip(j - k * bpo, 0, bpo - 1)

    out_specs = [pl.BlockSpec((rows, tn), lambda i, j: (i, col(j, 0)))]
    out_shape = [jax.ShapeDtypeStruct((N, width), out_dtypes[0])]
    for k in range(1, n_out):
        out_specs.append(pl.BlockSpec((None, rows, tn), functools.partial(lambda i, j, k: (slot, i, col(j, k)), k=k)))
        out_shape.append(jax.ShapeDtypeStruct((n_slots, N, width), out_dtypes[k]))
    in_specs = [pl.BlockSpec((rows, D), lambda i, j: (i, 0)),
                pl.BlockSpec((1, D), lambda i, j: (0, 0)),
                mod_spec(sc_chunk), mod_spec(sh_chunk),
                pl.BlockSpec((None, D, tn), lambda i, j: (wi, 0, j))]
    args = [x, g.reshape(1, D), mod, mod, w]
    aliases = {}
    if stacks is not None:
        for k, st in enumerate(stacks):
            aliases[len(args)] = k + 1
            in_specs.append(pl.BlockSpec(memory_space=pl.ANY))
            args.append(st)
    return pl.pallas_call(
        functools.partial(_in_proj_kernel, S=S, R=R, blocks_per_out=bpo, n_out=n_out, out_scales=tuple(out_scales)),
        grid=(N // rows, n_total // tn),
        in_specs=in_specs, out_specs=out_specs, out_shape=out_shape,
        scratch_shapes=[pltpu.VMEM((rows, D), BF16)],
        input_output_aliases=aliases,
        compiler_params=_params("parallel", "arbitrary"),
        name="in_proj",
    )(*args)


def _sgu_matrices(ws_ref, bs_ref, g, T, chunk, whole_chunks):
    w = ws_ref[g]
    b = jnp.broadcast_to(bs_ref[g], (chunk, chunk))
    ii = lax.broadcasted_iota(jnp.int32, (chunk, chunk), 0)
    jj = lax.broadcasted_iota(jnp.int32, (chunk, chunk), 1)
    if whole_chunks:
        return jnp.where(jj // GMLP_CAUSAL_BLOCK <= ii // GMLP_CAUSAL_BLOCK, w, 0.0), b
    w_tl = jnp.where((ii < T) & (jj < T), w, 0.0)
    b_tl = jnp.where(ii < T, b, 0.0)
    wm, bm = w_tl, b_tl
    for k in range(1, chunk // T):
        wm = wm + pltpu.roll(pltpu.roll(w_tl, k * T, 0), k * T, 1)
        bm = bm + pltpu.roll(b_tl, k * T, 0)
    return wm, bm


def _mix_even_kernel(*refs, S, R, T, chunk, groups, halo, emit_vn):
    refs = list(refs)
    xa_ref, gb_ref, gc_ref, u_ref, v_ref = refs[:5]
    del refs[:5]
    if halo:
        hxa_ref, hgc_ref = refs[:2]
        del refs[:2]
    st_ref, x_ref, g1_ref, wc_ref, gs_ref, ws_ref, bs_ref, wo_ref, gn_ref = refs[:9]
    del refs[:9]
    xo_ref, tail_ref = refs[:2]
    del refs[:2]
    if emit_vn:
        vn_out_ref = refs.pop(0)
    cat_ref, vn_ref = refs

    i = pl.program_id(0)
    rows = S * R
    C = xa_ref.shape[1]

    ca = gc_ref[...] * xa_ref[...]
    st = st_ref[...]
    p0, p1 = st[:, 0:1, :], st[:, 1:2, :]
    if halo:
        hca = hgc_ref[...] * hxa_ref[...]
        first = (i % (T // R)) == 0
        p0 = jnp.where(first, p0, hca[SUBLANES - 2:SUBLANES - 1][None])
        p1 = jnp.where(first, p1, hca[SUBLANES - 1:SUBLANES][None])
    conv = _causal_conv(ca, _per_row(p0, R), _per_row(p1, R), wc_ref[...], R, S)
    cat_ref[:, 0:C] = (gb_ref[...] * conv).astype(BF16)
    tail_ref[...] = ca.reshape(S, R, C)[:, R - SUBLANES:R, :]

    vn = _rms(v_ref[...], gs_ref[...])
    if emit_vn:
        vn_out_ref[...] = vn
    vn_ref[...] = vn.astype(BF16)
    gd = v_ref.shape[1] // groups
    for g in range(groups):
        wm, bm = _sgu_matrices(ws_ref, bs_ref, g, T, chunk, whole_chunks=(R % chunk == 0))
        wm = wm.astype(BF16)
        cs = slice(g * gd, (g + 1) * gd)
        for c in range(rows // chunk):
            rs = slice(c * chunk, (c + 1) * chunk)
            mixed = jnp.dot(wm, vn_ref[rs, cs], preferred_element_type=F32) + bm
            cat_ref[rs, C + g * gd:C + (g + 1) * gd] = (u_ref[rs, cs] * mixed).astype(BF16)

    y = jnp.dot(cat_ref[...], wo_ref[...], preferred_element_type=F32)
    xo_ref[...] = x_ref[...] + _per_row(g1_ref[...], R) * _rms(y, gn_ref[...])


def _mix_even(z, x, n_seq, T, state, mod, layer, wi, w_conv, g_sgu, w_sgu, b_sgu, w_out, g_norm, emit_vn):
    N, D = x.shape
    C = w_conv.shape[2]
    groups, chunk = w_sgu.shape[1], w_sgu.shape[2]
    Cb = g_sgu.shape[1]
    S, R = _split_rows(n_seq, T, ROWS_MIX)
    rows = S * R
    halo = R < T
    assert rows % chunk == 0 and (R % chunk == 0 or (chunk % T == 0 and T <= GMLP_CAUSAL_BLOCK))
    assert z.shape[1] == 3 * C + 2 * Cb and C == Cb

    in_specs = [pl.BlockSpec((rows, C), functools.partial(lambda i, k: (i, k), k=k)) for k in range(5)]
    args = [z] * 5
    if halo:
        per = R // SUBLANES
        in_specs += [pl.BlockSpec((SUBLANES, C), functools.partial(
            lambda i, k: (jnp.maximum(i * per - 1, 0), k), k=k)) for k in (0, 2)]
        args += [z, z]
    in_specs += [
        pl.BlockSpec((None, S, CONV_TAPS - 1, C), lambda i: (wi, _seq_block(i, S, R, T), 0, 0)),
        pl.BlockSpec((rows, D), lambda i: (i, 0)),
        pl.BlockSpec((None, S, 1, D), lambda i: (layer, _seq_block(i, S, R, T), 0, 2)),
        pl.BlockSpec((None, CONV_TAPS, C), lambda i: (wi, 0, 0)),
        pl.BlockSpec((None, 1, Cb), lambda i: (wi, 0, 0)),
        pl.BlockSpec((None, groups, chunk, chunk), lambda i: (wi, 0, 0, 0)),
        pl.BlockSpec((None, groups, chunk, 1), lambda i: (wi, 0, 0, 0)),
        pl.BlockSpec((None, C + Cb, D), lambda i: (wi, 0, 0)),
        pl.BlockSpec((1, D), lambda i: (0, 0)),
    ]
    n_even = w_conv.shape[0]
    args += [state, x, mod, w_conv, g_sgu.reshape(n_even, 1, Cb), w_sgu, b_sgu.reshape(n_even, groups, chunk, 1),
             w_out, g_norm.reshape(1, D)]
    out_specs = [pl.BlockSpec((rows, D), lambda i: (i, 0)),
                 pl.BlockSpec((S, SUBLANES, C), lambda i: (i, 0, 0))]
    out_shape = [jax.ShapeDtypeStruct((N, D), F32),
                 jax.ShapeDtypeStruct((N // rows * S, SUBLANES, C), F32)]
    if emit_vn:
        out_specs.append(pl.BlockSpec((rows, Cb), lambda i: (i, 0)))
        out_shape.append(jax.ShapeDtypeStruct((N, Cb), F32))
    return pl.pallas_call(
        functools.partial(_mix_even_kernel, S=S, R=R, T=T, chunk=chunk, groups=groups, halo=halo, emit_vn=emit_vn),
        grid=(N // rows,),
        in_specs=in_specs, out_specs=out_specs, out_shape=out_shape,
        scratch_shapes=[pltpu.VMEM((rows, C + Cb), BF16), pltpu.VMEM((rows, Cb), BF16)],
        compiler_params=_params("parallel"),
        name="mix_even",
    )(*args)


def _upper_inclusive(n):
    r = lax.broadcasted_iota(jnp.int32, (n, n), 0)
    c = lax.broadcasted_iota(jnp.int32, (n, n), 1)
    return jnp.where(r >= c, 1.0, 0.0).astype(BF16)


def _strictly_earlier(n):
    t = lax.broadcasted_iota(jnp.int32, (n, n), 0)
    s = lax.broadcasted_iota(jnp.int32, (n, n), 1)
    return s < t


def _sb_blocks(qs, ks, vs, u_inc, cs, accs, mask):
    tq = qs[0].shape[0]
    zs, parts = [], []
    for q, k in zip(qs, ks):
        z = jnp.minimum(lax.dot_general(q, k, (((1,), (1,)), ((), ())), preferred_element_type=F32), SB_Z_MAX)
        fail = jnp.log(1.0 + jnp.exp2(z))
        if mask is not None:
            fail = jnp.where(mask, fail, 0.0)
        hi = fail.astype(BF16)
        parts += [hi, (fail - hi.astype(F32)).astype(BF16)]
        zs.append(z)
    sums = jnp.dot(jnp.concatenate(parts, axis=0), u_inc, preferred_element_type=F32)
    cs_out, accs_out = [], []
    for h, (z, v, c, acc) in enumerate(zip(zs, vs, cs, accs)):
        incl = sums[2 * h * tq:(2 * h + 1) * tq] + sums[(2 * h + 1) * tq:(2 * h + 2) * tq]
        w = jnp.exp2(z - (incl + c) * LOG2E)
        if mask is not None:
            w = jnp.where(mask, w, 0.0)
        accs_out.append(acc + jnp.dot(w.astype(BF16), v, preferred_element_type=F32))
        cs_out.append(c + incl[:, 0:1])
    return tuple(cs_out), tuple(accs_out)


def _unfinished(cs):
    return (jnp.min(functools.reduce(jnp.minimum, cs)) < SB_DONE).astype(jnp.int32)


def _attn_prompt_kernel(q_ref, k_ref, v_ref, o_ref, kb_ref, vb_ref, u_ref, *, tq, heads, dh):
    qi = pl.program_id(2)

    @pl.when(qi == 0)
    def _():
        kb_ref[...] = k_ref[...].astype(BF16)
        vb_ref[...] = v_ref[...].astype(BF16)
        u_ref[...] = _upper_inclusive(tq)

    u = u_ref[...]
    cols = [slice(h * dh, (h + 1) * dh) for h in range(heads)]
    qs = [q_ref[:, cs] for cs in cols]

    def sweep(n, cs, accs, mask):
        rows = pl.ds(pl.multiple_of((qi - n) * tq, tq), tq)
        return _sb_blocks(qs, [kb_ref[rows, cs_] for cs_ in cols], [vb_ref[rows, cs_] for cs_ in cols], u, cs, accs,
                          mask)

    cs, accs = sweep(0, (jnp.zeros((tq, 1), F32),) * heads, (jnp.zeros((tq, dh), F32),) * heads,
                     _strictly_earlier(tq))

    def body(carry):
        n, _, cs, accs = carry
        cs, accs = sweep(n, cs, accs, None)
        return n + 1, _unfinished(cs), cs, accs

    _, _, _, accs = lax.while_loop(lambda carry: (carry[0] <= qi) & (carry[1] > 0), body,
                                   (jnp.int32(1), _unfinished(cs), cs, accs))
    for h in range(heads):
        o_ref[:, cols[h]] = accs[h].astype(o_ref.dtype)


def _attn_prompt(q, k_stack, v_stack, slot, n_seq, T, heads):
    N, D = q.shape
    dh = D // heads
    tq = min(ATTN_BLOCK, T)
    hp = _largest_divisor(heads, (ATTN_PROMPT_HEADS, 1))
    assert T % tq == 0
    n_slots = k_stack.shape[0]
    kv_spec = pl.BlockSpec((None, None, T, hp * dh), lambda b, h, i: (slot, b, 0, h))
    o = pl.pallas_call(
        functools.partial(_attn_prompt_kernel, tq=tq, heads=hp, dh=dh),
        grid=(n_seq, heads // hp, T // tq),
        in_specs=[pl.BlockSpec((None, tq, hp * dh), lambda b, h, i: (b, i, h)), kv_spec, kv_spec],
        out_specs=pl.BlockSpec((None, tq, hp * dh), lambda b, h, i: (b, i, h)),
        out_shape=jax.ShapeDtypeStruct((n_seq, T, D), BF16),
        scratch_shapes=[pltpu.VMEM((T, hp * dh), BF16), pltpu.VMEM((T, hp * dh), BF16),
                        pltpu.VMEM((tq, tq), BF16)],
        compiler_params=_params("parallel", "parallel", "arbitrary"),
        name="attn_prompt",
    )(q.reshape(n_seq, T, D), k_stack.reshape(n_slots, n_seq, T, D), v_stack.reshape(n_slots, n_seq, T, D))
    return o.reshape(N, D)


def _attn_sample_kernel(q_ref, kn_ref, vn_ref, ck_ref, cv_ref, o_ref, *, T, P, tk, heads, all_heads, dh):
    u_past = _upper_inclusive(tk)
    cols = [slice(h * dh, (h + 1) * dh) for h in range(heads)]
    qs = [q_ref[:, cs] for cs in cols]

    cs, accs = _sb_blocks(qs, [kn_ref[:, c].astype(BF16) for c in cols], [vn_ref[:, c].astype(BF16) for c in cols],
                          _upper_inclusive(T), (jnp.zeros((T, 1), F32),) * heads,
                          (jnp.zeros((T, dh), F32),) * heads, _strictly_earlier(T))

    h0 = pl.program_id(1) * heads

    def body(carry):
        n, _, cs, accs = carry
        r0 = (P - (n + 1) * tk) * all_heads + h0
        ks = [ck_ref[pl.ds(r0 + h, tk, stride=all_heads), :].astype(BF16) for h in range(heads)]
        vs = [cv_ref[pl.ds(r0 + h, tk, stride=all_heads), :].astype(BF16) for h in range(heads)]
        cs, accs = _sb_blocks(qs, ks, vs, u_past, cs, accs, None)
        return n + 1, _unfinished(cs), cs, accs

    _, _, _, accs = lax.while_loop(lambda carry: (carry[0] < P // tk) & (carry[1] > 0), body,
                                   (jnp.int32(0), _unfinished(cs), cs, accs))
    for h in range(heads):
        o_ref[:, cols[h]] = accs[h].astype(o_ref.dtype)


def _attn_sample(q, k_stack, v_stack, slot, cache_k, cache_v, n_seq, T, heads):
    N, D = q.shape
    dh = D // heads
    P = cache_k.shape[2]
    tk = _largest_divisor(P, (ATTN_BLOCK, 128))
    hg = _largest_divisor(heads, (ATTN_SAMPLE_HEADS, heads))
    n_slots = k_stack.shape[0]
    q_spec = pl.BlockSpec((None, T, hg * dh), lambda b, g: (b, 0, g))
    new_spec = pl.BlockSpec((None, None, T, hg * dh), lambda b, g: (slot, b, 0, g))
    past_spec = pl.BlockSpec((None, None, P * heads, dh), lambda b, g: (slot, b, 0, 0))
    past_shape = cache_k.shape[:2] + (P * heads, dh)
    o = pl.pallas_call(
        functools.partial(_attn_sample_kernel, T=T, P=P, tk=tk, heads=hg, all_heads=heads, dh=dh),
        grid=(n_seq, heads // hg),
        in_specs=[q_spec, new_spec, new_spec, past_spec, past_spec],
        out_specs=q_spec,
        out_shape=jax.ShapeDtypeStruct((n_seq, T, D), BF16),
        compiler_params=_params("parallel", "arbitrary"),
        name="attn_sample",
    )(q.reshape(n_seq, T, D), k_stack.reshape(n_slots, n_seq, T, D), v_stack.reshape(n_slots, n_seq, T, D),
      cache_k.reshape(past_shape), cache_v.reshape(past_shape))
    return o.reshape(N, D)


def _out_proj_kernel(o_ref, w_ref, x_ref, g1_ref, gn_ref, xo_ref, *, R):
    y = jnp.dot(o_ref[...], w_ref[...], preferred_element_type=F32)
    xo_ref[...] = x_ref[...] + _per_row(g1_ref[...], R) * _rms(y, gn_ref[...])


def _out_proj(o, w, wi, x, n_seq, T, mod, layer, g_norm):
    N, D = x.shape
    S, R = _split_rows(n_seq, T, ROWS_OUT_PROJ)
    rows = S * R
    return pl.pallas_call(
        functools.partial(_out_proj_kernel, R=R),
        grid=(N // rows,),
        in_specs=[pl.BlockSpec((rows, D), lambda i: (i, 0)),
                  pl.BlockSpec((None, D, D), lambda i: (wi, 0, 0)),
                  pl.BlockSpec((rows, D), lambda i: (i, 0)),
                  pl.BlockSpec((None, S, 1, D), lambda i: (layer, _seq_block(i, S, R, T), 0, 2)),
                  pl.BlockSpec((1, D), lambda i: (0, 0))],
        out_specs=pl.BlockSpec((rows, D), lambda i: (i, 0)),
        out_shape=jax.ShapeDtypeStruct((N, D), F32),
        compiler_params=_params("parallel"),
        name="out_proj",
    )(o, w, x, mod, g_norm.reshape(1, D))


def _ffn_kernel(x_ref, xp_ref, gn_in_ref, sc_ref, sh_ref, wa_ref, wg_ref, wc_ref, bc_ref, wd_ref, st_ref, g2_ref,
                gn_out_ref, xo_ref, tail_ref, h_ref, acc_ref, act0_ref, act1_ref, carry_ref, *, S, R, T, nf, nb):
    i, j = pl.program_id(0), pl.program_id(1)
    acts = (act0_ref, act1_ref)
    last_act = acts[(nf - 1) % 2]

    def up_gate(dst_ref):
        h = h_ref[...]
        a = jnp.dot(h, wa_ref[...], preferred_element_type=F32)
        gate = jnp.dot(h, wg_ref[...], preferred_element_type=F32)
        tail = a.reshape(S, R, -1)[:, R - SUBLANES:R, :]
        tail_ref[...] = tail
        p0, p1 = st_ref[:, 0:1, :], st_ref[:, 1:2, :]
        if R < T:
            first = (i % (T // R)) == 0
            prev = carry_ref[j]
            p0 = jnp.where(first, p0, prev[SUBLANES - 2:SUBLANES - 1][None])
            p1 = jnp.where(first, p1, prev[SUBLANES - 1:SUBLANES][None])
            carry_ref[j] = tail[0]
        half = 0.5 * (_causal_conv(a, _per_row(p0, R), _per_row(p1, R), wc_ref[...], R, S) + bc_ref[...])
        dst_ref[...] = ((half + half * jnp.tanh(half)) * gate).astype(BF16)

    def down(src_ref):
        acc_ref[...] += jnp.dot(src_ref[...], wd_ref[...], preferred_element_type=F32)

    def finish_previous():
        down(last_act)
        _residual_norm_rows(xp_ref, acc_ref, g2_ref, gn_out_ref[...], xo_ref, S, R, straight_line=True)

    def start_block():
        _norm_mod_rows(x_ref, gn_in_ref[...], sc_ref, sh_ref, h_ref, S, R, straight_line=True)
        acc_ref[...] = jnp.zeros_like(acc_ref)
        up_gate(acts[0])

    @pl.when((j == 0) & (i == 0))
    def _():
        if R < T:
            carry_ref[...] = jnp.zeros_like(carry_ref)
        start_block()

    @pl.when((j == 0) & (i > 0) & (i < nb))
    def _():
        finish_previous()
        start_block()

    @pl.when((j == 0) & (i == nb))
    def _():
        finish_previous()

    for parity in range(2):
        @pl.when((j > 0) & (i < nb) & (j % 2 == parity))
        def _(parity=parity):
            up_gate(acts[parity])
            down(acts[1 - parity])


def _ffn(x, n_seq, T, state, mod, layer, g_in, w_up, w_conv, b_conv, w_down, g_out):
    N, D = x.shape
    L, F = w_down.shape[0], w_down.shape[1]
    S, R = _split_rows(n_seq, T, ROWS_FFN)
    rows = S * R
    tf = _largest_divisor(F, (COLS_FFN, 256, 128))
    nf = F // tf

    nb = N // rows

    def cur(i):
        return jnp.minimum(i, nb - 1)

    def prev(i):
        return jnp.maximum(i - 1, 0)

    def up_chunk(i, j):
        return jnp.where(i < nb, j, nf - 1)

    def down_chunk(i, j):
        return jnp.where((i == nb) & (j > 0), nf - 1, (j + nf - 1) % nf)

    def mod_spec(chunk, blk):
        return pl.BlockSpec((None, S, 1, D), lambda i, j: (layer, _seq_block(blk(i), S, R, T), 0, chunk))

    return pl.pallas_call(
        functools.partial(_ffn_kernel, S=S, R=R, T=T, nf=nf, nb=nb),
        grid=(nb + 1, nf),
        in_specs=[pl.BlockSpec((rows, D), lambda i, j: (cur(i), 0), pipeline_mode=pl.Buffered(1)),
                  pl.BlockSpec((rows, D), lambda i, j: (prev(i), 0), pipeline_mode=pl.Buffered(1)),
                  pl.BlockSpec((1, D), lambda i, j: (0, 0)),
                  mod_spec(4, cur), mod_spec(3, cur),
                  pl.BlockSpec((None, D, tf), lambda i, j: (layer, 0, up_chunk(i, j))),
                  pl.BlockSpec((None, D, tf), lambda i, j: (layer, 0, nf + up_chunk(i, j))),
                  pl.BlockSpec((None, CONV_TAPS, tf), lambda i, j: (layer, 0, up_chunk(i, j))),
                  pl.BlockSpec((None, 1, tf), lambda i, j: (layer, 0, up_chunk(i, j))),
                  pl.BlockSpec((None, tf, D), lambda i, j: (layer, down_chunk(i, j), 0)),
                  pl.BlockSpec((None, S, CONV_TAPS - 1, tf),
                               lambda i, j: (layer, _seq_block(cur(i), S, R, T), 0, up_chunk(i, j))),
                  mod_spec(5, prev),
                  pl.BlockSpec((1, D), lambda i, j: (0, 0))],
        out_specs=[pl.BlockSpec((rows, D), lambda i, j: (prev(i), 0), pipeline_mode=pl.Buffered(1)),
                   pl.BlockSpec((S, SUBLANES, tf), lambda i, j: (cur(i), 0, up_chunk(i, j)))],
        out_shape=[jax.ShapeDtypeStruct((N, D), F32),
                   jax.ShapeDtypeStruct((N // rows * S, SUBLANES, F), F32)],
        scratch_shapes=[pltpu.VMEM((rows, D), BF16), pltpu.VMEM((rows, D), F32),
                        pltpu.VMEM((rows, tf), BF16), pltpu.VMEM((rows, tf), BF16),
                        pltpu.VMEM((nf, SUBLANES, tf), F32)],
        compiler_params=_params("arbitrary", "arbitrary"),
        name="ffn",
    )(x, x, g_in.reshape(1, D), mod, mod, w_up, w_up, w_conv, b_conv.reshape(L, 1, F), w_down, state, mod,
      g_out.reshape(1, D))


def _last_rows(tails, n_seq):
    t = tails.reshape(n_seq, tails.shape[0] // n_seq, SUBLANES, tails.shape[2])
    return t[:, -1, SUBLANES - (CONV_TAPS - 1):, :]


def _trunk(x, mod, cache_k, cache_v, conv_a_prev, ffn_prev, norm_g, w_in_ab, w_conv_a, g_sgu, w_sgu, b_sgu,
           w_out_ab, w_qkv_sb, w_o_sb, w_ffn_up, w_ffn_conv, b_ffn_conv, w_ffn_down, heads, emit_vn):
    n_seq, T, D = x.shape
    x = x.reshape(n_seq * T, D)
    depth = norm_g.shape[0]
    n_odd = w_qkv_sb.shape[0]
    q_scale = (D // heads) ** -0.5 * LOG2E
    kv = None
    conv_a_new, ffn_new, sgu_v_new = [], [], []
    for l in range(depth):
        i = l // 2
        if l % 2 == 0:
            (z,) = _in_proj(x, n_seq, T, norm_g[l, 0], mod, l, 1, 0, w_in_ab, i, (F32,), (1.0,))
            res = _mix_even(z, x, n_seq, T, conv_a_prev, mod, l, i, w_conv_a, g_sgu, w_sgu, b_sgu, w_out_ab,
                            norm_g[l, 1], emit_vn)
            x = res[0]
            conv_a_new.append(_last_rows(res[1], n_seq))
            if emit_vn:
                sgu_v_new.append(res[2].reshape(n_seq, T, -1))
        else:
            q, *kv = _in_proj(x, n_seq, T, norm_g[l, 0], mod, l, 1, 0, w_qkv_sb, i, (BF16, F32, F32),
                              (q_scale, 1.0, 1.0), stacks=kv, slot=i, n_slots=n_odd)
            if cache_k is None:
                o = _attn_prompt(q, kv[0], kv[1], i, n_seq, T, heads)
            else:
                o = _attn_sample(q, kv[0], kv[1], i, cache_k, cache_v, n_seq, T, heads)
            x = _out_proj(o, w_o_sb, i, x, n_seq, T, mod, l, norm_g[l, 1])
        x, tails = _ffn(x, n_seq, T, ffn_prev, mod, l, norm_g[l, 2], w_ffn_up, w_ffn_conv, b_ffn_conv, w_ffn_down,
                        norm_g[l, 3])
        ffn_new.append(_last_rows(tails, n_seq))
    k_new, v_new = (a.reshape(n_odd, n_seq, T, heads, D // heads) for a in kv)
    return x.reshape(n_seq, T, D), k_new, v_new, jnp.stack(conv_a_new), jnp.stack(ffn_new), sgu_v_new


def kernel(x_prompt, x_sample, c_prompt, c_sample, cache_sb_k, cache_sb_v, state_conv_a, state_ffn_conv, w_mod, b_mod,
           norm_g, w_in_ab, w_conv_a, g_sgu, w_sgu, b_sgu, w_out_ab, w_qkv_sb, w_o_sb, w_ffn_up, w_ffn_conv,
           b_ffn_conv, w_ffn_down):
    bp, bs = x_prompt.shape[0], x_sample.shape[0]
    depth = norm_g.shape[0]
    heads = cache_sb_k.shape[3]

    mod = _modulation(jnp.concatenate([c_prompt, c_sample], axis=0), w_mod, b_mod)
    mod_p = mod[:, :bp].reshape(depth, bp, 1, -1)
    mod_s = mod[:, bp:].reshape(depth, bs, 1, -1)

    weights = (norm_g, w_in_ab.astype(BF16), w_conv_a, g_sgu, w_sgu, b_sgu, w_out_ab.astype(BF16),
               w_qkv_sb.astype(BF16), w_o_sb.astype(BF16), w_ffn_up.astype(BF16), w_ffn_conv, b_ffn_conv,
               w_ffn_down.astype(BF16))

    zeros_a = jnp.zeros((state_conv_a.shape[0], bp) + state_conv_a.shape[2:], F32)
    zeros_f = jnp.zeros((depth, bp) + state_ffn_conv.shape[2:], F32)
    y_s, k_s, v_s, conv_a_s, ffn_s, sgu_v = _trunk(x_sample, mod_s, cache_sb_k, cache_sb_v, state_conv_a,
                                                   state_ffn_conv, *weights, heads=heads, emit_vn=True)
    sample_out, (x_prompt, mod_p) = lax.optimization_barrier(
        ((y_s, k_s, v_s, conv_a_s, ffn_s, sgu_v), (x_prompt, mod_p)))
    y_s, k_s, v_s, conv_a_s, ffn_s, sgu_v = sample_out
    y_p, k_p, v_p, conv_a_p, ffn_p, _ = _trunk(x_prompt, mod_p, None, None, zeros_a, zeros_f, *weights,
                                               heads=heads, emit_vn=False)
    return (y_p, y_s, k_p, v_p, conv_a_p, ffn_p, k_s, v_s, conv_a_s, ffn_s, jnp.stack(sgu_v))
```

```python
import functools
import math

import jax
import jax.numpy as jnp
from jax import lax
from jax.experimental import pallas as pl
from jax.experimental.pallas import tpu as pltpu

F32 = jnp.float32
BF16 = jnp.bfloat16

EPS = 1e-6
CONV_TAPS = 3
GMLP_CAUSAL_BLOCK = 64
SUBLANES = 8
VMEM_LIMIT_BYTES = 56 * 1024 * 1024

ROWS_IN_PROJ = 1024
ROWS_MIX = 256
ROWS_OUT_PROJ = 512
ROWS_FFN = 512
COLS_IN_PROJ = 512
COLS_FFN = 512
COLS_MOD = 1024
ATTN_BLOCK = 256
ATTN_PROMPT_HEADS = 4
ATTN_SAMPLE_HEADS = 8

LOG2E = math.log2(math.e)
SB_Z_MAX = 126.0
SB_DONE = 150.0 / LOG2E


def _params(*semantics):
    return pltpu.CompilerParams(dimension_semantics=semantics, vmem_limit_bytes=VMEM_LIMIT_BYTES)


def _split_rows(n_seq, seq_len, rows):
    if seq_len >= rows:
        assert seq_len % rows == 0
        return 1, rows
    s = min(rows // seq_len, n_seq)
    assert n_seq % s == 0 and seq_len % SUBLANES == 0
    return s, seq_len


def _seq_block(i, S, R, T):
    return i // (T // R) if S == 1 else i


def _largest_divisor(n, candidates):
    for c in candidates:
        if n % c == 0:
            return c
    raise ValueError(f"no block size among {candidates} divides {n}")


def _rms(x, g):
    return x * lax.rsqrt(jnp.mean(x * x, axis=-1, keepdims=True) + EPS) * g


def _per_row(v, R):
    S, _, C = v.shape
    if S == 1:
        return v[0]
    return jnp.broadcast_to(v, (S, R, C)).reshape(S * R, C)


def _sigmoid(x):
    return 1.0 / (1.0 + jnp.exp(-x))


def _conv_taps(a, p0, p1, w, R):
    rt = lax.broadcasted_iota(jnp.int32, a.shape, 0) & (R - 1)
    s1 = jnp.where(rt == 0, p1, pltpu.roll(a, 1, 0))
    s2 = jnp.where(rt == 0, p0, jnp.where(rt == 1, p1, pltpu.roll(a, 2, 0)))
    return s2 * w[0:1] + s1 * w[1:2] + a * w[2:3]


def _causal_conv(a, p0, p1, w, R, S):
    if S > 1:
        return _conv_taps(a, p0, p1, w, R)
    body = pltpu.roll(a, 2, 0) * w[0:1] + pltpu.roll(a, 1, 0) * w[1:2] + a * w[2:3]
    head = _conv_taps(a[0:SUBLANES], p0, p1, w, R)
    return jnp.concatenate([head, body[SUBLANES:]], axis=0)


NORM_ROWS = 128


def _row_chunks(ref_sc, c, S, R, step):
    if S == 1:
        return ref_sc[0]
    k = step // R
    return _per_row(ref_sc[pl.ds(c * k, k)], R)


def _norm_mod_rows(x_ref, g, sc_ref, sh_ref, h_ref, S, R, straight_line=False):
    rows = S * R
    step = min(NORM_ROWS, rows)
    assert rows % step == 0 and (S == 1 or step % R == 0)

    def body(c, carry):
        rs = pl.ds(c * step, step) if straight_line else pl.ds(pl.multiple_of(c * step, step), step)
        h = _rms(x_ref[rs, :], g) * (1.0 + _row_chunks(sc_ref, c, S, R, step)) + _row_chunks(sh_ref, c, S, R, step)
        h_ref[rs, :] = h.astype(BF16)
        return carry

    _for_chunks(rows // step, body, straight_line)


def _residual_norm_rows(x_ref, y_ref, gate_ref, g, xo_ref, S, R, straight_line=False):
    rows = S * R
    step = min(NORM_ROWS, rows)
    assert rows % step == 0 and (S == 1 or step % R == 0)

    def body(c, carry):
        rs = pl.ds(c * step, step) if straight_line else pl.ds(pl.multiple_of(c * step, step), step)
        xo_ref[rs, :] = x_ref[rs, :] + _row_chunks(gate_ref, c, S, R, step) * _rms(y_ref[rs, :], g)
        return carry

    _for_chunks(rows // step, body, straight_line)


def _for_chunks(n, body, straight_line):
    if straight_line:
        for c in range(n):
            body(c, 0)
    else:
        lax.fori_loop(0, n, body, 0)


def _mod_kernel(c_ref, w_ref, b_ref, o_ref):
    c = c_ref[...]
    s = (c * _sigmoid(c)).astype(BF16)
    o_ref[...] = jnp.dot(s, w_ref[...].astype(BF16), preferred_element_type=F32) + b_ref[...]


def _modulation(c, w_mod, b_mod):
    L, D, N = w_mod.shape
    B = c.shape[0]
    tn = _largest_divisor(N, (COLS_MOD, 512, 256, 128))
    return pl.pallas_call(
        _mod_kernel,
        grid=(L, N // tn),
        in_specs=[pl.BlockSpec((B, D), lambda l, j: (0, 0)),
                  pl.BlockSpec((None, D, tn), lambda l, j: (l, 0, j)),
                  pl.BlockSpec((None, 1, tn), lambda l, j: (l, 0, j))],
        out_specs=pl.BlockSpec((None, B, tn), lambda l, j: (l, 0, j)),
        out_shape=jax.ShapeDtypeStruct((L, B, N), F32),
        compiler_params=_params("parallel", "parallel"),
        name="modulation",
    )(c, w_mod, b_mod.reshape(L, 1, N))


def _in_proj_kernel(x_ref, g_ref, sc_ref, sh_ref, w_ref, *rest, S, R, nj, blocks_per_out, n_out, out_scales,
                    fill_slots):
    outs, hs = rest[-2 - n_out:-2], rest[-2:]
    i, j = pl.program_id(0), pl.program_id(1)

    def store(k, y):
        val = (y if out_scales[k] == 1.0 else y * out_scales[k]).astype(outs[k].dtype)
        if k > 0 and fill_slots:
            for s in range(outs[k].shape[0]):
                outs[k][s] = val
        else:
            outs[k][...] = val

    @pl.when((i == 0) & (j == 0))
    def _():
        _norm_mod_rows(x_ref, g_ref[...], sc_ref, sh_ref, hs[0], S, R)

    for parity in range(2):
        @pl.when((i % 2 == parity) & (j < nj - 1))
        def _(parity=parity):
            y = jnp.dot(hs[parity][...], w_ref[...], preferred_element_type=F32)
            if n_out == 1:
                store(0, y)
            else:
                for k in range(n_out):
                    @pl.when(j // blocks_per_out == k)
                    def _(k=k):
                        store(k, y)

        @pl.when((i % 2 == parity) & (j == nj - 1))
        def _(parity=parity):
            _norm_mod_rows(x_ref, g_ref[...], sc_ref, sh_ref, hs[1 - parity], S, R, straight_line=True)
            store(n_out - 1, jnp.dot(hs[parity][...], w_ref[...], preferred_element_type=F32))


def _in_proj(x, n_seq, T, g, mod, layer, sc_chunk, sh_chunk, w, wi, out_dtypes, out_scales, stacks=None, slot=0,
             n_slots=1):
    N, D = x.shape
    n_total = w.shape[2]
    n_out = len(out_dtypes)
    S, R = _split_rows(n_seq, T, ROWS_IN_PROJ)
    rows = S * R
    nb = N // rows
    width = n_total // n_out
    tn = _largest_divisor(width, (COLS_IN_PROJ, 256, 128))
    bpo = width // tn
    nj = n_total // tn

    def norm_block(i, j):
        return jnp.where(j == nj - 1, jnp.minimum(i + 1, nb - 1), i)

    def mod_spec(chunk):
        return pl.BlockSpec((None, S, 1, D), lambda i, j: (layer, _seq_block(norm_block(i, j), S, R, T), 0, chunk))

    def col(j, k):
        return jnp.clip(j - k * bpo, 0, bpo - 1)

    out_specs = [pl.BlockSpec((rows, tn), lambda i, j: (i, col(j, 0)))]
    out_shape = [jax.ShapeDtypeStruct((N, width), out_dtypes[0])]
    for k in range(1, n_out):
        if stacks is None:
            spec = pl.BlockSpec((n_slots, rows, tn), functools.partial(lambda i, j, k: (0, i, col(j, k)), k=k))
        else:
            spec = pl.BlockSpec((None, rows, tn), functools.partial(lambda i, j, k: (slot, i, col(j, k)), k=k))
        out_specs.append(spec)
        out_shape.append(jax.ShapeDtypeStruct((n_slots, N, width), out_dtypes[k]))
    in_specs = [pl.BlockSpec((rows, D), lambda i, j: (norm_block(i, j), 0)),
                pl.BlockSpec((1, D), lambda i, j: (0, 0)),
                mod_spec(sc_chunk), mod_spec(sh_chunk),
                pl.BlockSpec((None, D, tn), lambda i, j: (wi, 0, j))]
    args = [x, g.reshape(1, D), mod, mod, w]
    aliases = {}
    if stacks is not None:
        for k, st in enumerate(stacks):
            aliases[len(args)] = k + 1
            in_specs.append(pl.BlockSpec(memory_space=pl.ANY))
            args.append(st)
    return pl.pallas_call(
        functools.partial(_in_proj_kernel, S=S, R=R, nj=nj, blocks_per_out=bpo, n_out=n_out,
                          out_scales=tuple(out_scales), fill_slots=stacks is None),
        grid=(nb, nj),
        in_specs=in_specs, out_specs=out_specs, out_shape=out_shape,
        scratch_shapes=[pltpu.VMEM((rows, D), BF16), pltpu.VMEM((rows, D), BF16)],
        input_output_aliases=aliases,
        compiler_params=_params("arbitrary", "arbitrary"),
        name="in_proj",
    )(*args)


def _sgu_matrices(ws_ref, bs_ref, g, T, chunk, whole_chunks):
    w = ws_ref[g]
    b = jnp.broadcast_to(bs_ref[g], (chunk, chunk))
    ii = lax.broadcasted_iota(jnp.int32, (chunk, chunk), 0)
    jj = lax.broadcasted_iota(jnp.int32, (chunk, chunk), 1)
    if whole_chunks:
        return jnp.where(jj // GMLP_CAUSAL_BLOCK <= ii // GMLP_CAUSAL_BLOCK, w, 0.0), b
    w_tl = jnp.where((ii < T) & (jj < T), w, 0.0)
    b_tl = jnp.where(ii < T, b, 0.0)
    wm, bm = w_tl, b_tl
    for k in range(1, chunk // T):
        wm = wm + pltpu.roll(pltpu.roll(w_tl, k * T, 0), k * T, 1)
        bm = bm + pltpu.roll(b_tl, k * T, 0)
    return wm, bm


def _mix_even_kernel(*refs, S, R, T, chunk, groups, halo, emit_vn):
    refs = list(refs)
    xa_ref, gb_ref, gc_ref, u_ref, v_ref = refs[:5]
    del refs[:5]
    if halo:
        hxa_ref, hgc_ref = refs[:2]
        del refs[:2]
    st_ref, x_ref, g1_ref, wc_ref, gs_ref, ws_ref, bs_ref, wo_ref, gn_ref = refs[:9]
    del refs[:9]
    xo_ref, tail_ref = refs[:2]
    del refs[:2]
    if emit_vn:
        vn_out_ref = refs.pop(0)
    cat_ref, vn_ref = refs

    i = pl.program_id(0)
    rows = S * R
    C = xa_ref.shape[1]

    ca = gc_ref[...] * xa_ref[...]
    st = st_ref[...]
    p0, p1 = st[:, 0:1, :], st[:, 1:2, :]
    if halo:
        hca = hgc_ref[...] * hxa_ref[...]
        first = (i % (T // R)) == 0
        p0 = jnp.where(first, p0, hca[SUBLANES - 2:SUBLANES - 1][None])
        p1 = jnp.where(first, p1, hca[SUBLANES - 1:SUBLANES][None])
    conv = _causal_conv(ca, _per_row(p0, R), _per_row(p1, R), wc_ref[...], R, S)
    cat_ref[:, 0:C] = (gb_ref[...] * conv).astype(BF16)
    tail_ref[...] = ca.reshape(S, R, C)[:, R - SUBLANES:R, :]

    vn = _rms(v_ref[...], gs_ref[...])
    if emit_vn:
        vn_out_ref[...] = vn
    vn_ref[...] = vn.astype(BF16)
    gd = v_ref.shape[1] // groups
    for g in range(groups):
        wm, bm = _sgu_matrices(ws_ref, bs_ref, g, T, chunk, whole_chunks=(R % chunk == 0))
        wm = wm.astype(BF16)
        cs = slice(g * gd, (g + 1) * gd)
        for c in range(rows // chunk):
            rs = slice(c * chunk, (c + 1) * chunk)
            mixed = jnp.dot(wm, vn_ref[rs, cs], preferred_element_type=F32) + bm
            cat_ref[rs, C + g * gd:C + (g + 1) * gd] = (u_ref[rs, cs] * mixed).astype(BF16)

    y = jnp.dot(cat_ref[...], wo_ref[...], preferred_element_type=F32)
    xo_ref[...] = x_ref[...] + _per_row(g1_ref[...], R) * _rms(y, gn_ref[...])


def _mix_even(z, x, n_seq, T, state, mod, layer, wi, w_conv, g_sgu, w_sgu, b_sgu, w_out, g_norm, emit_vn):
    N, D = x.shape
    C = w_conv.shape[2]
    groups, chunk = w_sgu.shape[1], w_sgu.shape[2]
    Cb = g_sgu.shape[1]
    S, R = _split_rows(n_seq, T, ROWS_MIX)
    rows = S * R
    halo = R < T
    assert rows % chunk == 0 and (R % chunk == 0 or (chunk % T == 0 and T <= GMLP_CAUSAL_BLOCK))
    assert z.shape[1] == 3 * C + 2 * Cb and C == Cb

    in_specs = [pl.BlockSpec((rows, C), functools.partial(lambda i, k: (i, k), k=k)) for k in range(5)]
    args = [z] * 5
    if halo:
        per = R // SUBLANES
        in_specs += [pl.BlockSpec((SUBLANES, C), functools.partial(
            lambda i, k: (jnp.maximum(i * per - 1, 0), k), k=k)) for k in (0, 2)]
        args += [z, z]
    in_specs += [
        pl.BlockSpec((None, S, CONV_TAPS - 1, C), lambda i: (wi, _seq_block(i, S, R, T), 0, 0)),
        pl.BlockSpec((rows, D), lambda i: (i, 0)),
        pl.BlockSpec((None, S, 1, D), lambda i: (layer, _seq_block(i, S, R, T), 0, 2)),
        pl.BlockSpec((None, CONV_TAPS, C), lambda i: (wi, 0, 0)),
        pl.BlockSpec((None, 1, Cb), lambda i: (wi, 0, 0)),
        pl.BlockSpec((None, groups, chunk, chunk), lambda i: (wi, 0, 0, 0)),
        pl.BlockSpec((None, groups, chunk, 1), lambda i: (wi, 0, 0, 0)),
        pl.BlockSpec((None, C + Cb, D), lambda i: (wi, 0, 0)),
        pl.BlockSpec((1, D), lambda i: (0, 0)),
    ]
    n_even = w_conv.shape[0]
    args += [state, x, mod, w_conv, g_sgu.reshape(n_even, 1, Cb), w_sgu, b_sgu.reshape(n_even, groups, chunk, 1),
             w_out, g_norm.reshape(1, D)]
    out_specs = [pl.BlockSpec((rows, D), lambda i: (i, 0)),
                 pl.BlockSpec((S, SUBLANES, C), lambda i: (i, 0, 0))]
    out_shape = [jax.ShapeDtypeStruct((N, D), F32),
                 jax.ShapeDtypeStruct((N // rows * S, SUBLANES, C), F32)]
    if emit_vn:
        out_specs.append(pl.BlockSpec((rows, Cb), lambda i: (i, 0)))
        out_shape.append(jax.ShapeDtypeStruct((N, Cb), F32))
    return pl.pallas_call(
        functools.partial(_mix_even_kernel, S=S, R=R, T=T, chunk=chunk, groups=groups, halo=halo, emit_vn=emit_vn),
        grid=(N // rows,),
        in_specs=in_specs, out_specs=out_specs, out_shape=out_shape,
        scratch_shapes=[pltpu.VMEM((rows, C + Cb), BF16), pltpu.VMEM((rows, Cb), BF16)],
        compiler_params=_params("parallel"),
        name="mix_even",
    )(*args)


def _upper_inclusive(n):
    r = lax.broadcasted_iota(jnp.int32, (n, n), 0)
    c = lax.broadcasted_iota(jnp.int32, (n, n), 1)
    return jnp.where(r >= c, 1.0, 0.0).astype(BF16)


def _strictly_earlier(n):
    t = lax.broadcasted_iota(jnp.int32, (n, n), 0)
    s = lax.broadcasted_iota(jnp.int32, (n, n), 1)
    return s < t


def _sb_blocks(qs, ks, vs, u_inc, cs, accs, mask):
    tq = qs[0].shape[0]
    zs, parts = [], []
    for q, k in zip(qs, ks):
        z = jnp.minimum(lax.dot_general(q, k, (((1,), (1,)), ((), ())), preferred_element_type=F32), SB_Z_MAX)
        fail = jnp.log(1.0 + jnp.exp2(z))
        if mask is not None:
            fail = jnp.where(mask, fail, 0.0)
        hi = fail.astype(BF16)
        parts += [hi, (fail - hi.astype(F32)).astype(BF16)]
        zs.append(z)
    sums = jnp.dot(jnp.concatenate(parts, axis=0), u_inc, preferred_element_type=F32)
    cs_out, accs_out = [], []
    for h, (z, v, c, acc) in enumerate(zip(zs, vs, cs, accs)):
        incl = sums[2 * h * tq:(2 * h + 1) * tq] + sums[(2 * h + 1) * tq:(2 * h + 2) * tq]
        w = jnp.exp2(z - (incl + c) * LOG2E)
        if mask is not None:
            w = jnp.where(mask, w, 0.0)
        accs_out.append(acc + jnp.dot(w.astype(BF16), v, preferred_element_type=F32))
        cs_out.append(c + incl[:, 0:1])
    return tuple(cs_out), tuple(accs_out)


def _unfinished(cs):
    return (jnp.min(functools.reduce(jnp.minimum, cs)) < SB_DONE).astype(jnp.int32)


def _attn_prompt_kernel(q_ref, k_ref, v_ref, o_ref, kb_ref, vb_ref, u_ref, *, tq, heads, dh):
    qi = pl.program_id(2)

    @pl.when(qi == 0)
    def _():
        kb_ref[...] = k_ref[...].astype(BF16)
        vb_ref[...] = v_ref[...].astype(BF16)
        u_ref[...] = _upper_inclusive(tq)

    u = u_ref[...]
    cols = [slice(h * dh, (h + 1) * dh) for h in range(heads)]
    qs = [q_ref[:, cs] for cs in cols]

    def sweep(n, cs, accs, mask):
        rows = pl.ds(pl.multiple_of((qi - n) * tq, tq), tq)
        return _sb_blocks(qs, [kb_ref[rows, cs_] for cs_ in cols], [vb_ref[rows, cs_] for cs_ in cols], u, cs, accs,
                          mask)

    cs, accs = sweep(0, (jnp.zeros((tq, 1), F32),) * heads, (jnp.zeros((tq, dh), F32),) * heads,
                     _strictly_earlier(tq))

    def body(carry):
        n, _, cs, accs = carry
        cs, accs = sweep(n, cs, accs, None)
        return n + 1, _unfinished(cs), cs, accs

    _, _, _, accs = lax.while_loop(lambda carry: (carry[0] <= qi) & (carry[1] > 0), body,
                                   (jnp.int32(1), _unfinished(cs), cs, accs))
    for h in range(heads):
        o_ref[:, cols[h]] = accs[h].astype(o_ref.dtype)


def _attn_prompt(q, k_stack, v_stack, slot, n_seq, T, heads):
    N, D = q.shape
    dh = D // heads
    tq = min(ATTN_BLOCK, T)
    hp = _largest_divisor(heads, (ATTN_PROMPT_HEADS, 1))
    assert T % tq == 0
    n_slots = k_stack.shape[0]
    kv_spec = pl.BlockSpec((None, None, T, hp * dh), lambda b, h, i: (slot, b, 0, h))
    o = pl.pallas_call(
        functools.partial(_attn_prompt_kernel, tq=tq, heads=hp, dh=dh),
        grid=(n_seq, heads // hp, T // tq),
        in_specs=[pl.BlockSpec((None, tq, hp * dh), lambda b, h, i: (b, i, h)), kv_spec, kv_spec],
        out_specs=pl.BlockSpec((None, tq, hp * dh), lambda b, h, i: (b, i, h)),
        out_shape=jax.ShapeDtypeStruct((n_seq, T, D), BF16),
        scratch_shapes=[pltpu.VMEM((T, hp * dh), BF16), pltpu.VMEM((T, hp * dh), BF16),
                        pltpu.VMEM((tq, tq), BF16)],
        compiler_params=_params("parallel", "parallel", "arbitrary"),
        name="attn_prompt",
    )(q.reshape(n_seq, T, D), k_stack.reshape(n_slots, n_seq, T, D), v_stack.reshape(n_slots, n_seq, T, D))
    return o.reshape(N, D)


def _attn_sample_kernel(q_ref, kn_ref, vn_ref, ck_ref, cv_ref, o_ref, *, T, P, tk, heads, all_heads, dh):
    u_past = _upper_inclusive(tk)
    cols = [slice(h * dh, (h + 1) * dh) for h in range(heads)]
    qs = [q_ref[:, cs] for cs in cols]

    cs, accs = _sb_blocks(qs, [kn_ref[:, c].astype(BF16) for c in cols], [vn_ref[:, c].astype(BF16) for c in cols],
                          _upper_inclusive(T), (jnp.zeros((T, 1), F32),) * heads,
                          (jnp.zeros((T, dh), F32),) * heads, _strictly_earlier(T))

    h0 = pl.program_id(1) * heads

    def body(carry):
        n, _, cs, accs = carry
        r0 = (P - (n + 1) * tk) * all_heads + h0
        ks = [ck_ref[pl.ds(r0 + h, tk, stride=all_heads), :].astype(BF16) for h in range(heads)]
        vs = [cv_ref[pl.ds(r0 + h, tk, stride=all_heads), :].astype(BF16) for h in range(heads)]
        cs, accs = _sb_blocks(qs, ks, vs, u_past, cs, accs, None)
        return n + 1, _unfinished(cs), cs, accs

    _, _, _, accs = lax.while_loop(lambda carry: (carry[0] < P // tk) & (carry[1] > 0), body,
                                   (jnp.int32(0), _unfinished(cs), cs, accs))
    for h in range(heads):
        o_ref[:, cols[h]] = accs[h].astype(o_ref.dtype)


def _attn_sample(q, k_stack, v_stack, slot, cache_k, cache_v, n_seq, T, heads):
    N, D = q.shape
    dh = D // heads
    P = cache_k.shape[2]
    tk = _largest_divisor(P, (ATTN_BLOCK, 128))
    hg = _largest_divisor(heads, (ATTN_SAMPLE_HEADS, heads))
    n_slots = k_stack.shape[0]
    q_spec = pl.BlockSpec((None, T, hg * dh), lambda b, g: (b, 0, g))
    new_spec = pl.BlockSpec((None, None, T, hg * dh), lambda b, g: (slot, b, 0, g))
    past_spec = pl.BlockSpec((None, None, P * heads, dh), lambda b, g: (slot, b, 0, 0))
    past_shape = cache_k.shape[:2] + (P * heads, dh)
    o = pl.pallas_call(
        functools.partial(_attn_sample_kernel, T=T, P=P, tk=tk, heads=hg, all_heads=heads, dh=dh),
        grid=(n_seq, heads // hg),
        in_specs=[q_spec, new_spec, new_spec, past_spec, past_spec],
        out_specs=q_spec,
        out_shape=jax.ShapeDtypeStruct((n_seq, T, D), BF16),
        compiler_params=_params("parallel", "arbitrary"),
        name="attn_sample",
    )(q.reshape(n_seq, T, D), k_stack.reshape(n_slots, n_seq, T, D), v_stack.reshape(n_slots, n_seq, T, D),
      cache_k.reshape(past_shape), cache_v.reshape(past_shape))
    return o.reshape(N, D)


def _out_proj_kernel(o_ref, w_ref, x_ref, g1_ref, gn_ref, xo_ref, *, R):
    y = jnp.dot(o_ref[...], w_ref[...], preferred_element_type=F32)
    xo_ref[...] = x_ref[...] + _per_row(g1_ref[...], R) * _rms(y, gn_ref[...])


def _out_proj(o, w, wi, x, n_seq, T, mod, layer, g_norm):
    N, D = x.shape
    S, R = _split_rows(n_seq, T, ROWS_OUT_PROJ)
    rows = S * R
    return pl.pallas_call(
        functools.partial(_out_proj_kernel, R=R),
        grid=(N // rows,),
        in_specs=[pl.BlockSpec((rows, D), lambda i: (i, 0)),
                  pl.BlockSpec((None, D, D), lambda i: (wi, 0, 0)),
                  pl.BlockSpec((rows, D), lambda i: (i, 0)),
                  pl.BlockSpec((None, S, 1, D), lambda i: (layer, _seq_block(i, S, R, T), 0, 2)),
                  pl.BlockSpec((1, D), lambda i: (0, 0))],
        out_specs=pl.BlockSpec((rows, D), lambda i: (i, 0)),
        out_shape=jax.ShapeDtypeStruct((N, D), F32),
        compiler_params=_params("parallel"),
        name="out_proj",
    )(o, w, x, mod, g_norm.reshape(1, D))


def _ffn_kernel(x_ref, xp_ref, gn_in_ref, sc_ref, sh_ref, wa_ref, wg_ref, wc_ref, bc_ref, wd_ref, st_ref, g2_ref,
                gn_out_ref, xo_ref, tail_ref, h_ref, acc_ref, act0_ref, act1_ref, carry_ref, *, S, R, T, nf, nb):
    i, j = pl.program_id(0), pl.program_id(1)
    acts = (act0_ref, act1_ref)
    last_act = acts[(nf - 1) % 2]

    def up_gate(dst_ref):
        h = h_ref[...]
        a = jnp.dot(h, wa_ref[...], preferred_element_type=F32)
        gate = jnp.dot(h, wg_ref[...], preferred_element_type=F32)
        tail = a.reshape(S, R, -1)[:, R - SUBLANES:R, :]
        tail_ref[...] = tail
        p0, p1 = st_ref[:, 0:1, :], st_ref[:, 1:2, :]
        if R < T:
            first = (i % (T // R)) == 0
            prev = carry_ref[j]
            p0 = jnp.where(first, p0, prev[SUBLANES - 2:SUBLANES - 1][None])
            p1 = jnp.where(first, p1, prev[SUBLANES - 1:SUBLANES][None])
            carry_ref[j] = tail[0]
        half = 0.5 * (_causal_conv(a, _per_row(p0, R), _per_row(p1, R), wc_ref[...], R, S) + bc_ref[...])
        dst_ref[...] = ((half + half * jnp.tanh(half)) * gate).astype(BF16)

    def down(src_ref):
        acc_ref[...] += jnp.dot(src_ref[...], wd_ref[...], preferred_element_type=F32)

    def finish_previous():
        down(last_act)
        _residual_norm_rows(xp_ref, acc_ref, g2_ref, gn_out_ref[...], xo_ref, S, R, straight_line=True)

    def start_block():
        _norm_mod_rows(x_ref, gn_in_ref[...], sc_ref, sh_ref, h_ref, S, R, straight_line=True)
        acc_ref[...] = jnp.zeros_like(acc_ref)
        up_gate(acts[0])

    @pl.when((j == 0) & (i == 0))
    def _():
        if R < T:
            carry_ref[...] = jnp.zeros_like(carry_ref)
        start_block()

    @pl.when((j == 0) & (i > 0) & (i < nb))
    def _():
        finish_previous()
        start_block()

    @pl.when((j == 0) & (i == nb))
    def _():
        finish_previous()

    for parity in range(2):
        @pl.when((j > 0) & (i < nb) & (j % 2 == parity))
        def _(parity=parity):
            up_gate(acts[parity])
            down(acts[1 - parity])


def _ffn(x, n_seq, T, state, mod, layer, g_in, w_up, w_conv, b_conv, w_down, g_out):
    N, D = x.shape
    L, F = w_down.shape[0], w_down.shape[1]
    S, R = _split_rows(n_seq, T, ROWS_FFN)
    rows = S * R
    tf = _largest_divisor(F, (COLS_FFN, 256, 128))
    nf = F // tf

    nb = N // rows

    def cur(i):
        return jnp.minimum(i, nb - 1)

    def prev(i):
        return jnp.maximum(i - 1, 0)

    def up_chunk(i, j):
        return jnp.where(i < nb, j, nf - 1)

    def down_chunk(i, j):
        return jnp.where((i == nb) & (j > 0), nf - 1, (j + nf - 1) % nf)

    def mod_spec(chunk, blk):
        return pl.BlockSpec((None, S, 1, D), lambda i, j: (layer, _seq_block(blk(i), S, R, T), 0, chunk))

    return pl.pallas_call(
        functools.partial(_ffn_kernel, S=S, R=R, T=T, nf=nf, nb=nb),
        grid=(nb + 1, nf),
        in_specs=[pl.BlockSpec((rows, D), lambda i, j: (cur(i), 0)),
                  pl.BlockSpec((rows, D), lambda i, j: (prev(i), 0)),
                  pl.BlockSpec((1, D), lambda i, j: (0, 0)),
                  mod_spec(4, cur), mod_spec(3, cur),
                  pl.BlockSpec((None, D, tf), lambda i, j: (layer, 0, up_chunk(i, j))),
                  pl.BlockSpec((None, D, tf), lambda i, j: (layer, 0, nf + up_chunk(i, j))),
                  pl.BlockSpec((None, CONV_TAPS, tf), lambda i, j: (layer, 0, up_chunk(i, j))),
                  pl.BlockSpec((None, 1, tf), lambda i, j: (layer, 0, up_chunk(i, j))),
                  pl.BlockSpec((None, tf, D), lambda i, j: (layer, down_chunk(i, j), 0)),
                  pl.BlockSpec((None, S, CONV_TAPS - 1, tf),
                               lambda i, j: (layer, _seq_block(cur(i), S, R, T), 0, up_chunk(i, j))),
                  mod_spec(5, prev),
                  pl.BlockSpec((1, D), lambda i, j: (0, 0))],
        out_specs=[pl.BlockSpec((rows, D), lambda i, j: (prev(i), 0)),
                   pl.BlockSpec((S, SUBLANES, tf), lambda i, j: (cur(i), 0, up_chunk(i, j)))],
        out_shape=[jax.ShapeDtypeStruct((N, D), F32),
                   jax.ShapeDtypeStruct((N // rows * S, SUBLANES, F), F32)],
        scratch_shapes=[pltpu.VMEM((rows, D), BF16), pltpu.VMEM((rows, D), F32),
                        pltpu.VMEM((rows, tf), BF16), pltpu.VMEM((rows, tf), BF16),
                        pltpu.VMEM((nf, SUBLANES, tf), F32)],
        compiler_params=_params("arbitrary", "arbitrary"),
        name="ffn",
    )(x, x, g_in.reshape(1, D), mod, mod, w_up, w_up, w_conv, b_conv.reshape(L, 1, F), w_down, state, mod,
      g_out.reshape(1, D))


def _last_rows(tails, n_seq):
    t = tails.reshape(n_seq, tails.shape[0] // n_seq, SUBLANES, tails.shape[2])
    return t[:, -1, SUBLANES - (CONV_TAPS - 1):, :]


def _trunk(x, mod, cache_k, cache_v, conv_a_prev, ffn_prev, norm_g, w_in_ab, w_conv_a, g_sgu, w_sgu, b_sgu,
           w_out_ab, w_qkv_sb, w_o_sb, w_ffn_up, w_ffn_conv, b_ffn_conv, w_ffn_down, heads, emit_vn):
    n_seq, T, D = x.shape
    x = x.reshape(n_seq * T, D)
    depth = norm_g.shape[0]
    n_odd = w_qkv_sb.shape[0]
    q_scale = (D // heads) ** -0.5 * LOG2E
    kv = None
    conv_a_new, ffn_new, sgu_v_new = [], [], []
    for l in range(depth):
        i = l // 2
        if l % 2 == 0:
            (z,) = _in_proj(x, n_seq, T, norm_g[l, 0], mod, l, 1, 0, w_in_ab, i, (F32,), (1.0,))
            res = _mix_even(z, x, n_seq, T, conv_a_prev, mod, l, i, w_conv_a, g_sgu, w_sgu, b_sgu, w_out_ab,
                            norm_g[l, 1], emit_vn)
            x = res[0]
            conv_a_new.append(_last_rows(res[1], n_seq))
            if emit_vn:
                sgu_v_new.append(res[2].reshape(n_seq, T, -1))
        else:
            q, *kv = _in_proj(x, n_seq, T, norm_g[l, 0], mod, l, 1, 0, w_qkv_sb, i, (BF16, F32, F32),
                              (q_scale, 1.0, 1.0), stacks=kv, slot=i, n_slots=n_odd)
            if cache_k is None:
                o = _attn_prompt(q, kv[0], kv[1], i, n_seq, T, heads)
            else:
                o = _attn_sample(q, kv[0], kv[1], i, cache_k, cache_v, n_seq, T, heads)
            x = _out_proj(o, w_o_sb, i, x, n_seq, T, mod, l, norm_g[l, 1])
        x, tails = _ffn(x, n_seq, T, ffn_prev, mod, l, norm_g[l, 2], w_ffn_up, w_ffn_conv, b_ffn_conv, w_ffn_down,
                        norm_g[l, 3])
        ffn_new.append(_last_rows(tails, n_seq))
    k_new, v_new = (a.reshape(n_odd, n_seq, T, heads, D // heads) for a in kv)
    return x.reshape(n_seq, T, D), k_new, v_new, jnp.stack(conv_a_new), jnp.stack(ffn_new), sgu_v_new


def kernel(x_prompt, x_sample, c_prompt, c_sample, cache_sb_k, cache_sb_v, state_conv_a, state_ffn_conv, w_mod, b_mod,
           norm_g, w_in_ab, w_conv_a, g_sgu, w_sgu, b_sgu, w_out_ab, w_qkv_sb, w_o_sb, w_ffn_up, w_ffn_conv,
           b_ffn_conv, w_ffn_down):
    bp, bs = x_prompt.shape[0], x_sample.shape[0]
    depth = norm_g.shape[0]
    heads = cache_sb_k.shape[3]

    mod = _modulation(jnp.concatenate([c_prompt, c_sample], axis=0), w_mod, b_mod)
    mod_p = mod[:, :bp].reshape(depth, bp, 1, -1)
    mod_s = mod[:, bp:].reshape(depth, bs, 1, -1)

    weights = (norm_g, w_in_ab.astype(BF16), w_conv_a, g_sgu, w_sgu, b_sgu, w_out_ab.astype(BF16),
               w_qkv_sb.astype(BF16), w_o_sb.astype(BF16), w_ffn_up.astype(BF16), w_ffn_conv, b_ffn_conv,
               w_ffn_down.astype(BF16))

    zeros_a = jnp.zeros((state_conv_a.shape[0], bp) + state_conv_a.shape[2:], F32)
    zeros_f = jnp.zeros((depth, bp) + state_ffn_conv.shape[2:], F32)
    y_s, k_s, v_s, conv_a_s, ffn_s, sgu_v = _trunk(x_sample, mod_s, cache_sb_k, cache_sb_v, state_conv_a,
                                                   state_ffn_conv, *weights, heads=heads, emit_vn=True)
    sample_out, (x_prompt, mod_p) = lax.optimization_barrier(
        ((y_s, k_s, v_s, conv_a_s, ffn_s, sgu_v), (x_prompt, mod_p)))
    y_s, k_s, v_s, conv_a_s, ffn_s, sgu_v = sample_out
    y_p, k_p, v_p, conv_a_p, ffn_p, _ = _trunk(x_prompt, mod_p, None, None, zeros_a, zeros_f, *weights,
                                               heads=heads, emit_vn=False)
    return (y_p, y_s, k_p, v_p, conv_a_p, ffn_p, k_s, v_s, conv_a_s, ffn_s, jnp.stack(sgu_v))
```

```python
import functools
import math

import jax
import jax.numpy as jnp
from jax import lax
from jax.experimental import pallas as pl
from jax.experimental.pallas import tpu as pltpu

F32 = jnp.float32
BF16 = jnp.bfloat16

EPS = 1e-6
CONV_TAPS = 3
GMLP_CAUSAL_BLOCK = 64
SUBLANES = 8
VMEM_LIMIT_BYTES = 56 * 1024 * 1024

ROWS_IN_PROJ = 1024
ROWS_MIX = 256
ROWS_OUT_PROJ = 512
ROWS_FFN = 512
COLS_IN_PROJ = 512
COLS_FFN = 512
COLS_MOD = 1024
ATTN_BLOCK = 256
ATTN_PROMPT_HEADS = 4
ATTN_SAMPLE_HEADS = 8

LOG2E = math.log2(math.e)
SB_Z_MAX = 126.0
SB_DONE = 150.0 / LOG2E


def _params(*semantics):
    return pltpu.CompilerParams(dimension_semantics=semantics, vmem_limit_bytes=VMEM_LIMIT_BYTES)


def _split_rows(n_seq, seq_len, rows):
    if seq_len >= rows:
        assert seq_len % rows == 0
        return 1, rows
    s = min(rows // seq_len, n_seq)
    assert n_seq % s == 0 and seq_len % SUBLANES == 0
    return s, seq_len


def _seq_block(i, S, R, T):
    return i // (T // R) if S == 1 else i


def _largest_divisor(n, candidates):
    for c in candidates:
        if n % c == 0:
            return c
    raise ValueError(f"no block size among {candidates} divides {n}")


def _rms(x, g):
    return x * lax.rsqrt(jnp.mean(x * x, axis=-1, keepdims=True) + EPS) * g


def _per_row(v, R):
    S, _, C = v.shape
    if S == 1:
        return v[0]
    return jnp.broadcast_to(v, (S, R, C)).reshape(S * R, C)


def _sigmoid(x):
    return 1.0 / (1.0 + jnp.exp(-x))


def _conv_taps(a, p0, p1, w, R):
    rt = lax.broadcasted_iota(jnp.int32, a.shape, 0) & (R - 1)
    s1 = jnp.where(rt == 0, p1, pltpu.roll(a, 1, 0))
    s2 = jnp.where(rt == 0, p0, jnp.where(rt == 1, p1, pltpu.roll(a, 2, 0)))
    return s2 * w[0:1] + s1 * w[1:2] + a * w[2:3]


def _causal_conv(a, p0, p1, w, R, S):
    if S > 1:
        return _conv_taps(a, p0, p1, w, R)
    body = pltpu.roll(a, 2, 0) * w[0:1] + pltpu.roll(a, 1, 0) * w[1:2] + a * w[2:3]
    head = _conv_taps(a[0:SUBLANES], p0, p1, w, R)
    return jnp.concatenate([head, body[SUBLANES:]], axis=0)


NORM_ROWS = 128


def _row_chunks(ref_sc, c, S, R, step):
    if S == 1:
        return ref_sc[0]
    k = step // R
    return _per_row(ref_sc[pl.ds(c * k, k)], R)


def _norm_mod_rows(x_ref, g, sc_ref, sh_ref, h_ref, S, R, straight_line=False):
    rows = S * R
    step = min(NORM_ROWS, rows)
    assert rows % step == 0 and (S == 1 or step % R == 0)

    def body(c, carry):
        rs = pl.ds(c * step, step) if straight_line else pl.ds(pl.multiple_of(c * step, step), step)
        h = _rms(x_ref[rs, :], g) * (1.0 + _row_chunks(sc_ref, c, S, R, step)) + _row_chunks(sh_ref, c, S, R, step)
        h_ref[rs, :] = h.astype(BF16)
        return carry

    _for_chunks(rows // step, body, straight_line)


def _residual_norm_rows(x_ref, y_ref, gate_ref, g, xo_ref, S, R, straight_line=False):
    rows = S * R
    step = min(NORM_ROWS, rows)
    assert rows % step == 0 and (S == 1 or step % R == 0)

    def body(c, carry):
        rs = pl.ds(c * step, step) if straight_line else pl.ds(pl.multiple_of(c * step, step), step)
        xo_ref[rs, :] = x_ref[rs, :] + _row_chunks(gate_ref, c, S, R, step) * _rms(y_ref[rs, :], g)
        return carry

    _for_chunks(rows // step, body, straight_line)


def _for_chunks(n, body, straight_line):
    if straight_line:
        for c in range(n):
            body(c, 0)
    else:
        lax.fori_loop(0, n, body, 0)


def _mod_kernel(c_ref, w_ref, b_ref, o_ref):
    c = c_ref[...]
    s = (c * _sigmoid(c)).astype(BF16)
    o_ref[...] = jnp.dot(s, w_ref[...].astype(BF16), preferred_element_type=F32) + b_ref[...]


def _modulation(c, w_mod, b_mod):
    L, D, N = w_mod.shape
    B = c.shape[0]
    tn = _largest_divisor(N, (COLS_MOD, 512, 256, 128))
    return pl.pallas_call(
        _mod_kernel,
        grid=(L, N // tn),
        in_specs=[pl.BlockSpec((B, D), lambda l, j: (0, 0)),
                  pl.BlockSpec((None, D, tn), lambda l, j: (l, 0, j)),
                  pl.BlockSpec((None, 1, tn), lambda l, j: (l, 0, j))],
        out_specs=pl.BlockSpec((None, B, tn), lambda l, j: (l, 0, j)),
        out_shape=jax.ShapeDtypeStruct((L, B, N), F32),
        compiler_params=_params("parallel", "parallel"),
        name="modulation",
    )(c, w_mod, b_mod.reshape(L, 1, N))


def _in_proj_kernel(x_ref, g_ref, sc_ref, sh_ref, w_ref, *rest, S, R, nj, blocks_per_out, n_out, out_scales,
                    fill_slots):
    outs, hs = rest[-2 - n_out:-2], rest[-2:]
    i, j = pl.program_id(0), pl.program_id(1)

    def store(k, y):
        val = (y if out_scales[k] == 1.0 else y * out_scales[k]).astype(outs[k].dtype)
        if k > 0 and fill_slots:
            for s in range(outs[k].shape[0]):
                outs[k][s] = val
        else:
            outs[k][...] = val

    @pl.when((i == 0) & (j == 0))
    def _():
        _norm_mod_rows(x_ref, g_ref[...], sc_ref, sh_ref, hs[0], S, R)

    for parity in range(2):
        @pl.when((i % 2 == parity) & (j < nj - 1))
        def _(parity=parity):
            y = jnp.dot(hs[parity][...], w_ref[...], preferred_element_type=F32)
            if n_out == 1:
                store(0, y)
            else:
                for k in range(n_out):
                    @pl.when(j // blocks_per_out == k)
                    def _(k=k):
                        store(k, y)

        @pl.when((i % 2 == parity) & (j == nj - 1))
        def _(parity=parity):
            _norm_mod_rows(x_ref, g_ref[...], sc_ref, sh_ref, hs[1 - parity], S, R, straight_line=True)
            store(n_out - 1, jnp.dot(hs[parity][...], w_ref[...], preferred_element_type=F32))


def _in_proj(x, n_seq, T, g, mod, layer, sc_chunk, sh_chunk, w, wi, out_dtypes, out_scales, stacks=None, slot=0,
             n_slots=1):
    N, D = x.shape
    n_total = w.shape[2]
    n_out = len(out_dtypes)
    S, R = _split_rows(n_seq, T, ROWS_IN_PROJ)
    rows = S * R
    nb = N // rows
    width = n_total // n_out
    tn = _largest_divisor(width, (COLS_IN_PROJ, 256, 128))
    bpo = width // tn
    nj = n_total // tn

    def norm_block(i, j):
        return jnp.where(j == nj - 1, jnp.minimum(i + 1, nb - 1), i)

    def mod_spec(chunk):
        return pl.BlockSpec((None, S, 1, D), lambda i, j: (layer, _seq_block(norm_block(i, j), S, R, T), 0, chunk))

    def col(j, k):
        return jnp.clip(j - k * bpo, 0, bpo - 1)

    out_specs = [pl.BlockSpec((rows, tn), lambda i, j: (i, col(j, 0)))]
    out_shape = [jax.ShapeDtypeStruct((N, width), out_dtypes[0])]
    for k in range(1, n_out):
        if stacks is None:
            spec = pl.BlockSpec((n_slots, rows, tn), functools.partial(lambda i, j, k: (0, i, col(j, k)), k=k))
        else:
            spec = pl.BlockSpec((None, rows, tn), functools.partial(lambda i, j, k: (slot, i, col(j, k)), k=k))
        out_specs.append(spec)
        out_shape.append(jax.ShapeDtypeStruct((n_slots, N, width), out_dtypes[k]))
    in_specs = [pl.BlockSpec((rows, D), lambda i, j: (norm_block(i, j), 0)),
                pl.BlockSpec((1, D), lambda i, j: (0, 0)),
                mod_spec(sc_chunk), mod_spec(sh_chunk),
                pl.BlockSpec((None, D, tn), lambda i, j: (wi, 0, j))]
    args = [x, g.reshape(1, D), mod, mod, w]
    aliases = {}
    if stacks is not None:
        for k, st in enumerate(stacks):
            aliases[len(args)] = k + 1
            in_specs.append(pl.BlockSpec(memory_space=pl.ANY))
            args.append(st)
    return pl.pallas_call(
        functools.partial(_in_proj_kernel, S=S, R=R, nj=nj, blocks_per_out=bpo, n_out=n_out,
                          out_scales=tuple(out_scales), fill_slots=stacks is None),
        grid=(nb, nj),
        in_specs=in_specs, out_specs=out_specs, out_shape=out_shape,
        scratch_shapes=[pltpu.VMEM((rows, D), BF16), pltpu.VMEM((rows, D), BF16)],
        input_output_aliases=aliases,
        compiler_params=_params("arbitrary", "arbitrary"),
        name="in_proj",
    )(*args)


def _sgu_matrices(ws_ref, bs_ref, g, T, chunk, whole_chunks):
    w = ws_ref[g]
    b = jnp.broadcast_to(bs_ref[g], (chunk, chunk))
    ii = lax.broadcasted_iota(jnp.int32, (chunk, chunk), 0)
    jj = lax.broadcasted_iota(jnp.int32, (chunk, chunk), 1)
    if whole_chunks:
        return jnp.where(jj // GMLP_CAUSAL_BLOCK <= ii // GMLP_CAUSAL_BLOCK, w, 0.0), b
    w_tl = jnp.where((ii < T) & (jj < T), w, 0.0)
    b_tl = jnp.where(ii < T, b, 0.0)
    wm, bm = w_tl, b_tl
    for k in range(1, chunk // T):
        wm = wm + pltpu.roll(pltpu.roll(w_tl, k * T, 0), k * T, 1)
        bm = bm + pltpu.roll(b_tl, k * T, 0)
    return wm, bm


def _mix_even_kernel(*refs, S, R, T, chunk, groups, halo, emit_vn):
    refs = list(refs)
    xa_ref, gb_ref, gc_ref, u_ref, v_ref = refs[:5]
    del refs[:5]
    if halo:
        hxa_ref, hgc_ref = refs[:2]
        del refs[:2]
    st_ref, x_ref, g1_ref, wc_ref, gs_ref, ws_ref, bs_ref, wo_ref, gn_ref = refs[:9]
    del refs[:9]
    xo_ref, tail_ref = refs[:2]
    del refs[:2]
    if emit_vn:
        vn_out_ref = refs.pop(0)
    cat_ref, vn_ref = refs

    i = pl.program_id(0)
    rows = S * R
    C = xa_ref.shape[1]

    ca = gc_ref[...] * xa_ref[...]
    st = st_ref[...]
    p0, p1 = st[:, 0:1, :], st[:, 1:2, :]
    if halo:
        hca = hgc_ref[...] * hxa_ref[...]
        first = (i % (T // R)) == 0
        p0 = jnp.where(first, p0, hca[SUBLANES - 2:SUBLANES - 1][None])
        p1 = jnp.where(first, p1, hca[SUBLANES - 1:SUBLANES][None])
    conv = _causal_conv(ca, _per_row(p0, R), _per_row(p1, R), wc_ref[...], R, S)
    cat_ref[:, 0:C] = (gb_ref[...] * conv).astype(BF16)
    tail_ref[...] = ca.reshape(S, R, C)[:, R - SUBLANES:R, :]

    vn = _rms(v_ref[...], gs_ref[...])
    if emit_vn:
        vn_out_ref[...] = vn
    vn_ref[...] = vn.astype(BF16)
    gd = v_ref.shape[1] // groups
    for g in range(groups):
        wm, bm = _sgu_matrices(ws_ref, bs_ref, g, T, chunk, whole_chunks=(R % chunk == 0))
        wm = wm.astype(BF16)
        cs = slice(g * gd, (g + 1) * gd)
        for c in range(rows // chunk):
            rs = slice(c * chunk, (c + 1) * chunk)
            mixed = jnp.dot(wm, vn_ref[rs, cs], preferred_element_type=F32) + bm
            cat_ref[rs, C + g * gd:C + (g + 1) * gd] = (u_ref[rs, cs] * mixed).astype(BF16)

    y = jnp.dot(cat_ref[...], wo_ref[...], preferred_element_type=F32)
    xo_ref[...] = x_ref[...] + _per_row(g1_ref[...], R) * _rms(y, gn_ref[...])


def _mix_even(z, x, n_seq, T, state, mod, layer, wi, w_conv, g_sgu, w_sgu, b_sgu, w_out, g_norm, emit_vn):
    N, D = x.shape
    C = w_conv.shape[2]
    groups, chunk = w_sgu.shape[1], w_sgu.shape[2]
    Cb = g_sgu.shape[1]
    S, R = _split_rows(n_seq, T, ROWS_MIX)
    rows = S * R
    halo = R < T
    assert rows % chunk == 0 and (R % chunk == 0 or (chunk % T == 0 and T <= GMLP_CAUSAL_BLOCK))
    assert z.shape[1] == 3 * C + 2 * Cb and C == Cb

    in_specs = [pl.BlockSpec((rows, C), functools.partial(lambda i, k: (i, k), k=k)) for k in range(5)]
    args = [z] * 5
    if halo:
        per = R // SUBLANES
        in_specs += [pl.BlockSpec((SUBLANES, C), functools.partial(
            lambda i, k: (jnp.maximum(i * per - 1, 0), k), k=k)) for k in (0, 2)]
        args += [z, z]
    in_specs += [
        pl.BlockSpec((None, S, CONV_TAPS - 1, C), lambda i: (wi, _seq_block(i, S, R, T), 0, 0)),
        pl.BlockSpec((rows, D), lambda i: (i, 0)),
        pl.BlockSpec((None, S, 1, D), lambda i: (layer, _seq_block(i, S, R, T), 0, 2)),
        pl.BlockSpec((None, CONV_TAPS, C), lambda i: (wi, 0, 0)),
        pl.BlockSpec((None, 1, Cb), lambda i: (wi, 0, 0)),
        pl.BlockSpec((None, groups, chunk, chunk), lambda i: (wi, 0, 0, 0)),
        pl.BlockSpec((None, groups, chunk, 1), lambda i: (wi, 0, 0, 0)),
        pl.BlockSpec((None, C + Cb, D), lambda i: (wi, 0, 0)),
        pl.BlockSpec((1, D), lambda i: (0, 0)),
    ]
    n_even = w_conv.shape[0]
    args += [state, x, mod, w_conv, g_sgu.reshape(n_even, 1, Cb), w_sgu, b_sgu.reshape(n_even, groups, chunk, 1),
             w_out, g_norm.reshape(1, D)]
    out_specs = [pl.BlockSpec((rows, D), lambda i: (i, 0)),
                 pl.BlockSpec((S, SUBLANES, C), lambda i: (i, 0, 0))]
    out_shape = [jax.ShapeDtypeStruct((N, D), F32),
                 jax.ShapeDtypeStruct((N // rows * S, SUBLANES, C), F32)]
    if emit_vn:
        out_specs.append(pl.BlockSpec((rows, Cb), lambda i: (i, 0)))
        out_shape.append(jax.ShapeDtypeStruct((N, Cb), F32))
    return pl.pallas_call(
        functools.partial(_mix_even_kernel, S=S, R=R, T=T, chunk=chunk, groups=groups, halo=halo, emit_vn=emit_vn),
        grid=(N // rows,),
        in_specs=in_specs, out_specs=out_specs, out_shape=out_shape,
        scratch_shapes=[pltpu.VMEM((rows, C + Cb), BF16), pltpu.VMEM((rows, Cb), BF16)],
        compiler_params=_params("parallel"),
        name="mix_even",
    )(*args)


def _upper_inclusive(n):
    r = lax.broadcasted_iota(jnp.int32, (n, n), 0)
    c = lax.broadcasted_iota(jnp.int32, (n, n), 1)
    return jnp.where(r >= c, 1.0, 0.0).astype(BF16)


def _strictly_earlier(n):
    t = lax.broadcasted_iota(jnp.int32, (n, n), 0)
    s = lax.broadcasted_iota(jnp.int32, (n, n), 1)
    return s < t


def _sb_blocks(qs, ks, vs, u_inc, cs, accs, mask):
    tq = qs[0].shape[0]
    zs, parts = [], []
    for q, k in zip(qs, ks):
        z = jnp.minimum(lax.dot_general(q, k, (((1,), (1,)), ((), ())), preferred_element_type=F32), SB_Z_MAX)
        fail = jnp.log(1.0 + jnp.exp2(z))
        if mask is not None:
            fail = jnp.where(mask, fail, 0.0)
        hi = fail.astype(BF16)
        parts += [hi, (fail - hi.astype(F32)).astype(BF16)]
        zs.append(z)
    sums = jnp.dot(jnp.concatenate(parts, axis=0), u_inc, preferred_element_type=F32)
    cs_out, accs_out = [], []
    for h, (z, v, c, acc) in enumerate(zip(zs, vs, cs, accs)):
        incl = sums[2 * h * tq:(2 * h + 1) * tq] + sums[(2 * h + 1) * tq:(2 * h + 2) * tq]
        w = jnp.exp2(z - (incl + c) * LOG2E)
        if mask is not None:
            w = jnp.where(mask, w, 0.0)
        accs_out.append(acc + jnp.dot(w.astype(BF16), v, preferred_element_type=F32))
        cs_out.append(c + incl[:, 0:1])
    return tuple(cs_out), tuple(accs_out)


def _unfinished(cs):
    return (jnp.min(functools.reduce(jnp.minimum, cs)) < SB_DONE).astype(jnp.int32)


def _attn_prompt_kernel(q_ref, k_ref, v_ref, o_ref, kb_ref, vb_ref, u_ref, *, tq, heads, dh):
    qi = pl.program_id(2)

    @pl.when(qi == 0)
    def _():
        kb_ref[...] = k_ref[...].astype(BF16)
        vb_ref[...] = v_ref[...].astype(BF16)
        u_ref[...] = _upper_inclusive(tq)

    u = u_ref[...]
    cols = [slice(h * dh, (h + 1) * dh) for h in range(heads)]
    qs = [q_ref[:, cs] for cs in cols]

    def sweep(n, cs, accs, mask):
        rows = pl.ds(pl.multiple_of((qi - n) * tq, tq), tq)
        return _sb_blocks(qs, [kb_ref[rows, cs_] for cs_ in cols], [vb_ref[rows, cs_] for cs_ in cols], u, cs, accs,
                          mask)

    cs, accs = sweep(0, (jnp.zeros((tq, 1), F32),) * heads, (jnp.zeros((tq, dh), F32),) * heads,
                     _strictly_earlier(tq))

    def body(carry):
        n, _, cs, accs = carry
        cs, accs = sweep(n, cs, accs, None)
        return n + 1, _unfinished(cs), cs, accs

    _, _, _, accs = lax.while_loop(lambda carry: (carry[0] <= qi) & (carry[1] > 0), body,
                                   (jnp.int32(1), _unfinished(cs), cs, accs))
    for h in range(heads):
        o_ref[:, cols[h]] = accs[h].astype(o_ref.dtype)


def _attn_prompt(q, k_stack, v_stack, slot, n_seq, T, heads):
    N, D = q.shape
    dh = D // heads
    tq = min(ATTN_BLOCK, T)
    hp = _largest_divisor(heads, (ATTN_PROMPT_HEADS, 1))
    assert T % tq == 0
    n_slots = k_stack.shape[0]
    kv_spec = pl.BlockSpec((None, None, T, hp * dh), lambda b, h, i: (slot, b, 0, h))
    o = pl.pallas_call(
        functools.partial(_attn_prompt_kernel, tq=tq, heads=hp, dh=dh),
        grid=(n_seq, heads // hp, T // tq),
        in_specs=[pl.BlockSpec((None, tq, hp * dh), lambda b, h, i: (b, i, h)), kv_spec, kv_spec],
        out_specs=pl.BlockSpec((None, tq, hp * dh), lambda b, h, i: (b, i, h)),
        out_shape=jax.ShapeDtypeStruct((n_seq, T, D), BF16),
        scratch_shapes=[pltpu.VMEM((T, hp * dh), BF16), pltpu.VMEM((T, hp * dh), BF16),
                        pltpu.VMEM((tq, tq), BF16)],
        compiler_params=_params("parallel", "parallel", "arbitrary"),
        name="attn_prompt",
    )(q.reshape(n_seq, T, D), k_stack.reshape(n_slots, n_seq, T, D), v_stack.reshape(n_slots, n_seq, T, D))
    return o.reshape(N, D)


def _attn_sample_kernel(q_ref, kn_ref, vn_ref, ck_last_ref, cv_last_ref, ck_hbm, cv_hbm, o_ref, kbuf, vbuf, sem,
                        *, T, P, tk, heads, all_heads, dh, slot):
    u_past = _upper_inclusive(tk)
    cols = [slice(h * dh, (h + 1) * dh) for h in range(heads)]
    qs = [q_ref[:, cs] for cs in cols]

    cs, accs = _sb_blocks(qs, [kn_ref[:, c].astype(BF16) for c in cols], [vn_ref[:, c].astype(BF16) for c in cols],
                          _upper_inclusive(T), (jnp.zeros((T, 1), F32),) * heads,
                          (jnp.zeros((T, dh), F32),) * heads, _strictly_earlier(T))

    b, h0 = pl.program_id(0), pl.program_id(1) * heads

    def head_rows(ref):
        return [ref[pl.ds(h0 + h, tk, stride=all_heads), :].astype(BF16) for h in range(heads)]

    cs, accs = _sb_blocks(qs, head_rows(ck_last_ref), head_rows(cv_last_ref), u_past, cs, accs, None)

    def body(carry):
        n, _, cs, accs = carry
        rows = pl.ds(pl.multiple_of((P - (n + 1) * tk) * all_heads, tk * all_heads), tk * all_heads)
        copies = [pltpu.make_async_copy(src.at[slot, b, rows, :], dst, sem.at[s])
                  for s, (src, dst) in enumerate(((ck_hbm, kbuf), (cv_hbm, vbuf)))]
        for c in copies:
            c.start()
        for c in copies:
            c.wait()
        cs, accs = _sb_blocks(qs, head_rows(kbuf), head_rows(vbuf), u_past, cs, accs, None)
        return n + 1, _unfinished(cs), cs, accs

    _, _, _, accs = lax.while_loop(lambda carry: (carry[0] < P // tk) & (carry[1] > 0), body,
                                   (jnp.int32(1), _unfinished(cs), cs, accs))
    for h in range(heads):
        o_ref[:, cols[h]] = accs[h].astype(o_ref.dtype)


def _attn_sample(q, k_stack, v_stack, slot, cache_k, cache_v, n_seq, T, heads):
    N, D = q.shape
    dh = D // heads
    P = cache_k.shape[2]
    tk = _largest_divisor(P, (ATTN_BLOCK, 128))
    hg = _largest_divisor(heads, (ATTN_SAMPLE_HEADS, heads))
    n_slots = k_stack.shape[0]
    q_spec = pl.BlockSpec((None, T, hg * dh), lambda b, g: (b, 0, g))
    new_spec = pl.BlockSpec((None, None, T, hg * dh), lambda b, g: (slot, b, 0, g))
    last_spec = pl.BlockSpec((None, None, tk * heads, dh), lambda b, g: (slot, b, P // tk - 1, 0))
    hbm_spec = pl.BlockSpec(memory_space=pl.ANY)
    past_shape = cache_k.shape[:2] + (P * heads, dh)
    cache_k, cache_v = cache_k.reshape(past_shape), cache_v.reshape(past_shape)
    o = pl.pallas_call(
        functools.partial(_attn_sample_kernel, T=T, P=P, tk=tk, heads=hg, all_heads=heads, dh=dh, slot=slot),
        grid=(n_seq, heads // hg),
        in_specs=[q_spec, new_spec, new_spec, last_spec, last_spec, hbm_spec, hbm_spec],
        out_specs=q_spec,
        out_shape=jax.ShapeDtypeStruct((n_seq, T, D), BF16),
        scratch_shapes=[pltpu.VMEM((tk * heads, dh), F32), pltpu.VMEM((tk * heads, dh), F32),
                        pltpu.SemaphoreType.DMA((2,))],
        compiler_params=_params("parallel", "arbitrary"),
        name="attn_sample",
    )(q.reshape(n_seq, T, D), k_stack.reshape(n_slots, n_seq, T, D), v_stack.reshape(n_slots, n_seq, T, D),
      cache_k, cache_v, cache_k, cache_v)
    return o.reshape(N, D)


def _out_proj_kernel(o_ref, w_ref, x_ref, g1_ref, gn_ref, xo_ref, *, R):
    y = jnp.dot(o_ref[...], w_ref[...], preferred_element_type=F32)
    xo_ref[...] = x_ref[...] + _per_row(g1_ref[...], R) * _rms(y, gn_ref[...])


def _out_proj(o, w, wi, x, n_seq, T, mod, layer, g_norm):
    N, D = x.shape
    S, R = _split_rows(n_seq, T, ROWS_OUT_PROJ)
    rows = S * R
    return pl.pallas_call(
        functools.partial(_out_proj_kernel, R=R),
        grid=(N // rows,),
        in_specs=[pl.BlockSpec((rows, D), lambda i: (i, 0)),
                  pl.BlockSpec((None, D, D), lambda i: (wi, 0, 0)),
                  pl.BlockSpec((rows, D), lambda i: (i, 0)),
                  pl.BlockSpec((None, S, 1, D), lambda i: (layer, _seq_block(i, S, R, T), 0, 2)),
                  pl.BlockSpec((1, D), lambda i: (0, 0))],
        out_specs=pl.BlockSpec((rows, D), lambda i: (i, 0)),
        out_shape=jax.ShapeDtypeStruct((N, D), F32),
        compiler_params=_params("parallel"),
        name="out_proj",
    )(o, w, x, mod, g_norm.reshape(1, D))


def _ffn_kernel(x_ref, xp_ref, gn_in_ref, sc_ref, sh_ref, wa_ref, wg_ref, wc_ref, bc_ref, wd_ref, st_ref, g2_ref,
                gn_out_ref, xo_ref, tail_ref, h_ref, acc_ref, act0_ref, act1_ref, carry_ref, *, S, R, T, nf, nb):
    i, j = pl.program_id(0), pl.program_id(1)
    acts = (act0_ref, act1_ref)
    last_act = acts[(nf - 1) % 2]

    def up_gate(dst_ref):
        h = h_ref[...]
        a = jnp.dot(h, wa_ref[...], preferred_element_type=F32)
        gate = jnp.dot(h, wg_ref[...], preferred_element_type=F32)
        tail = a.reshape(S, R, -1)[:, R - SUBLANES:R, :]
        tail_ref[...] = tail
        p0, p1 = st_ref[:, 0:1, :], st_ref[:, 1:2, :]
        if R < T:
            first = (i % (T // R)) == 0
            prev = carry_ref[j]
            p0 = jnp.where(first, p0, prev[SUBLANES - 2:SUBLANES - 1][None])
            p1 = jnp.where(first, p1, prev[SUBLANES - 1:SUBLANES][None])
            carry_ref[j] = tail[0]
        half = 0.5 * (_causal_conv(a, _per_row(p0, R), _per_row(p1, R), wc_ref[...], R, S) + bc_ref[...])
        dst_ref[...] = ((half + half * jnp.tanh(half)) * gate).astype(BF16)

    def down(src_ref):
        acc_ref[...] += jnp.dot(src_ref[...], wd_ref[...], preferred_element_type=F32)

    def finish_previous():
        down(last_act)
        _residual_norm_rows(xp_ref, acc_ref, g2_ref, gn_out_ref[...], xo_ref, S, R, straight_line=True)

    def start_block():
        _norm_mod_rows(x_ref, gn_in_ref[...], sc_ref, sh_ref, h_ref, S, R, straight_line=True)
        acc_ref[...] = jnp.zeros_like(acc_ref)
        up_gate(acts[0])

    @pl.when((j == 0) & (i == 0))
    def _():
        if R < T:
            carry_ref[...] = jnp.zeros_like(carry_ref)
        start_block()

    @pl.when((j == 0) & (i > 0) & (i < nb))
    def _():
        finish_previous()
        start_block()

    @pl.when((j == 0) & (i == nb))
    def _():
        finish_previous()

    for parity in range(2):
        @pl.when((j > 0) & (i < nb) & (j % 2 == parity))
        def _(parity=parity):
            up_gate(acts[parity])
            down(acts[1 - parity])


def _ffn(x, n_seq, T, state, mod, layer, g_in, w_up, w_conv, b_conv, w_down, g_out):
    N, D = x.shape
    L, F = w_down.shape[0], w_down.shape[1]
    S, R = _split_rows(n_seq, T, ROWS_FFN)
    rows = S * R
    tf = _largest_divisor(F, (COLS_FFN, 256, 128))
    nf = F // tf

    nb = N // rows

    def cur(i):
        return jnp.minimum(i, nb - 1)

    def prev(i):
        return jnp.maximum(i - 1, 0)

    def up_chunk(i, j):
        return jnp.where(i < nb, j, nf - 1)

    def down_chunk(i, j):
        return jnp.where((i == nb) & (j > 0), nf - 1, (j + nf - 1) % nf)

    def mod_spec(chunk, blk):
        return pl.BlockSpec((None, S, 1, D), lambda i, j: (layer, _seq_block(blk(i), S, R, T), 0, chunk))

    return pl.pallas_call(
        functools.partial(_ffn_kernel, S=S, R=R, T=T, nf=nf, nb=nb),
        grid=(nb + 1, nf),
        in_specs=[pl.BlockSpec((rows, D), lambda i, j: (cur(i), 0)),
                  pl.BlockSpec((rows, D), lambda i, j: (prev(i), 0)),
                  pl.BlockSpec((1, D), lambda i, j: (0, 0)),
                  mod_spec(4, cur), mod_spec(3, cur),
                  pl.BlockSpec((None, D, tf), lambda i, j: (layer, 0, up_chunk(i, j))),
                  pl.BlockSpec((None, D, tf), lambda i, j: (layer, 0, nf + up_chunk(i, j))),
                  pl.BlockSpec((None, CONV_TAPS, tf), lambda i, j: (layer, 0, up_chunk(i, j))),
                  pl.BlockSpec((None, 1, tf), lambda i, j: (layer, 0, up_chunk(i, j))),
                  pl.BlockSpec((None, tf, D), lambda i, j: (layer, down_chunk(i, j), 0)),
                  pl.BlockSpec((None, S, CONV_TAPS - 1, tf),
                               lambda i, j: (layer, _seq_block(cur(i), S, R, T), 0, up_chunk(i, j))),
                  mod_spec(5, prev),
                  pl.BlockSpec((1, D), lambda i, j: (0, 0))],
        out_specs=[pl.BlockSpec((rows, D), lambda i, j: (prev(i), 0)),
                   pl.BlockSpec((S, SUBLANES, tf), lambda i, j: (cur(i), 0, up_chunk(i, j)))],
        out_shape=[jax.ShapeDtypeStruct((N, D), F32),
                   jax.ShapeDtypeStruct((N // rows * S, SUBLANES, F), F32)],
        scratch_shapes=[pltpu.VMEM((rows, D), BF16), pltpu.VMEM((rows, D), F32),
                        pltpu.VMEM((rows, tf), BF16), pltpu.VMEM((rows, tf), BF16),
                        pltpu.VMEM((nf, SUBLANES, tf), F32)],
        compiler_params=_params("arbitrary", "arbitrary"),
        name="ffn",
    )(x, x, g_in.reshape(1, D), mod, mod, w_up, w_up, w_conv, b_conv.reshape(L, 1, F), w_down, state, mod,
      g_out.reshape(1, D))


def _last_rows(tails, n_seq):
    t = tails.reshape(n_seq, tails.shape[0] // n_seq, SUBLANES, tails.shape[2])
    return t[:, -1, SUBLANES - (CONV_TAPS - 1):, :]


def _trunk(x, mod, cache_k, cache_v, conv_a_prev, ffn_prev, norm_g, w_in_ab, w_conv_a, g_sgu, w_sgu, b_sgu,
           w_out_ab, w_qkv_sb, w_o_sb, w_ffn_up, w_ffn_conv, b_ffn_conv, w_ffn_down, heads, emit_vn):
    n_seq, T, D = x.shape
    x = x.reshape(n_seq * T, D)
    depth = norm_g.shape[0]
    n_odd = w_qkv_sb.shape[0]
    q_scale = (D // heads) ** -0.5 * LOG2E
    kv = None
    conv_a_new, ffn_new, sgu_v_new = [], [], []
    for l in range(depth):
        i = l // 2
        if l % 2 == 0:
            (z,) = _in_proj(x, n_seq, T, norm_g[l, 0], mod, l, 1, 0, w_in_ab, i, (F32,), (1.0,))
            res = _mix_even(z, x, n_seq, T, conv_a_prev, mod, l, i, w_conv_a, g_sgu, w_sgu, b_sgu, w_out_ab,
                            norm_g[l, 1], emit_vn)
            x = res[0]
            conv_a_new.append(_last_rows(res[1], n_seq))
            if emit_vn:
                sgu_v_new.append(res[2].reshape(n_seq, T, -1))
        else:
            q, *kv = _in_proj(x, n_seq, T, norm_g[l, 0], mod, l, 1, 0, w_qkv_sb, i, (BF16, F32, F32),
                              (q_scale, 1.0, 1.0), stacks=kv, slot=i, n_slots=n_odd)
            if cache_k is None:
                o = _attn_prompt(q, kv[0], kv[1], i, n_seq, T, heads)
            else:
                o = _attn_sample(q, kv[0], kv[1], i, cache_k, cache_v, n_seq, T, heads)
            x = _out_proj(o, w_o_sb, i, x, n_seq, T, mod, l, norm_g[l, 1])
        x, tails = _ffn(x, n_seq, T, ffn_prev, mod, l, norm_g[l, 2], w_ffn_up, w_ffn_conv, b_ffn_conv, w_ffn_down,
                        norm_g[l, 3])
        ffn_new.append(_last_rows(tails, n_seq))
    k_new, v_new = (a.reshape(n_odd, n_seq, T, heads, D // heads) for a in kv)
    return x.reshape(n_seq, T, D), k_new, v_new, jnp.stack(conv_a_new), jnp.stack(ffn_new), sgu_v_new


def kernel(x_prompt, x_sample, c_prompt, c_sample, cache_sb_k, cache_sb_v, state_conv_a, state_ffn_conv, w_mod, b_mod,
           norm_g, w_in_ab, w_conv_a, g_sgu, w_sgu, b_sgu, w_out_ab, w_qkv_sb, w_o_sb, w_ffn_up, w_ffn_conv,
           b_ffn_conv, w_ffn_down):
    bp, bs = x_prompt.shape[0], x_sample.shape[0]
    depth = norm_g.shape[0]
    heads = cache_sb_k.shape[3]

    mod = _modulation(jnp.concatenate([c_prompt, c_sample], axis=0), w_mod, b_mod)
    mod_p = mod[:, :bp].reshape(depth, bp, 1, -1)
    mod_s = mod[:, bp:].reshape(depth, bs, 1, -1)

    weights = (norm_g, w_in_ab.astype(BF16), w_conv_a, g_sgu, w_sgu, b_sgu, w_out_ab.astype(BF16),
               w_qkv_sb.astype(BF16), w_o_sb.astype(BF16), w_ffn_up.astype(BF16), w_ffn_conv, b_ffn_conv,
               w_ffn_down.astype(BF16))

    zeros_a = jnp.zeros((state_conv_a.shape[0], bp) + state_conv_a.shape[2:], F32)
    zeros_f = jnp.zeros((depth, bp) + state_ffn_conv.shape[2:], F32)
    y_s, k_s, v_s, conv_a_s, ffn_s, sgu_v = _trunk(x_sample, mod_s, cache_sb_k, cache_sb_v, state_conv_a,
                                                   state_ffn_conv, *weights, heads=heads, emit_vn=True)
    sample_out, (x_prompt, mod_p) = lax.optimization_barrier(
        ((y_s, k_s, v_s, conv_a_s, ffn_s, sgu_v), (x_prompt, mod_p)))
    y_s, k_s, v_s, conv_a_s, ffn_s, sgu_v = sample_out
    y_p, k_p, v_p, conv_a_p, ffn_p, _ = _trunk(x_prompt, mod_p, None, None, zeros_a, zeros_f, *weights,
                                               heads=heads, emit_vn=False)
    return (y_p, y_s, k_p, v_p, conv_a_p, ffn_p, k_s, v_s, conv_a_s, ffn_s, jnp.stack(sgu_v))
```

```python
import functools
import math

import jax
import jax.numpy as jnp
from jax import lax
from jax.experimental import pallas as pl
from jax.experimental.pallas import tpu as pltpu

F32 = jnp.float32
BF16 = jnp.bfloat16

EPS = 1e-6
CONV_TAPS = 3
GMLP_CAUSAL_BLOCK = 64
SUBLANES = 8
VMEM_LIMIT_BYTES = 56 * 1024 * 1024

ROWS_IN_PROJ = 1024
ROWS_MIX = 256
ROWS_OUT_PROJ = 512
ROWS_FFN = 512
COLS_IN_PROJ = 512
COLS_FFN = 512
COLS_MOD = 1024
ATTN_BLOCK = 256
ATTN_PROMPT_HEADS = 4
ATTN_SAMPLE_HEADS = 8

LOG2E = math.log2(math.e)
SB_Z_MAX = 126.0
SB_DONE = 150.0 / LOG2E


def _params(*semantics):
    return pltpu.CompilerParams(dimension_semantics=semantics, vmem_limit_bytes=VMEM_LIMIT_BYTES)


def _split_rows(n_seq, seq_len, rows):
    if seq_len >= rows:
        assert seq_len % rows == 0
        return 1, rows
    s = min(rows // seq_len, n_seq)
    assert n_seq % s == 0 and seq_len % SUBLANES == 0
    return s, seq_len


def _seq_block(i, S, R, T):
    return i // (T // R) if S == 1 else i


def _largest_divisor(n, candidates):
    for c in candidates:
        if n % c == 0:
            return c
    raise ValueError(f"no block size among {candidates} divides {n}")


def _rms(x, g):
    return x * lax.rsqrt(jnp.mean(x * x, axis=-1, keepdims=True) + EPS) * g


def _per_row(v, R):
    S, _, C = v.shape
    if S == 1:
        return v[0]
    return jnp.broadcast_to(v, (S, R, C)).reshape(S * R, C)


def _sigmoid(x):
    return 1.0 / (1.0 + jnp.exp(-x))


def _conv_taps(a, p0, p1, w, R):
    rt = lax.broadcasted_iota(jnp.int32, a.shape, 0) & (R - 1)
    s1 = jnp.where(rt == 0, p1, pltpu.roll(a, 1, 0))
    s2 = jnp.where(rt == 0, p0, jnp.where(rt == 1, p1, pltpu.roll(a, 2, 0)))
    return s2 * w[0:1] + s1 * w[1:2] + a * w[2:3]


def _causal_conv(a, p0, p1, w, R, S):
    if S > 1:
        return _conv_taps(a, p0, p1, w, R)
    body = pltpu.roll(a, 2, 0) * w[0:1] + pltpu.roll(a, 1, 0) * w[1:2] + a * w[2:3]
    head = _conv_taps(a[0:SUBLANES], p0, p1, w, R)
    return jnp.concatenate([head, body[SUBLANES:]], axis=0)


NORM_ROWS = 128


def _row_chunks(ref_sc, c, S, R, step):
    if S == 1:
        return ref_sc[0]
    k = step // R
    return _per_row(ref_sc[pl.ds(c * k, k)], R)


def _norm_mod_rows(x_ref, g, sc_ref, sh_ref, h_ref, S, R, straight_line=False):
    rows = S * R
    step = min(NORM_ROWS, rows)
    assert rows % step == 0 and (S == 1 or step % R == 0)

    def body(c, carry):
        rs = pl.ds(c * step, step) if straight_line else pl.ds(pl.multiple_of(c * step, step), step)
        h = _rms(x_ref[rs, :], g) * (1.0 + _row_chunks(sc_ref, c, S, R, step)) + _row_chunks(sh_ref, c, S, R, step)
        h_ref[rs, :] = h.astype(BF16)
        return carry

    _for_chunks(rows // step, body, straight_line)


def _residual_norm_rows(x_ref, y_ref, gate_ref, g, xo_ref, S, R, straight_line=False):
    rows = S * R
    step = min(NORM_ROWS, rows)
    assert rows % step == 0 and (S == 1 or step % R == 0)

    def body(c, carry):
        rs = pl.ds(c * step, step) if straight_line else pl.ds(pl.multiple_of(c * step, step), step)
        xo_ref[rs, :] = x_ref[rs, :] + _row_chunks(gate_ref, c, S, R, step) * _rms(y_ref[rs, :], g)
        return carry

    _for_chunks(rows // step, body, straight_line)


def _for_chunks(n, body, straight_line):
    if straight_line:
        for c in range(n):
            body(c, 0)
    else:
        lax.fori_loop(0, n, body, 0)


def _mod_kernel(c_ref, w_ref, b_ref, o_ref):
    c = c_ref[...]
    s = (c * _sigmoid(c)).astype(BF16)
    o_ref[...] = jnp.dot(s, w_ref[...].astype(BF16), preferred_element_type=F32) + b_ref[...]


def _modulation(c, w_mod, b_mod):
    L, D, N = w_mod.shape
    B = c.shape[0]
    tn = _largest_divisor(N, (COLS_MOD, 512, 256, 128))
    return pl.pallas_call(
        _mod_kernel,
        grid=(L, N // tn),
        in_specs=[pl.BlockSpec((B, D), lambda l, j: (0, 0)),
                  pl.BlockSpec((None, D, tn), lambda l, j: (l, 0, j)),
                  pl.BlockSpec((None, 1, tn), lambda l, j: (l, 0, j))],
        out_specs=pl.BlockSpec((None, B, tn), lambda l, j: (l, 0, j)),
        out_shape=jax.ShapeDtypeStruct((L, B, N), F32),
        compiler_params=_params("parallel", "parallel"),
        name="modulation",
    )(c, w_mod, b_mod.reshape(L, 1, N))


def _in_proj_kernel(x_ref, g_ref, sc_ref, sh_ref, w_ref, *rest, S, R, nj, blocks_per_out, n_out, out_scales,
                    fill_slots):
    outs, hs = rest[-2 - n_out:-2], rest[-2:]
    i, j = pl.program_id(0), pl.program_id(1)

    def store(k, y):
        val = (y if out_scales[k] == 1.0 else y * out_scales[k]).astype(outs[k].dtype)
        if k > 0 and fill_slots:
            for s in range(outs[k].shape[0]):
                outs[k][s] = val
        else:
            outs[k][...] = val

    @pl.when((i == 0) & (j == 0))
    def _():
        _norm_mod_rows(x_ref, g_ref[...], sc_ref, sh_ref, hs[0], S, R)

    for parity in range(2):
        @pl.when((i % 2 == parity) & (j < nj - 1))
        def _(parity=parity):
            y = jnp.dot(hs[parity][...], w_ref[...], preferred_element_type=F32)
            if n_out == 1:
                store(0, y)
            else:
                for k in range(n_out):
                    @pl.when(j // blocks_per_out == k)
                    def _(k=k):
                        store(k, y)

        @pl.when((i % 2 == parity) & (j == nj - 1))
        def _(parity=parity):
            _norm_mod_rows(x_ref, g_ref[...], sc_ref, sh_ref, hs[1 - parity], S, R, straight_line=True)
            store(n_out - 1, jnp.dot(hs[parity][...], w_ref[...], preferred_element_type=F32))


def _in_proj(x, n_seq, T, g, mod, layer, sc_chunk, sh_chunk, w, wi, out_dtypes, out_scales, stacks=None, slot=0,
             n_slots=1):
    N, D = x.shape
    n_total = w.shape[2]
    n_out = len(out_dtypes)
    S, R = _split_rows(n_seq, T, ROWS_IN_PROJ)
    rows = S * R
    nb = N // rows
    width = n_total // n_out
    tn = _largest_divisor(width, (COLS_IN_PROJ, 256, 128))
    bpo = width // tn
    nj = n_total // tn

    def norm_block(i, j):
        return jnp.where(j == nj - 1, jnp.minimum(i + 1, nb - 1), i)

    def mod_spec(chunk):
        return pl.BlockSpec((None, S, 1, D), lambda i, j: (layer, _seq_block(norm_block(i, j), S, R, T), 0, chunk))

    def col(j, k):
        return jnp.clip(j - k * bpo, 0, bpo - 1)

    out_specs = [pl.BlockSpec((rows, tn), lambda i, j: (i, col(j, 0)))]
    out_shape = [jax.ShapeDtypeStruct((N, width), out_dtypes[0])]
    for k in range(1, n_out):
        if stacks is None:
            spec = pl.BlockSpec((n_slots, rows, tn), functools.partial(lambda i, j, k: (0, i, col(j, k)), k=k))
        else:
            spec = pl.BlockSpec((None, rows, tn), functools.partial(lambda i, j, k: (slot, i, col(j, k)), k=k))
        out_specs.append(spec)
        out_shape.append(jax.ShapeDtypeStruct((n_slots, N, width), out_dtypes[k]))
    in_specs = [pl.BlockSpec((rows, D), lambda i, j: (norm_block(i, j), 0)),
                pl.BlockSpec((1, D), lambda i, j: (0, 0)),
                mod_spec(sc_chunk), mod_spec(sh_chunk),
                pl.BlockSpec((None, D, tn), lambda i, j: (wi, 0, j))]
    args = [x, g.reshape(1, D), mod, mod, w]
    aliases = {}
    if stacks is not None:
        for k, st in enumerate(stacks):
            aliases[len(args)] = k + 1
            in_specs.append(pl.BlockSpec(memory_space=pl.ANY))
            args.append(st)
    return pl.pallas_call(
        functools.partial(_in_proj_kernel, S=S, R=R, nj=nj, blocks_per_out=bpo, n_out=n_out,
                          out_scales=tuple(out_scales), fill_slots=stacks is None),
        grid=(nb, nj),
        in_specs=in_specs, out_specs=out_specs, out_shape=out_shape,
        scratch_shapes=[pltpu.VMEM((rows, D), BF16), pltpu.VMEM((rows, D), BF16)],
        input_output_aliases=aliases,
        compiler_params=_params("arbitrary", "arbitrary"),
        name="in_proj",
    )(*args)


def _sgu_matrices(ws_ref, bs_ref, g, T, chunk, whole_chunks):
    w = ws_ref[g]
    b = jnp.broadcast_to(bs_ref[g], (chunk, chunk))
    ii = lax.broadcasted_iota(jnp.int32, (chunk, chunk), 0)
    jj = lax.broadcasted_iota(jnp.int32, (chunk, chunk), 1)
    if whole_chunks:
        return jnp.where(jj // GMLP_CAUSAL_BLOCK <= ii // GMLP_CAUSAL_BLOCK, w, 0.0), b
    w_tl = jnp.where((ii < T) & (jj < T), w, 0.0)
    b_tl = jnp.where(ii < T, b, 0.0)
    wm, bm = w_tl, b_tl
    for k in range(1, chunk // T):
        wm = wm + pltpu.roll(pltpu.roll(w_tl, k * T, 0), k * T, 1)
        bm = bm + pltpu.roll(b_tl, k * T, 0)
    return wm, bm


def _mix_even_kernel(*refs, S, R, T, chunk, groups, halo, emit_vn):
    refs = list(refs)
    xa_ref, gb_ref, gc_ref, u_ref, v_ref = refs[:5]
    del refs[:5]
    if halo:
        hxa_ref, hgc_ref = refs[:2]
        del refs[:2]
    st_ref, x_ref, g1_ref, wc_ref, gs_ref, ws_ref, bs_ref, wo_ref, gn_ref = refs[:9]
    del refs[:9]
    xo_ref, tail_ref = refs[:2]
    del refs[:2]
    if emit_vn:
        vn_out_ref = refs.pop(0)
    cat_ref, vn_ref = refs

    i = pl.program_id(0)
    rows = S * R
    C = xa_ref.shape[1]

    ca = gc_ref[...] * xa_ref[...]
    st = st_ref[...]
    p0, p1 = st[:, 0:1, :], st[:, 1:2, :]
    if halo:
        hca = hgc_ref[...] * hxa_ref[...]
        first = (i % (T // R)) == 0
        p0 = jnp.where(first, p0, hca[SUBLANES - 2:SUBLANES - 1][None])
        p1 = jnp.where(first, p1, hca[SUBLANES - 1:SUBLANES][None])
    conv = _causal_conv(ca, _per_row(p0, R), _per_row(p1, R), wc_ref[...], R, S)
    cat_ref[:, 0:C] = (gb_ref[...] * conv).astype(BF16)
    tail_ref[...] = ca.reshape(S, R, C)[:, R - SUBLANES:R, :]

    vn = _rms(v_ref[...], gs_ref[...])
    if emit_vn:
        vn_out_ref[...] = vn
    vn_ref[...] = vn.astype(BF16)
    gd = v_ref.shape[1] // groups
    for g in range(groups):
        wm, bm = _sgu_matrices(ws_ref, bs_ref, g, T, chunk, whole_chunks=(R % chunk == 0))
        wm = wm.astype(BF16)
        cs = slice(g * gd, (g + 1) * gd)
        for c in range(rows // chunk):
            rs = slice(c * chunk, (c + 1) * chunk)
            mixed = jnp.dot(wm, vn_ref[rs, cs], preferred_element_type=F32) + bm
            cat_ref[rs, C + g * gd:C + (g + 1) * gd] = (u_ref[rs, cs] * mixed).astype(BF16)

    y = jnp.dot(cat_ref[...], wo_ref[...], preferred_element_type=F32)
    xo_ref[...] = x_ref[...] + _per_row(g1_ref[...], R) * _rms(y, gn_ref[...])


def _mix_even(z, x, n_seq, T, state, mod, layer, wi, w_conv, g_sgu, w_sgu, b_sgu, w_out, g_norm, emit_vn):
    N, D = x.shape
    C = w_conv.shape[2]
    groups, chunk = w_sgu.shape[1], w_sgu.shape[2]
    Cb = g_sgu.shape[1]
    S, R = _split_rows(n_seq, T, ROWS_MIX)
    rows = S * R
    halo = R < T
    assert rows % chunk == 0 and (R % chunk == 0 or (chunk % T == 0 and T <= GMLP_CAUSAL_BLOCK))
    assert z.shape[1] == 3 * C + 2 * Cb and C == Cb

    in_specs = [pl.BlockSpec((rows, C), functools.partial(lambda i, k: (i, k), k=k)) for k in range(5)]
    args = [z] * 5
    if halo:
        per = R // SUBLANES
        in_specs += [pl.BlockSpec((SUBLANES, C), functools.partial(
            lambda i, k: (jnp.maximum(i * per - 1, 0), k), k=k)) for k in (0, 2)]
        args += [z, z]
    in_specs += [
        pl.BlockSpec((None, S, CONV_TAPS - 1, C), lambda i: (wi, _seq_block(i, S, R, T), 0, 0)),
        pl.BlockSpec((rows, D), lambda i: (i, 0)),
        pl.BlockSpec((None, S, 1, D), lambda i: (layer, _seq_block(i, S, R, T), 0, 2)),
        pl.BlockSpec((None, CONV_TAPS, C), lambda i: (wi, 0, 0)),
        pl.BlockSpec((None, 1, Cb), lambda i: (wi, 0, 0)),
        pl.BlockSpec((None, groups, chunk, chunk), lambda i: (wi, 0, 0, 0)),
        pl.BlockSpec((None, groups, chunk, 1), lambda i: (wi, 0, 0, 0)),
        pl.BlockSpec((None, C + Cb, D), lambda i: (wi, 0, 0)),
        pl.BlockSpec((1, D), lambda i: (0, 0)),
    ]
    n_even = w_conv.shape[0]
    args += [state, x, mod, w_conv, g_sgu.reshape(n_even, 1, Cb), w_sgu, b_sgu.reshape(n_even, groups, chunk, 1),
             w_out, g_norm.reshape(1, D)]
    out_specs = [pl.BlockSpec((rows, D), lambda i: (i, 0)),
                 pl.BlockSpec((S, SUBLANES, C), lambda i: (i, 0, 0))]
    out_shape = [jax.ShapeDtypeStruct((N, D), F32),
                 jax.ShapeDtypeStruct((N // rows * S, SUBLANES, C), F32)]
    if emit_vn:
        out_specs.append(pl.BlockSpec((rows, Cb), lambda i: (i, 0)))
        out_shape.append(jax.ShapeDtypeStruct((N, Cb), F32))
    return pl.pallas_call(
        functools.partial(_mix_even_kernel, S=S, R=R, T=T, chunk=chunk, groups=groups, halo=halo, emit_vn=emit_vn),
        grid=(N // rows,),
        in_specs=in_specs, out_specs=out_specs, out_shape=out_shape,
        scratch_shapes=[pltpu.VMEM((rows, C + Cb), BF16), pltpu.VMEM((rows, Cb), BF16)],
        compiler_params=_params("parallel"),
        name="mix_even",
    )(*args)


def _upper_inclusive(n):
    r = lax.broadcasted_iota(jnp.int32, (n, n), 0)
    c = lax.broadcasted_iota(jnp.int32, (n, n), 1)
    return jnp.where(r >= c, 1.0, 0.0).astype(BF16)


def _strictly_earlier(n):
    t = lax.broadcasted_iota(jnp.int32, (n, n), 0)
    s = lax.broadcasted_iota(jnp.int32, (n, n), 1)
    return s < t


def _sb_blocks(qs, ks, vs, u_inc, cs, accs, mask):
    tq = qs[0].shape[0]
    zs, parts = [], []
    for q, k in zip(qs, ks):
        z = jnp.minimum(lax.dot_general(q, k, (((1,), (1,)), ((), ())), preferred_element_type=F32), SB_Z_MAX)
        fail = jnp.log(1.0 + jnp.exp2(z))
        if mask is not None:
            fail = jnp.where(mask, fail, 0.0)
        parts.append(fail.astype(BF16))
        zs.append(z)
    sums = jnp.dot(jnp.concatenate(parts, axis=0), u_inc, preferred_element_type=F32)
    cs_out, accs_out = [], []
    for h, (z, v, c, acc) in enumerate(zip(zs, vs, cs, accs)):
        incl = sums[h * tq:(h + 1) * tq]
        w = jnp.exp2(z - (incl + c) * LOG2E)
        if mask is not None:
            w = jnp.where(mask, w, 0.0)
        accs_out.append(acc + jnp.dot(w.astype(BF16), v, preferred_element_type=F32))
        cs_out.append(c + incl[:, 0:1])
    return tuple(cs_out), tuple(accs_out)


def _unfinished(cs):
    return (jnp.min(functools.reduce(jnp.minimum, cs)) < SB_DONE).astype(jnp.int32)


def _attn_prompt_kernel(q_ref, k_ref, v_ref, o_ref, kb_ref, vb_ref, u_ref, *, tq, heads, dh):
    qi = pl.program_id(2)

    @pl.when(qi == 0)
    def _():
        kb_ref[...] = k_ref[...].astype(BF16)
        vb_ref[...] = v_ref[...].astype(BF16)
        u_ref[...] = _upper_inclusive(tq)

    u = u_ref[...]
    cols = [slice(h * dh, (h + 1) * dh) for h in range(heads)]
    qs = [q_ref[:, cs] for cs in cols]

    def sweep(n, cs, accs, mask):
        rows = pl.ds(pl.multiple_of((qi - n) * tq, tq), tq)
        return _sb_blocks(qs, [kb_ref[rows, cs_] for cs_ in cols], [vb_ref[rows, cs_] for cs_ in cols], u, cs, accs,
                          mask)

    cs, accs = sweep(0, (jnp.zeros((tq, 1), F32),) * heads, (jnp.zeros((tq, dh), F32),) * heads,
                     _strictly_earlier(tq))

    def body(carry):
        n, _, cs, accs = carry
        cs, accs = sweep(n, cs, accs, None)
        return n + 1, _unfinished(cs), cs, accs

    _, _, _, accs = lax.while_loop(lambda carry: (carry[0] <= qi) & (carry[1] > 0), body,
                                   (jnp.int32(1), _unfinished(cs), cs, accs))
    for h in range(heads):
        o_ref[:, cols[h]] = accs[h].astype(o_ref.dtype)


def _attn_prompt(q, k_stack, v_stack, slot, n_seq, T, heads):
    N, D = q.shape
    dh = D // heads
    tq = min(ATTN_BLOCK, T)
    hp = _largest_divisor(heads, (ATTN_PROMPT_HEADS, 1))
    assert T % tq == 0
    n_slots = k_stack.shape[0]
    kv_spec = pl.BlockSpec((None, None, T, hp * dh), lambda b, h, i: (slot, b, 0, h))
    o = pl.pallas_call(
        functools.partial(_attn_prompt_kernel, tq=tq, heads=hp, dh=dh),
        grid=(n_seq, heads // hp, T // tq),
        in_specs=[pl.BlockSpec((None, tq, hp * dh), lambda b, h, i: (b, i, h)), kv_spec, kv_spec],
        out_specs=pl.BlockSpec((None, tq, hp * dh), lambda b, h, i: (b, i, h)),
        out_shape=jax.ShapeDtypeStruct((n_seq, T, D), BF16),
        scratch_shapes=[pltpu.VMEM((T, hp * dh), BF16), pltpu.VMEM((T, hp * dh), BF16),
                        pltpu.VMEM((tq, tq), BF16)],
        compiler_params=_params("parallel", "parallel", "arbitrary"),
        name="attn_prompt",
    )(q.reshape(n_seq, T, D), k_stack.reshape(n_slots, n_seq, T, D), v_stack.reshape(n_slots, n_seq, T, D))
    return o.reshape(N, D)


def _attn_sample_kernel(q_ref, kn_ref, vn_ref, ck_last_ref, cv_last_ref, ck_hbm, cv_hbm, o_ref, kbuf, vbuf, sem,
                        *, T, P, tk, heads, all_heads, dh, slot):
    u_past = _upper_inclusive(tk)
    cols = [slice(h * dh, (h + 1) * dh) for h in range(heads)]
    qs = [q_ref[:, cs] for cs in cols]

    cs, accs = _sb_blocks(qs, [kn_ref[:, c].astype(BF16) for c in cols], [vn_ref[:, c].astype(BF16) for c in cols],
                          _upper_inclusive(T), (jnp.zeros((T, 1), F32),) * heads,
                          (jnp.zeros((T, dh), F32),) * heads, _strictly_earlier(T))

    b, h0 = pl.program_id(0), pl.program_id(1) * heads

    def head_rows(ref):
        return [ref[pl.ds(h0 + h, tk, stride=all_heads), :].astype(BF16) for h in range(heads)]

    cs, accs = _sb_blocks(qs, head_rows(ck_last_ref), head_rows(cv_last_ref), u_past, cs, accs, None)

    def body(carry):
        n, _, cs, accs = carry
        rows = pl.ds(pl.multiple_of((P - (n + 1) * tk) * all_heads, tk * all_heads), tk * all_heads)
        copies = [pltpu.make_async_copy(src.at[slot, b, rows, :], dst, sem.at[s])
                  for s, (src, dst) in enumerate(((ck_hbm, kbuf), (cv_hbm, vbuf)))]
        for c in copies:
            c.start()
        for c in copies:
            c.wait()
        cs, accs = _sb_blocks(qs, head_rows(kbuf), head_rows(vbuf), u_past, cs, accs, None)
        return n + 1, _unfinished(cs), cs, accs

    _, _, _, accs = lax.while_loop(lambda carry: (carry[0] < P // tk) & (carry[1] > 0), body,
                                   (jnp.int32(1), _unfinished(cs), cs, accs))
    for h in range(heads):
        o_ref[:, cols[h]] = accs[h].astype(o_ref.dtype)


def _attn_sample(q, k_stack, v_stack, slot, cache_k, cache_v, n_seq, T, heads):
    N, D = q.shape
    dh = D // heads
    P = cache_k.shape[2]
    tk = _largest_divisor(P, (ATTN_BLOCK, 128))
    hg = _largest_divisor(heads, (ATTN_SAMPLE_HEADS, heads))
    n_slots = k_stack.shape[0]
    q_spec = pl.BlockSpec((None, T, hg * dh), lambda b, g: (b, 0, g))
    new_spec = pl.BlockSpec((None, None, T, hg * dh), lambda b, g: (slot, b, 0, g))
    last_spec = pl.BlockSpec((None, None, tk * heads, dh), lambda b, g: (slot, b, P // tk - 1, 0))
    hbm_spec = pl.BlockSpec(memory_space=pl.ANY)
    past_shape = cache_k.shape[:2] + (P * heads, dh)
    cache_k, cache_v = cache_k.reshape(past_shape), cache_v.reshape(past_shape)
    o = pl.pallas_call(
        functools.partial(_attn_sample_kernel, T=T, P=P, tk=tk, heads=hg, all_heads=heads, dh=dh, slot=slot),
        grid=(n_seq, heads // hg),
        in_specs=[q_spec, new_spec, new_spec, last_spec, last_spec, hbm_spec, hbm_spec],
        out_specs=q_spec,
        out_shape=jax.ShapeDtypeStruct((n_seq, T, D), BF16),
        scratch_shapes=[pltpu.VMEM((tk * heads, dh), F32), pltpu.VMEM((tk * heads, dh), F32),
                        pltpu.SemaphoreType.DMA((2,))],
        compiler_params=_params("parallel", "arbitrary"),
        name="attn_sample",
    )(q.reshape(n_seq, T, D), k_stack.reshape(n_slots, n_seq, T, D), v_stack.reshape(n_slots, n_seq, T, D),
      cache_k, cache_v, cache_k, cache_v)
    return o.reshape(N, D)


def _out_proj_kernel(o_ref, w_ref, x_ref, g1_ref, gn_ref, xo_ref, *, R):
    y = jnp.dot(o_ref[...], w_ref[...], preferred_element_type=F32)
    xo_ref[...] = x_ref[...] + _per_row(g1_ref[...], R) * _rms(y, gn_ref[...])


def _out_proj(o, w, wi, x, n_seq, T, mod, layer, g_norm):
    N, D = x.shape
    S, R = _split_rows(n_seq, T, ROWS_OUT_PROJ)
    rows = S * R
    return pl.pallas_call(
        functools.partial(_out_proj_kernel, R=R),
        grid=(N // rows,),
        in_specs=[pl.BlockSpec((rows, D), lambda i: (i, 0)),
                  pl.BlockSpec((None, D, D), lambda i: (wi, 0, 0)),
                  pl.BlockSpec((rows, D), lambda i: (i, 0)),
                  pl.BlockSpec((None, S, 1, D), lambda i: (layer, _seq_block(i, S, R, T), 0, 2)),
                  pl.BlockSpec((1, D), lambda i: (0, 0))],
        out_specs=pl.BlockSpec((rows, D), lambda i: (i, 0)),
        out_shape=jax.ShapeDtypeStruct((N, D), F32),
        compiler_params=_params("parallel"),
        name="out_proj",
    )(o, w, x, mod, g_norm.reshape(1, D))


def _ffn_kernel(x_ref, xp_ref, gn_in_ref, sc_ref, sh_ref, wa_ref, wg_ref, wc_ref, bc_ref, wd_ref, st_ref, g2_ref,
                gn_out_ref, xo_ref, tail_ref, h_ref, acc_ref, act0_ref, act1_ref, carry_ref, *, S, R, T, nf, nb):
    i, j = pl.program_id(0), pl.program_id(1)
    acts = (act0_ref, act1_ref)
    last_act = acts[(nf - 1) % 2]

    def up_gate(dst_ref):
        h = h_ref[...]
        a = jnp.dot(h, wa_ref[...], preferred_element_type=F32)
        gate = jnp.dot(h, wg_ref[...], preferred_element_type=F32)
        tail = a.reshape(S, R, -1)[:, R - SUBLANES:R, :]
        tail_ref[...] = tail
        p0, p1 = st_ref[:, 0:1, :], st_ref[:, 1:2, :]
        if R < T:
            first = (i % (T // R)) == 0
            prev = carry_ref[j]
            p0 = jnp.where(first, p0, prev[SUBLANES - 2:SUBLANES - 1][None])
            p1 = jnp.where(first, p1, prev[SUBLANES - 1:SUBLANES][None])
            carry_ref[j] = tail[0]
        half = 0.5 * (_causal_conv(a, _per_row(p0, R), _per_row(p1, R), wc_ref[...], R, S) + bc_ref[...])
        dst_ref[...] = ((half + half * jnp.tanh(half)) * gate).astype(BF16)

    def down(src_ref):
        acc_ref[...] += jnp.dot(src_ref[...], wd_ref[...], preferred_element_type=F32)

    def finish_previous():
        down(last_act)
        _residual_norm_rows(xp_ref, acc_ref, g2_ref, gn_out_ref[...], xo_ref, S, R, straight_line=True)

    def start_block():
        _norm_mod_rows(x_ref, gn_in_ref[...], sc_ref, sh_ref, h_ref, S, R, straight_line=True)
        acc_ref[...] = jnp.zeros_like(acc_ref)
        up_gate(acts[0])

    @pl.when((j == 0) & (i == 0))
    def _():
        if R < T:
            carry_ref[...] = jnp.zeros_like(carry_ref)
        start_block()

    @pl.when((j == 0) & (i > 0) & (i < nb))
    def _():
        finish_previous()
        start_block()

    @pl.when((j == 0) & (i == nb))
    def _():
        finish_previous()

    for parity in range(2):
        @pl.when((j > 0) & (i < nb) & (j % 2 == parity))
        def _(parity=parity):
            up_gate(acts[parity])
            down(acts[1 - parity])


def _ffn(x, n_seq, T, state, mod, layer, g_in, w_up, w_conv, b_conv, w_down, g_out):
    N, D = x.shape
    L, F = w_down.shape[0], w_down.shape[1]
    S, R = _split_rows(n_seq, T, ROWS_FFN)
    rows = S * R
    tf = _largest_divisor(F, (COLS_FFN, 256, 128))
    nf = F // tf

    nb = N // rows

    def cur(i):
        return jnp.minimum(i, nb - 1)

    def prev(i):
        return jnp.maximum(i - 1, 0)

    def up_chunk(i, j):
        return jnp.where(i < nb, j, nf - 1)

    def down_chunk(i, j):
        return jnp.where((i == nb) & (j > 0), nf - 1, (j + nf - 1) % nf)

    def mod_spec(chunk, blk):
        return pl.BlockSpec((None, S, 1, D), lambda i, j: (layer, _seq_block(blk(i), S, R, T), 0, chunk))

    return pl.pallas_call(
        functools.partial(_ffn_kernel, S=S, R=R, T=T, nf=nf, nb=nb),
        grid=(nb + 1, nf),
        in_specs=[pl.BlockSpec((rows, D), lambda i, j: (cur(i), 0)),
                  pl.BlockSpec((rows, D), lambda i, j: (prev(i), 0)),
                  pl.BlockSpec((1, D), lambda i, j: (0, 0)),
                  mod_spec(4, cur), mod_spec(3, cur),
                  pl.BlockSpec((None, D, tf), lambda i, j: (layer, 0, up_chunk(i, j))),
                  pl.BlockSpec((None, D, tf), lambda i, j: (layer, 0, nf + up_chunk(i, j))),
                  pl.BlockSpec((None, CONV_TAPS, tf), lambda i, j: (layer, 0, up_chunk(i, j))),
                  pl.BlockSpec((None, 1, tf), lambda i, j: (layer, 0, up_chunk(i, j))),
                  pl.BlockSpec((None, tf, D), lambda i, j: (layer, down_chunk(i, j), 0)),
                  pl.BlockSpec((None, S, CONV_TAPS - 1, tf),
                               lambda i, j: (layer, _seq_block(cur(i), S, R, T), 0, up_chunk(i, j))),
                  mod_spec(5, prev),
                  pl.BlockSpec((1, D), lambda i, j: (0, 0))],
        out_specs=[pl.BlockSpec((rows, D), lambda i, j: (prev(i), 0)),
                   pl.BlockSpec((S, SUBLANES, tf), lambda i, j: (cur(i), 0, up_chunk(i, j)))],
        out_shape=[jax.ShapeDtypeStruct((N, D), F32),
                   jax.ShapeDtypeStruct((N // rows * S, SUBLANES, F), F32)],
        scratch_shapes=[pltpu.VMEM((rows, D), BF16), pltpu.VMEM((rows, D), F32),
                        pltpu.VMEM((rows, tf), BF16), pltpu.VMEM((rows, tf), BF16),
                        pltpu.VMEM((nf, SUBLANES, tf), F32)],
        compiler_params=_params("arbitrary", "arbitrary"),
        name="ffn",
    )(x, x, g_in.reshape(1, D), mod, mod, w_up, w_up, w_conv, b_conv.reshape(L, 1, F), w_down, state, mod,
      g_out.reshape(1, D))


def _last_rows(tails, n_seq):
    t = tails.reshape(n_seq, tails.shape[0] // n_seq, SUBLANES, tails.shape[2])
    return t[:, -1, SUBLANES - (CONV_TAPS - 1):, :]


def _trunk(x, mod, cache_k, cache_v, conv_a_prev, ffn_prev, norm_g, w_in_ab, w_conv_a, g_sgu, w_sgu, b_sgu,
           w_out_ab, w_qkv_sb, w_o_sb, w_ffn_up, w_ffn_conv, b_ffn_conv, w_ffn_down, heads, emit_vn):
    n_seq, T, D = x.shape
    x = x.reshape(n_seq * T, D)
    depth = norm_g.shape[0]
    n_odd = w_qkv_sb.shape[0]
    q_scale = (D // heads) ** -0.5 * LOG2E
    kv = None
    conv_a_new, ffn_new, sgu_v_new = [], [], []
    for l in range(depth):
        i = l // 2
        if l % 2 == 0:
            (z,) = _in_proj(x, n_seq, T, norm_g[l, 0], mod, l, 1, 0, w_in_ab, i, (F32,), (1.0,))
            res = _mix_even(z, x, n_seq, T, conv_a_prev, mod, l, i, w_conv_a, g_sgu, w_sgu, b_sgu, w_out_ab,
                            norm_g[l, 1], emit_vn)
            x = res[0]
            conv_a_new.append(_last_rows(res[1], n_seq))
            if emit_vn:
                sgu_v_new.append(res[2].reshape(n_seq, T, -1))
        else:
            q, *kv = _in_proj(x, n_seq, T, norm_g[l, 0], mod, l, 1, 0, w_qkv_sb, i, (BF16, F32, F32),
                              (q_scale, 1.0, 1.0), stacks=kv, slot=i, n_slots=n_odd)
            if cache_k is None:
                o = _attn_prompt(q, kv[0], kv[1], i, n_seq, T, heads)
            else:
                o = _attn_sample(q, kv[0], kv[1], i, cache_k, cache_v, n_seq, T, heads)
            if i == n_odd - 1:
                kv[0], kv[1], o = lax.optimization_barrier((kv[0], kv[1], o))
            x = _out_proj(o, w_o_sb, i, x, n_seq, T, mod, l, norm_g[l, 1])
        x, tails = _ffn(x, n_seq, T, ffn_prev, mod, l, norm_g[l, 2], w_ffn_up, w_ffn_conv, b_ffn_conv, w_ffn_down,
                        norm_g[l, 3])
        ffn_new.append(_last_rows(tails, n_seq))
    k_new, v_new = (a.reshape(n_odd, n_seq, T, heads, D // heads) for a in kv)
    return x.reshape(n_seq, T, D), k_new, v_new, jnp.stack(conv_a_new), jnp.stack(ffn_new), sgu_v_new


def kernel(x_prompt, x_sample, c_prompt, c_sample, cache_sb_k, cache_sb_v, state_conv_a, state_ffn_conv, w_mod, b_mod,
           norm_g, w_in_ab, w_conv_a, g_sgu, w_sgu, b_sgu, w_out_ab, w_qkv_sb, w_o_sb, w_ffn_up, w_ffn_conv,
           b_ffn_conv, w_ffn_down):
    bp, bs = x_prompt.shape[0], x_sample.shape[0]
    depth = norm_g.shape[0]
    heads = cache_sb_k.shape[3]

    mod = _modulation(jnp.concatenate([c_prompt, c_sample], axis=0), w_mod, b_mod)
    mod_p = mod[:, :bp].reshape(depth, bp, 1, -1)
    mod_s = mod[:, bp:].reshape(depth, bs, 1, -1)

    weights = (norm_g, w_in_ab.astype(BF16), w_conv_a, g_sgu, w_sgu, b_sgu, w_out_ab.astype(BF16),
               w_qkv_sb.astype(BF16), w_o_sb.astype(BF16), w_ffn_up.astype(BF16), w_ffn_conv, b_ffn_conv,
               w_ffn_down.astype(BF16))

    zeros_a = jnp.zeros((state_conv_a.shape[0], bp) + state_conv_a.shape[2:], F32)
    zeros_f = jnp.zeros((depth, bp) + state_ffn_conv.shape[2:], F32)
    y_s, k_s, v_s, conv_a_s, ffn_s, sgu_v = _trunk(x_sample, mod_s, cache_sb_k, cache_sb_v, state_conv_a,
                                                   state_ffn_conv, *weights, heads=heads, emit_vn=True)
    sample_out, (x_prompt, mod_p) = lax.optimization_barrier(
        ((y_s, k_s, v_s, conv_a_s, ffn_s, sgu_v), (x_prompt, mod_p)))
    y_s, k_s, v_s, conv_a_s, ffn_s, sgu_v = sample_out
    y_p, k_p, v_p, conv_a_p, ffn_p, _ = _trunk(x_prompt, mod_p, None, None, zeros_a, zeros_f, *weights,
                                               heads=heads, emit_vn=False)
    return (y_p, y_s, k_p, v_p, conv_a_p, ffn_p, k_s, v_s, conv_a_s, ffn_s, jnp.stack(sgu_v))
```

```python
import functools
import math

import jax
import jax.numpy as jnp
from jax import lax
from jax.experimental import pallas as pl
from jax.experimental.pallas import tpu as pltpu

F32 = jnp.float32
BF16 = jnp.bfloat16

EPS = 1e-6
CONV_TAPS = 3
GMLP_CAUSAL_BLOCK = 64
SUBLANES = 8
VMEM_LIMIT_BYTES = 56 * 1024 * 1024

ROWS_IN_PROJ = 1024
ROWS_MIX = 512
ROWS_OUT_PROJ = 512
ROWS_FFN = 512
COLS_IN_PROJ = 512
COLS_FFN = 512
COLS_MOD = 1024
ATTN_BLOCK = 256
ATTN_PROMPT_HEADS = 8
ATTN_SAMPLE_HEADS = 8

LOG2E = math.log2(math.e)
SB_Z_MAX = 126.0
SB_DONE = 150.0 / LOG2E


def _params(*semantics):
    return pltpu.CompilerParams(dimension_semantics=semantics, vmem_limit_bytes=VMEM_LIMIT_BYTES)


def _split_rows(n_seq, seq_len, rows):
    if seq_len >= rows:
        assert seq_len % rows == 0
        return 1, rows
    s = min(rows // seq_len, n_seq)
    assert n_seq % s == 0 and seq_len % SUBLANES == 0
    return s, seq_len


def _seq_block(i, S, R, T):
    return i // (T // R) if S == 1 else i


def _largest_divisor(n, candidates):
    for c in candidates:
        if n % c == 0:
            return c
    raise ValueError(f"no block size among {candidates} divides {n}")


def _rms(x, g):
    return x * lax.rsqrt(jnp.mean(x * x, axis=-1, keepdims=True) + EPS) * g


def _per_row(v, R):
    S, _, C = v.shape
    if S == 1:
        return v[0]
    return jnp.broadcast_to(v, (S, R, C)).reshape(S * R, C)


def _sigmoid(x):
    return 1.0 / (1.0 + jnp.exp(-x))


def _conv_taps(a, p0, p1, w, R):
    rt = lax.broadcasted_iota(jnp.int32, a.shape, 0) & (R - 1)
    s1 = jnp.where(rt == 0, p1, pltpu.roll(a, 1, 0))
    s2 = jnp.where(rt == 0, p0, jnp.where(rt == 1, p1, pltpu.roll(a, 2, 0)))
    return s2 * w[0:1] + s1 * w[1:2] + a * w[2:3]


def _causal_conv(a, p0, p1, w, R, S):
    if S > 1:
        return _conv_taps(a, p0, p1, w, R)
    body = pltpu.roll(a, 2, 0) * w[0:1] + pltpu.roll(a, 1, 0) * w[1:2] + a * w[2:3]
    head = _conv_taps(a[0:SUBLANES], p0, p1, w, R)
    return jnp.concatenate([head, body[SUBLANES:]], axis=0)


NORM_ROWS = 128


def _row_chunks(ref_sc, c, S, R, step):
    if S == 1:
        return ref_sc[0]
    k = step // R
    return _per_row(ref_sc[pl.ds(c * k, k)], R)


def _norm_mod_rows(x_ref, g, sc_ref, sh_ref, h_ref, S, R, straight_line=False):
    rows = S * R
    step = min(NORM_ROWS, rows)
    assert rows % step == 0 and (S == 1 or step % R == 0)

    def body(c, carry):
        _norm_mod_chunk(x_ref, g, sc_ref, sh_ref, h_ref, S, R, c)
        return carry

    _for_chunks(rows // step, body, straight_line)


def _norm_mod_chunk(x_ref, g, sc_ref, sh_ref, h_ref, S, R, c):
    step = min(NORM_ROWS, S * R)
    rs = pl.ds(c * step, step) if isinstance(c, int) else pl.ds(pl.multiple_of(c * step, step), step)
    h = _rms(x_ref[rs, :], g) * (1.0 + _row_chunks(sc_ref, c, S, R, step)) + _row_chunks(sh_ref, c, S, R, step)
    h_ref[rs, :] = h.astype(BF16)


def _residual_norm_rows(x_ref, y_ref, gate_ref, g, xo_ref, S, R, straight_line=False):
    rows = S * R
    step = min(NORM_ROWS, rows)
    assert rows % step == 0 and (S == 1 or step % R == 0)

    def body(c, carry):
        rs = pl.ds(c * step, step) if straight_line else pl.ds(pl.multiple_of(c * step, step), step)
        xo_ref[rs, :] = x_ref[rs, :] + _row_chunks(gate_ref, c, S, R, step) * _rms(y_ref[rs, :], g)
        return carry

    _for_chunks(rows // step, body, straight_line)


def _for_chunks(n, body, straight_line):
    if straight_line:
        for c in range(n):
            body(c, 0)
    else:
        lax.fori_loop(0, n, body, 0)


def _mod_kernel(c_ref, w_ref, b_ref, o_ref):
    c = c_ref[...]
    s = (c * _sigmoid(c)).astype(BF16)
    o_ref[...] = jnp.dot(s, w_ref[...].astype(BF16), preferred_element_type=F32) + b_ref[...]


def _modulation(c, w_mod, b_mod):
    L, D, N = w_mod.shape
    B = c.shape[0]
    tn = _largest_divisor(N, (COLS_MOD, 512, 256, 128))
    return pl.pallas_call(
        _mod_kernel,
        grid=(L, N // tn),
        in_specs=[pl.BlockSpec((B, D), lambda l, j: (0, 0)),
                  pl.BlockSpec((None, D, tn), lambda l, j: (l, 0, j)),
                  pl.BlockSpec((None, 1, tn), lambda l, j: (l, 0, j))],
        out_specs=pl.BlockSpec((None, B, tn), lambda l, j: (l, 0, j)),
        out_shape=jax.ShapeDtypeStruct((L, B, N), F32),
        compiler_params=_params("parallel", "parallel"),
        name="modulation",
    )(c, w_mod, b_mod.reshape(L, 1, N))


def _in_proj_kernel(x_ref, g_ref, sc_ref, sh_ref, w_ref, *rest, S, R, nj, blocks_per_out, n_out, out_scales,
                    fill_slots):
    outs, hs = rest[-2 - n_out:-2], rest[-2:]
    i, j = pl.program_id(0), pl.program_id(1)
    n_chunks = S * R // min(NORM_ROWS, S * R)
    per_step = next(d for d in range(1, n_chunks + 1) if n_chunks % d == 0 and d * (nj - 1) >= n_chunks)
    norm_steps = n_chunks // per_step

    def store(k, y):
        val = (y if out_scales[k] == 1.0 else y * out_scales[k]).astype(outs[k].dtype)
        if k > 0 and fill_slots:
            for s in range(outs[k].shape[0]):
                outs[k][s] = val
        else:
            outs[k][...] = val

    def project(parity):
        y = jnp.dot(hs[parity][...], w_ref[...], preferred_element_type=F32)
        if n_out == 1:
            store(0, y)
        else:
            for k in range(n_out):
                @pl.when(j // blocks_per_out == k)
                def _(k=k):
                    store(k, y)

    @pl.when((i == 0) & (j == 0))
    def _():
        _norm_mod_rows(x_ref, g_ref[...], sc_ref, sh_ref, hs[0], S, R)

    for parity in range(2):
        @pl.when((i % 2 == parity) & ((j == 0) | (j > norm_steps)))
        def _(parity=parity):
            project(parity)

        @pl.when((i % 2 == parity) & (j > 0) & (j <= norm_steps))
        def _(parity=parity):
            for t in range(per_step):
                _norm_mod_chunk(x_ref, g_ref[...], sc_ref, sh_ref, hs[1 - parity], S, R, (j - 1) * per_step + t)
            project(parity)


def _in_proj(x, n_seq, T, g, mod, layer, sc_chunk, sh_chunk, w, wi, out_dtypes, out_scales, stacks=None, slot=0,
             n_slots=1):
    N, D = x.shape
    n_total = w.shape[2]
    n_out = len(out_dtypes)
    S, R = _split_rows(n_seq, T, ROWS_IN_PROJ)
    rows = S * R
    nb = N // rows
    width = n_total // n_out
    tn = _largest_divisor(width, (COLS_IN_PROJ, 256, 128))
    bpo = width // tn
    nj = n_total // tn

    def norm_block(i, j):
        return jnp.where((i == 0) & (j == 0), 0, jnp.minimum(i + 1, nb - 1))

    def mod_spec(chunk):
        return pl.BlockSpec((None, S, 1, D), lambda i, j: (layer, _seq_block(norm_block(i, j), S, R, T), 0, chunk))

    def col(j, k):
        return jnp.clip(j - k * bpo, 0, bpo - 1)

    out_specs = [pl.BlockSpec((rows, tn), lambda i, j: (i, col(j, 0)))]
    out_shape = [jax.ShapeDtypeStruct((N, width), out_dtypes[0])]
    for k in range(1, n_out):
        if stacks is None:
            spec = pl.BlockSpec((n_slots, rows, tn), functools.partial(lambda i, j, k: (0, i, col(j, k)), k=k))
        else:
            spec = pl.BlockSpec((None, rows, tn), functools.partial(lambda i, j, k: (slot, i, col(j, k)), k=k))
        out_specs.append(spec)
        out_shape.append(jax.ShapeDtypeStruct((n_slots, N, width), out_dtypes[k]))
    in_specs = [pl.BlockSpec((rows, D), lambda i, j: (norm_block(i, j), 0)),
                pl.BlockSpec((1, D), lambda i, j: (0, 0)),
                mod_spec(sc_chunk), mod_spec(sh_chunk),
                pl.BlockSpec((None, D, tn), lambda i, j: (wi, 0, j))]
    args = [x, g.reshape(1, D), mod, mod, w]
    aliases = {}
    if stacks is not None:
        for k, st in enumerate(stacks):
            aliases[len(args)] = k + 1
            in_specs.append(pl.BlockSpec(memory_space=pl.ANY))
            args.append(st)
    return pl.pallas_call(
        functools.partial(_in_proj_kernel, S=S, R=R, nj=nj, blocks_per_out=bpo, n_out=n_out,
                          out_scales=tuple(out_scales), fill_slots=stacks is None),
        grid=(nb, nj),
        in_specs=in_specs, out_specs=out_specs, out_shape=out_shape,
        scratch_shapes=[pltpu.VMEM((rows, D), BF16), pltpu.VMEM((rows, D), BF16)],
        input_output_aliases=aliases,
        compiler_params=_params("arbitrary", "arbitrary"),
        name="in_proj",
    )(*args)


def _sgu_matrices(ws_ref, bs_ref, g, T, chunk, whole_chunks):
    w = ws_ref[g]
    b = jnp.broadcast_to(bs_ref[g], (chunk, chunk))
    ii = lax.broadcasted_iota(jnp.int32, (chunk, chunk), 0)
    jj = lax.broadcasted_iota(jnp.int32, (chunk, chunk), 1)
    if whole_chunks:
        return jnp.where(jj // GMLP_CAUSAL_BLOCK <= ii // GMLP_CAUSAL_BLOCK, w, 0.0), b
    w_tl = jnp.where((ii < T) & (jj < T), w, 0.0)
    b_tl = jnp.where(ii < T, b, 0.0)
    wm, bm = w_tl, b_tl
    for k in range(1, chunk // T):
        wm = wm + pltpu.roll(pltpu.roll(w_tl, k * T, 0), k * T, 1)
        bm = bm + pltpu.roll(b_tl, k * T, 0)
    return wm, bm


def _mix_even_kernel(*refs, S, R, T, chunk, groups, halo, emit_vn):
    refs = list(refs)
    xa_ref, gb_ref, gc_ref, u_ref, v_ref = refs[:5]
    del refs[:5]
    if halo:
        hxa_ref, hgc_ref = refs[:2]
        del refs[:2]
    st_ref, x_ref, g1_ref, wc_ref, gs_ref, ws_ref, bs_ref, wo_ref, gn_ref = refs[:9]
    del refs[:9]
    xo_ref, tail_ref = refs[:2]
    del refs[:2]
    if emit_vn:
        vn_out_ref = refs.pop(0)
    cat_ref, vn_ref = refs

    i = pl.program_id(0)
    rows = S * R
    C = xa_ref.shape[1]

    ca = gc_ref[...] * xa_ref[...]
    st = st_ref[...]
    p0, p1 = st[:, 0:1, :], st[:, 1:2, :]
    if halo:
        hca = hgc_ref[...] * hxa_ref[...]
        first = (i % (T // R)) == 0
        p0 = jnp.where(first, p0, hca[SUBLANES - 2:SUBLANES - 1][None])
        p1 = jnp.where(first, p1, hca[SUBLANES - 1:SUBLANES][None])
    conv = _causal_conv(ca, _per_row(p0, R), _per_row(p1, R), wc_ref[...], R, S)
    cat_ref[:, 0:C] = (gb_ref[...] * conv).astype(BF16)
    tail_ref[...] = ca.reshape(S, R, C)[:, R - SUBLANES:R, :]

    vn = _rms(v_ref[...], gs_ref[...])
    if emit_vn:
        vn_out_ref[...] = vn
    vn_ref[...] = vn.astype(BF16)
    gd = v_ref.shape[1] // groups
    for g in range(groups):
        wm, bm = _sgu_matrices(ws_ref, bs_ref, g, T, chunk, whole_chunks=(R % chunk == 0))
        wm = wm.astype(BF16)
        cs = slice(g * gd, (g + 1) * gd)
        for c in range(rows // chunk):
            rs = slice(c * chunk, (c + 1) * chunk)
            mixed = jnp.dot(wm, vn_ref[rs, cs], preferred_element_type=F32) + bm
            cat_ref[rs, C + g * gd:C + (g + 1) * gd] = (u_ref[rs, cs] * mixed).astype(BF16)

    y = jnp.dot(cat_ref[...], wo_ref[...], preferred_element_type=F32)
    xo_ref[...] = x_ref[...] + _per_row(g1_ref[...], R) * _rms(y, gn_ref[...])


def _mix_even(z, x, n_seq, T, state, mod, layer, wi, w_conv, g_sgu, w_sgu, b_sgu, w_out, g_norm, emit_vn):
    N, D = x.shape
    C = w_conv.shape[2]
    groups, chunk = w_sgu.shape[1], w_sgu.shape[2]
    Cb = g_sgu.shape[1]
    S, R = _split_rows(n_seq, T, ROWS_MIX // 2 if emit_vn else ROWS_MIX)
    rows = S * R
    halo = R < T
    assert rows % chunk == 0 and (R % chunk == 0 or (chunk % T == 0 and T <= GMLP_CAUSAL_BLOCK))
    assert z.shape[1] == 3 * C + 2 * Cb and C == Cb

    in_specs = [pl.BlockSpec((rows, C), functools.partial(lambda i, k: (i, k), k=k)) for k in range(5)]
    args = [z] * 5
    if halo:
        per = R // SUBLANES
        in_specs += [pl.BlockSpec((SUBLANES, C), functools.partial(
            lambda i, k: (jnp.maximum(i * per - 1, 0), k), k=k)) for k in (0, 2)]
        args += [z, z]
    in_specs += [
        pl.BlockSpec((None, S, CONV_TAPS - 1, C), lambda i: (wi, _seq_block(i, S, R, T), 0, 0)),
        pl.BlockSpec((rows, D), lambda i: (i, 0)),
        pl.BlockSpec((None, S, 1, D), lambda i: (layer, _seq_block(i, S, R, T), 0, 2)),
        pl.BlockSpec((None, CONV_TAPS, C), lambda i: (wi, 0, 0)),
        pl.BlockSpec((None, 1, Cb), lambda i: (wi, 0, 0)),
        pl.BlockSpec((None, groups, chunk, chunk), lambda i: (wi, 0, 0, 0)),
        pl.BlockSpec((None, groups, chunk, 1), lambda i: (wi, 0, 0, 0)),
        pl.BlockSpec((None, C + Cb, D), lambda i: (wi, 0, 0), pipeline_mode=pl.Buffered(1)),
        pl.BlockSpec((1, D), lambda i: (0, 0)),
    ]
    n_even = w_conv.shape[0]
    args += [state, x, mod, w_conv, g_sgu.reshape(n_even, 1, Cb), w_sgu, b_sgu.reshape(n_even, groups, chunk, 1),
             w_out, g_norm.reshape(1, D)]
    out_specs = [pl.BlockSpec((rows, D), lambda i: (i, 0)),
                 pl.BlockSpec((S, SUBLANES, C), lambda i: (i, 0, 0))]
    out_shape = [jax.ShapeDtypeStruct((N, D), F32),
                 jax.ShapeDtypeStruct((N // rows * S, SUBLANES, C), F32)]
    if emit_vn:
        out_specs.append(pl.BlockSpec((rows, Cb), lambda i: (i, 0)))
        out_shape.append(jax.ShapeDtypeStruct((N, Cb), F32))
    return pl.pallas_call(
        functools.partial(_mix_even_kernel, S=S, R=R, T=T, chunk=chunk, groups=groups, halo=halo, emit_vn=emit_vn),
        grid=(N // rows,),
        in_specs=in_specs, out_specs=out_specs, out_shape=out_shape,
        scratch_shapes=[pltpu.VMEM((rows, C + Cb), BF16), pltpu.VMEM((rows, Cb), BF16)],
        compiler_params=_params("parallel"),
        name="mix_even",
    )(*args)


def _upper_inclusive(n):
    r = lax.broadcasted_iota(jnp.int32, (n, n), 0)
    c = lax.broadcasted_iota(jnp.int32, (n, n), 1)
    return jnp.where(r >= c, 1.0, 0.0).astype(BF16)


def _strictly_earlier(n):
    t = lax.broadcasted_iota(jnp.int32, (n, n), 0)
    s = lax.broadcasted_iota(jnp.int32, (n, n), 1)
    return s < t


def _sb_blocks(qs, ks, vs, u_inc, cs, accs, mask):
    tq = qs[0].shape[0]
    zs, parts = [], []
    for q, k in zip(qs, ks):
        z = jnp.minimum(lax.dot_general(q, k, (((1,), (1,)), ((), ())), preferred_element_type=F32), SB_Z_MAX)
        fail = jnp.log(1.0 + jnp.exp2(z))
        if mask is not None:
            fail = jnp.where(mask, fail, 0.0)
        parts.append(fail.astype(BF16))
        zs.append(z)
    sums = jnp.dot(jnp.concatenate(parts, axis=0), u_inc, preferred_element_type=F32)
    cs_out, accs_out = [], []
    for h, (z, v, c, acc) in enumerate(zip(zs, vs, cs, accs)):
        incl = sums[h * tq:(h + 1) * tq]
        w = jnp.exp2(z - (incl + c) * LOG2E)
        if mask is not None:
            w = jnp.where(mask, w, 0.0)
        accs_out.append(acc + jnp.dot(w.astype(BF16), v, preferred_element_type=F32))
        cs_out.append(c + incl[:, 0:1])
    return tuple(cs_out), tuple(accs_out)


def _unfinished(cs):
    return (jnp.min(functools.reduce(jnp.minimum, cs)) < SB_DONE).astype(jnp.int32)


def _attn_prompt_kernel(q_ref, k_ref, v_ref, o_ref, kb_ref, vb_ref, u_ref, *, tq, heads, dh):
    qi = pl.program_id(2)

    @pl.when(qi == 0)
    def _():
        kb_ref[...] = k_ref[...].astype(BF16)
        vb_ref[...] = v_ref[...].astype(BF16)
        u_ref[...] = _upper_inclusive(tq)

    u = u_ref[...]
    cols = [slice(h * dh, (h + 1) * dh) for h in range(heads)]
    qs = [q_ref[:, cs] for cs in cols]

    def sweep(n, cs, accs, mask):
        rows = pl.ds(pl.multiple_of((qi - n) * tq, tq), tq)
        return _sb_blocks(qs, [kb_ref[rows, cs_] for cs_ in cols], [vb_ref[rows, cs_] for cs_ in cols], u, cs, accs,
                          mask)

    cs, accs = sweep(0, (jnp.zeros((tq, 1), F32),) * heads, (jnp.zeros((tq, dh), F32),) * heads,
                     _strictly_earlier(tq))

    def body(carry):
        n, _, cs, accs = carry
        cs, accs = sweep(n, cs, accs, None)
        return n + 1, _unfinished(cs), cs, accs

    _, _, _, accs = lax.while_loop(lambda carry: (carry[0] <= qi) & (carry[1] > 0), body,
                                   (jnp.int32(1), _unfinished(cs), cs, accs))
    for h in range(heads):
        o_ref[:, cols[h]] = accs[h].astype(o_ref.dtype)


def _attn_prompt(q, k_stack, v_stack, slot, n_seq, T, heads):
    N, D = q.shape
    dh = D // heads
    tq = min(ATTN_BLOCK, T)
    hp = _largest_divisor(heads, (ATTN_PROMPT_HEADS, 1))
    assert T % tq == 0
    n_slots = k_stack.shape[0]
    kv_spec = pl.BlockSpec((None, None, T, hp * dh), lambda b, h, i: (slot, b, 0, h))
    o = pl.pallas_call(
        functools.partial(_attn_prompt_kernel, tq=tq, heads=hp, dh=dh),
        grid=(n_seq, heads // hp, T // tq),
        in_specs=[pl.BlockSpec((None, tq, hp * dh), lambda b, h, i: (b, i, h)), kv_spec, kv_spec],
        out_specs=pl.BlockSpec((None, tq, hp * dh), lambda b, h, i: (b, i, h)),
        out_shape=jax.ShapeDtypeStruct((n_seq, T, D), BF16),
        scratch_shapes=[pltpu.VMEM((T, hp * dh), BF16), pltpu.VMEM((T, hp * dh), BF16),
                        pltpu.VMEM((tq, tq), BF16)],
        compiler_params=_params("parallel", "parallel", "arbitrary"),
        name="attn_prompt",
    )(q.reshape(n_seq, T, D), k_stack.reshape(n_slots, n_seq, T, D), v_stack.reshape(n_slots, n_seq, T, D))
    return o.reshape(N, D)


def _attn_sample_kernel(q_ref, kn_ref, vn_ref, ck_last_ref, cv_last_ref, ck_hbm, cv_hbm, o_ref, kbuf, vbuf, sem,
                        *, T, P, tk, heads, all_heads, dh, slot):
    u_past = _upper_inclusive(tk)
    cols = [slice(h * dh, (h + 1) * dh) for h in range(heads)]
    qs = [q_ref[:, cs] for cs in cols]

    cs, accs = _sb_blocks(qs, [kn_ref[:, c].astype(BF16) for c in cols], [vn_ref[:, c].astype(BF16) for c in cols],
                          _upper_inclusive(T), (jnp.zeros((T, 1), F32),) * heads,
                          (jnp.zeros((T, dh), F32),) * heads, _strictly_earlier(T))

    b, h0 = pl.program_id(0), pl.program_id(1) * heads

    def head_rows(ref):
        return [ref[pl.ds(h0 + h, tk, stride=all_heads), :].astype(BF16) for h in range(heads)]

    cs, accs = _sb_blocks(qs, head_rows(ck_last_ref), head_rows(cv_last_ref), u_past, cs, accs, None)

    def body(carry):
        n, _, cs, accs = carry
        rows = pl.ds(pl.multiple_of((P - (n + 1) * tk) * all_heads, tk * all_heads), tk * all_heads)
        copies = [pltpu.make_async_copy(src.at[slot, b, rows, :], dst, sem.at[s])
                  for s, (src, dst) in enumerate(((ck_hbm, kbuf), (cv_hbm, vbuf)))]
        for c in copies:
            c.start()
        for c in copies:
            c.wait()
        cs, accs = _sb_blocks(qs, head_rows(kbuf), head_rows(vbuf), u_past, cs, accs, None)
        return n + 1, _unfinished(cs), cs, accs

    _, _, _, accs = lax.while_loop(lambda carry: (carry[0] < P // tk) & (carry[1] > 0), body,
                                   (jnp.int32(1), _unfinished(cs), cs, accs))
    for h in range(heads):
        o_ref[:, cols[h]] = accs[h].astype(o_ref.dtype)


def _attn_sample(q, k_stack, v_stack, slot, cache_k, cache_v, n_seq, T, heads):
    N, D = q.shape
    dh = D // heads
    P = cache_k.shape[2]
    tk = _largest_divisor(P, (ATTN_BLOCK, 128))
    hg = _largest_divisor(heads, (ATTN_SAMPLE_HEADS, heads))
    n_slots = k_stack.shape[0]
    q_spec = pl.BlockSpec((None, T, hg * dh), lambda b, g: (b, 0, g))
    new_spec = pl.BlockSpec((None, None, T, hg * dh), lambda b, g: (slot, b, 0, g))
    last_spec = pl.BlockSpec((None, None, tk * heads, dh), lambda b, g: (slot, b, P // tk - 1, 0))
    hbm_spec = pl.BlockSpec(memory_space=pl.ANY)
    past_shape = cache_k.shape[:2] + (P * heads, dh)
    cache_k, cache_v = cache_k.reshape(past_shape), cache_v.reshape(past_shape)
    o = pl.pallas_call(
        functools.partial(_attn_sample_kernel, T=T, P=P, tk=tk, heads=hg, all_heads=heads, dh=dh, slot=slot),
        grid=(n_seq, heads // hg),
        in_specs=[q_spec, new_spec, new_spec, last_spec, last_spec, hbm_spec, hbm_spec],
        out_specs=q_spec,
        out_shape=jax.ShapeDtypeStruct((n_seq, T, D), BF16),
        scratch_shapes=[pltpu.VMEM((tk * heads, dh), F32), pltpu.VMEM((tk * heads, dh), F32),
                        pltpu.SemaphoreType.DMA((2,))],
        compiler_params=_params("parallel", "arbitrary"),
        name="attn_sample",
    )(q.reshape(n_seq, T, D), k_stack.reshape(n_slots, n_seq, T, D), v_stack.reshape(n_slots, n_seq, T, D),
      cache_k, cache_v, cache_k, cache_v)
    return o.reshape(N, D)


def _out_proj_kernel(o_ref, w_ref, x_ref, g1_ref, gn_ref, xo_ref, *, R):
    y = jnp.dot(o_ref[...], w_ref[...], preferred_element_type=F32)
    xo_ref[...] = x_ref[...] + _per_row(g1_ref[...], R) * _rms(y, gn_ref[...])


def _out_proj(o, w, wi, x, n_seq, T, mod, layer, g_norm):
    N, D = x.shape
    S, R = _split_rows(n_seq, T, ROWS_OUT_PROJ)
    rows = S * R
    return pl.pallas_call(
        functools.partial(_out_proj_kernel, R=R),
        grid=(N // rows,),
        in_specs=[pl.BlockSpec((rows, D), lambda i: (i, 0)),
                  pl.BlockSpec((None, D, D), lambda i: (wi, 0, 0)),
                  pl.BlockSpec((rows, D), lambda i: (i, 0)),
                  pl.BlockSpec((None, S, 1, D), lambda i: (layer, _seq_block(i, S, R, T), 0, 2)),
                  pl.BlockSpec((1, D), lambda i: (0, 0))],
        out_specs=pl.BlockSpec((rows, D), lambda i: (i, 0)),
        out_shape=jax.ShapeDtypeStruct((N, D), F32),
        compiler_params=_params("parallel"),
        name="out_proj",
    )(o, w, x, mod, g_norm.reshape(1, D))


def _ffn_kernel(x_ref, xp_ref, gn_in_ref, sc_ref, sh_ref, wa_ref, wg_ref, wc_ref, bc_ref, wd_ref, st_ref, g2_ref,
                gn_out_ref, xo_ref, tail_ref, h_ref, acc_ref, act0_ref, act1_ref, carry_ref, *, S, R, T, nf, nb):
    i, j = pl.program_id(0), pl.program_id(1)
    acts = (act0_ref, act1_ref)
    last_act = acts[(nf - 1) % 2]

    def up_gate(dst_ref):
        h = h_ref[...]
        a = jnp.dot(h, wa_ref[...], preferred_element_type=F32)
        gate = jnp.dot(h, wg_ref[...], preferred_element_type=F32)
        tail = a.reshape(S, R, -1)[:, R - SUBLANES:R, :]
        tail_ref[...] = tail
        p0, p1 = st_ref[:, 0:1, :], st_ref[:, 1:2, :]
        if R < T:
            first = (i % (T // R)) == 0
            prev = carry_ref[j]
            p0 = jnp.where(first, p0, prev[SUBLANES - 2:SUBLANES - 1][None])
            p1 = jnp.where(first, p1, prev[SUBLANES - 1:SUBLANES][None])
            carry_ref[j] = tail[0]
        half = 0.5 * (_causal_conv(a, _per_row(p0, R), _per_row(p1, R), wc_ref[...], R, S) + bc_ref[...])
        dst_ref[...] = ((half + half * jnp.tanh(half)) * gate).astype(BF16)

    def down(src_ref):
        acc_ref[...] += jnp.dot(src_ref[...], wd_ref[...], preferred_element_type=F32)

    def finish_previous():
        down(last_act)
        _residual_norm_rows(xp_ref, acc_ref, g2_ref, gn_out_ref[...], xo_ref, S, R, straight_line=True)

    def start_block():
        _norm_mod_rows(x_ref, gn_in_ref[...], sc_ref, sh_ref, h_ref, S, R, straight_line=True)
        acc_ref[...] = jnp.zeros_like(acc_ref)
        up_gate(acts[0])

    @pl.when((j == 0) & (i == 0))
    def _():
        if R < T:
            carry_ref[...] = jnp.zeros_like(carry_ref)
        start_block()

    @pl.when((j == 0) & (i > 0) & (i < nb))
    def _():
        finish_previous()
        start_block()

    @pl.when((j == 0) & (i == nb))
    def _():
        finish_previous()

    for parity in range(2):
        @pl.when((j > 0) & (i < nb) & (j % 2 == parity))
        def _(parity=parity):
            up_gate(acts[parity])
            down(acts[1 - parity])


def _ffn(x, n_seq, T, state, mod, layer, g_in, w_up, w_conv, b_conv, w_down, g_out):
    N, D = x.shape
    L, F = w_down.shape[0], w_down.shape[1]
    S, R = _split_rows(n_seq, T, ROWS_FFN)
    rows = S * R
    tf = _largest_divisor(F, (COLS_FFN, 256, 128))
    nf = F // tf

    nb = N // rows

    def cur(i):
        return jnp.minimum(i, nb - 1)

    def prev(i):
        return jnp.maximum(i - 1, 0)

    def up_chunk(i, j):
        return jnp.where(i < nb, j, nf - 1)

    def down_chunk(i, j):
        return jnp.where((i == nb) & (j > 0), nf - 1, (j + nf - 1) % nf)

    def mod_spec(chunk, blk):
        return pl.BlockSpec((None, S, 1, D), lambda i, j: (layer, _seq_block(blk(i), S, R, T), 0, chunk))

    return pl.pallas_call(
        functools.partial(_ffn_kernel, S=S, R=R, T=T, nf=nf, nb=nb),
        grid=(nb + 1, nf),
        in_specs=[pl.BlockSpec((rows, D), lambda i, j: (cur(i), 0)),
                  pl.BlockSpec((rows, D), lambda i, j: (prev(i), 0)),
                  pl.BlockSpec((1, D), lambda i, j: (0, 0)),
                  mod_spec(4, cur), mod_spec(3, cur),
                  pl.BlockSpec((None, D, tf), lambda i, j: (layer, 0, up_chunk(i, j))),
                  pl.BlockSpec((None, D, tf), lambda i, j: (layer, 0, nf + up_chunk(i, j))),
                  pl.BlockSpec((None, CONV_TAPS, tf), lambda i, j: (layer, 0, up_chunk(i, j))),
                  pl.BlockSpec((None, 1, tf), lambda i, j: (layer, 0, up_chunk(i, j))),
                  pl.BlockSpec((None, tf, D), lambda i, j: (layer, down_chunk(i, j), 0)),
                  pl.BlockSpec((None, S, CONV_TAPS - 1, tf),
                               lambda i, j: (layer, _seq_block(cur(i), S, R, T), 0, up_chunk(i, j))),
                  mod_spec(5, prev),
                  pl.BlockSpec((1, D), lambda i, j: (0, 0))],
        out_specs=[pl.BlockSpec((rows, D), lambda i, j: (prev(i), 0)),
                   pl.BlockSpec((S, SUBLANES, tf), lambda i, j: (cur(i), 0, up_chunk(i, j)))],
        out_shape=[jax.ShapeDtypeStruct((N, D), F32),
                   jax.ShapeDtypeStruct((N // rows * S, SUBLANES, F), F32)],
        scratch_shapes=[pltpu.VMEM((rows, D), BF16), pltpu.VMEM((rows, D), F32),
                        pltpu.VMEM((rows, tf), BF16), pltpu.VMEM((rows, tf), BF16),
                        pltpu.VMEM((nf, SUBLANES, tf), F32)],
        compiler_params=_params("arbitrary", "arbitrary"),
        name="ffn",
    )(x, x, g_in.reshape(1, D), mod, mod, w_up, w_up, w_conv, b_conv.reshape(L, 1, F), w_down, state, mod,
      g_out.reshape(1, D))


def _last_rows(tails, n_seq):
    t = tails.reshape(n_seq, tails.shape[0] // n_seq, SUBLANES, tails.shape[2])
    return t[:, -1, SUBLANES - (CONV_TAPS - 1):, :]


def _trunk(x, mod, cache_k, cache_v, conv_a_prev, ffn_prev, norm_g, w_in_ab, w_conv_a, g_sgu, w_sgu, b_sgu,
           w_out_ab, w_qkv_sb, w_o_sb, w_ffn_up, w_ffn_conv, b_ffn_conv, w_ffn_down, heads, emit_vn):
    n_seq, T, D = x.shape
    x = x.reshape(n_seq * T, D)
    depth = norm_g.shape[0]
    n_odd = w_qkv_sb.shape[0]
    q_scale = (D // heads) ** -0.5 * LOG2E
    kv = None
    conv_a_new, ffn_new, sgu_v_new = [], [], []
    for l in range(depth):
        i = l // 2
        if l % 2 == 0:
            (z,) = _in_proj(x, n_seq, T, norm_g[l, 0], mod, l, 1, 0, w_in_ab, i, (F32,), (1.0,))
            res = _mix_even(z, x, n_seq, T, conv_a_prev, mod, l, i, w_conv_a, g_sgu, w_sgu, b_sgu, w_out_ab,
                            norm_g[l, 1], emit_vn)
            x = res[0]
            conv_a_new.append(_last_rows(res[1], n_seq))
            if emit_vn:
                sgu_v_new.append(res[2].reshape(n_seq, T, -1))
        else:
            q, *kv = _in_proj(x, n_seq, T, norm_g[l, 0], mod, l, 1, 0, w_qkv_sb, i, (BF16, F32, F32),
                              (q_scale, 1.0, 1.0), stacks=kv, slot=i, n_slots=n_odd)
            if cache_k is None:
                o = _attn_prompt(q, kv[0], kv[1], i, n_seq, T, heads)
            else:
                o = _attn_sample(q, kv[0], kv[1], i, cache_k, cache_v, n_seq, T, heads)
            x = _out_proj(o, w_o_sb, i, x, n_seq, T, mod, l, norm_g[l, 1])
        x, tails = _ffn(x, n_seq, T, ffn_prev, mod, l, norm_g[l, 2], w_ffn_up, w_ffn_conv, b_ffn_conv, w_ffn_down,
                        norm_g[l, 3])
        ffn_new.append(_last_rows(tails, n_seq))
    k_new, v_new = (a.reshape(n_odd, n_seq, T, heads, D // heads) for a in kv)
    return x.reshape(n_seq, T, D), k_new, v_new, jnp.stack(conv_a_new), jnp.stack(ffn_new), sgu_v_new


def kernel(x_prompt, x_sample, c_prompt, c_sample, cache_sb_k, cache_sb_v, state_conv_a, state_ffn_conv, w_mod, b_mod,
           norm_g, w_in_ab, w_conv_a, g_sgu, w_sgu, b_sgu, w_out_ab, w_qkv_sb, w_o_sb, w_ffn_up, w_ffn_conv,
           b_ffn_conv, w_ffn_down):
    bp, bs = x_prompt.shape[0], x_sample.shape[0]
    depth = norm_g.shape[0]
    heads = cache_sb_k.shape[3]

    mod = _modulation(jnp.concatenate([c_prompt, c_sample], axis=0), w_mod, b_mod)
    mod_p = mod[:, :bp].reshape(depth, bp, 1, -1)
    mod_s = mod[:, bp:].reshape(depth, bs, 1, -1)

    weights = (norm_g, w_in_ab.astype(BF16), w_conv_a, g_sgu, w_sgu, b_sgu, w_out_ab.astype(BF16),
               w_qkv_sb.astype(BF16), w_o_sb.astype(BF16), w_ffn_up.astype(BF16), w_ffn_conv, b_ffn_conv,
               w_ffn_down.astype(BF16))

    zeros_a = jnp.zeros((state_conv_a.shape[0], bp) + state_conv_a.shape[2:], F32)
    zeros_f = jnp.zeros((depth, bp) + state_ffn_conv.shape[2:], F32)
    y_s, k_s, v_s, conv_a_s, ffn_s, sgu_v = _trunk(x_sample, mod_s, cache_sb_k, cache_sb_v, state_conv_a,
                                                   state_ffn_conv, *weights, heads=heads, emit_vn=True)
    sample_out, (x_prompt, mod_p) = lax.optimization_barrier(
        ((y_s, k_s, v_s, conv_a_s, ffn_s, sgu_v), (x_prompt, mod_p)))
    y_s, k_s, v_s, conv_a_s, ffn_s, sgu_v = sample_out
    y_p, k_p, v_p, conv_a_p, ffn_p, _ = _trunk(x_prompt, mod_p, None, None, zeros_a, zeros_f, *weights,
                                               heads=heads, emit_vn=False)
    return (y_p, y_s, k_p, v_p, conv_a_p, ffn_p, k_s, v_s, conv_a_s, ffn_s, jnp.stack(sgu_v))
```

```python
import functools
import math

import jax
import jax.numpy as jnp
from jax import lax
from jax.experimental import pallas as pl
from jax.experimental.pallas import tpu as pltpu

F32 = jnp.float32
BF16 = jnp.bfloat16

EPS = 1e-6
CONV_TAPS = 3
GMLP_CAUSAL_BLOCK = 64
SUBLANES = 8
VMEM_LIMIT_BYTES = 56 * 1024 * 1024

ROWS_IN_PROJ = 1024
ROWS_MIX = 512
ROWS_OUT_PROJ = 512
ROWS_FFN = 512
COLS_IN_PROJ = 512
COLS_FFN = 512
COLS_MOD = 1024
ATTN_BLOCK = 256
ATTN_PROMPT_HEADS = 8
ATTN_SAMPLE_HEADS = 8

LOG2E = math.log2(math.e)
SB_Z_MAX = 126.0
SB_DONE = 150.0 / LOG2E


def _params(*semantics):
    return pltpu.CompilerParams(dimension_semantics=semantics, vmem_limit_bytes=VMEM_LIMIT_BYTES)


def _split_rows(n_seq, seq_len, rows):
    if seq_len >= rows:
        assert seq_len % rows == 0
        return 1, rows
    s = min(rows // seq_len, n_seq)
    assert n_seq % s == 0 and seq_len % SUBLANES == 0
    return s, seq_len


def _seq_block(i, S, R, T):
    return i // (T // R) if S == 1 else i


def _largest_divisor(n, candidates):
    for c in candidates:
        if n % c == 0:
            return c
    raise ValueError(f"no block size among {candidates} divides {n}")


def _rms(x, g):
    return x * lax.rsqrt(jnp.mean(x * x, axis=-1, keepdims=True) + EPS) * g


def _per_row(v, R):
    S, _, C = v.shape
    if S == 1:
        return v[0]
    return jnp.broadcast_to(v, (S, R, C)).reshape(S * R, C)


def _sigmoid(x):
    return 1.0 / (1.0 + jnp.exp(-x))


def _conv_taps(a, p0, p1, w, R):
    rt = lax.broadcasted_iota(jnp.int32, a.shape, 0) & (R - 1)
    s1 = jnp.where(rt == 0, p1, pltpu.roll(a, 1, 0))
    s2 = jnp.where(rt == 0, p0, jnp.where(rt == 1, p1, pltpu.roll(a, 2, 0)))
    return s2 * w[0:1] + s1 * w[1:2] + a * w[2:3]


def _causal_conv(a, p0, p1, w, R, S):
    if S > 1:
        return _conv_taps(a, p0, p1, w, R)
    body = pltpu.roll(a, 2, 0) * w[0:1] + pltpu.roll(a, 1, 0) * w[1:2] + a * w[2:3]
    head = _conv_taps(a[0:SUBLANES], p0, p1, w, R)
    return jnp.concatenate([head, body[SUBLANES:]], axis=0)


NORM_ROWS = 128


def _row_chunks(ref_sc, c, S, R, step):
    if S == 1:
        return ref_sc[0]
    k = step // R
    return _per_row(ref_sc[pl.ds(c * k, k)], R)


def _norm_mod_rows(x_ref, g, sc_ref, sh_ref, h_ref, S, R, straight_line=False):
    rows = S * R
    step = min(NORM_ROWS, rows)
    assert rows % step == 0 and (S == 1 or step % R == 0)

    def body(c, carry):
        _norm_mod_chunk(x_ref, g, sc_ref, sh_ref, h_ref, S, R, c)
        return carry

    _for_chunks(rows // step, body, straight_line)


def _norm_mod_chunk(x_ref, g, sc_ref, sh_ref, h_ref, S, R, c):
    step = min(NORM_ROWS, S * R)
    rs = pl.ds(c * step, step) if isinstance(c, int) else pl.ds(pl.multiple_of(c * step, step), step)
    h = _rms(x_ref[rs, :], g) * (1.0 + _row_chunks(sc_ref, c, S, R, step)) + _row_chunks(sh_ref, c, S, R, step)
    h_ref[rs, :] = h.astype(BF16)


def _residual_norm_rows(x_ref, y_ref, gate_ref, g, xo_ref, S, R, straight_line=False):
    rows = S * R
    step = min(NORM_ROWS, rows)
    assert rows % step == 0 and (S == 1 or step % R == 0)

    def body(c, carry):
        rs = pl.ds(c * step, step) if straight_line else pl.ds(pl.multiple_of(c * step, step), step)
        xo_ref[rs, :] = x_ref[rs, :] + _row_chunks(gate_ref, c, S, R, step) * _rms(y_ref[rs, :], g)
        return carry

    _for_chunks(rows // step, body, straight_line)


def _for_chunks(n, body, straight_line):
    if straight_line:
        for c in range(n):
            body(c, 0)
    else:
        lax.fori_loop(0, n, body, 0)


def _mod_kernel(c_ref, w_ref, b_ref, o_ref):
    c = c_ref[...]
    s = (c * _sigmoid(c)).astype(BF16)
    o_ref[...] = jnp.dot(s, w_ref[...].astype(BF16), preferred_element_type=F32) + b_ref[...]


def _modulation(c, w_mod, b_mod):
    L, D, N = w_mod.shape
    B = c.shape[0]
    tn = _largest_divisor(N, (COLS_MOD, 512, 256, 128))
    return pl.pallas_call(
        _mod_kernel,
        grid=(L, N // tn),
        in_specs=[pl.BlockSpec((B, D), lambda l, j: (0, 0)),
                  pl.BlockSpec((None, D, tn), lambda l, j: (l, 0, j)),
                  pl.BlockSpec((None, 1, tn), lambda l, j: (l, 0, j))],
        out_specs=pl.BlockSpec((None, B, tn), lambda l, j: (l, 0, j)),
        out_shape=jax.ShapeDtypeStruct((L, B, N), F32),
        compiler_params=_params("parallel", "parallel"),
        name="modulation",
    )(c, w_mod, b_mod.reshape(L, 1, N))


def _in_proj_kernel(x_ref, g_ref, sc_ref, sh_ref, w_ref, *rest, S, R, nj, blocks_per_out, n_out, out_scales,
                    fill_slots):
    outs, hs = rest[-2 - n_out:-2], rest[-2:]
    i, j = pl.program_id(0), pl.program_id(1)
    n_chunks = S * R // min(NORM_ROWS, S * R)
    per_step = next(d for d in range(1, n_chunks + 1) if n_chunks % d == 0 and d * (nj - 1) >= n_chunks)
    norm_steps = n_chunks // per_step

    def store(k, y):
        val = (y if out_scales[k] == 1.0 else y * out_scales[k]).astype(outs[k].dtype)
        if k > 0 and fill_slots:
            for s in range(outs[k].shape[0]):
                outs[k][s] = val
        else:
            outs[k][...] = val

    def project(parity):
        y = jnp.dot(hs[parity][...], w_ref[...], preferred_element_type=F32)
        if n_out == 1:
            store(0, y)
        else:
            for k in range(n_out):
                @pl.when(j // blocks_per_out == k)
                def _(k=k):
                    store(k, y)

    @pl.when((i == 0) & (j == 0))
    def _():
        _norm_mod_rows(x_ref, g_ref[...], sc_ref, sh_ref, hs[0], S, R)

    for parity in range(2):
        @pl.when((i % 2 == parity) & ((j == 0) | (j > norm_steps)))
        def _(parity=parity):
            project(parity)

        @pl.when((i % 2 == parity) & (j > 0) & (j <= norm_steps))
        def _(parity=parity):
            for t in range(per_step):
                _norm_mod_chunk(x_ref, g_ref[...], sc_ref, sh_ref, hs[1 - parity], S, R, (j - 1) * per_step + t)
            project(parity)


def _in_proj(x, n_seq, T, g, mod, layer, sc_chunk, sh_chunk, w, wi, out_dtypes, out_scales, stacks=None, slot=0,
             n_slots=1):
    N, D = x.shape
    n_total = w.shape[2]
    n_out = len(out_dtypes)
    S, R = _split_rows(n_seq, T, ROWS_IN_PROJ)
    rows = S * R
    nb = N // rows
    width = n_total // n_out
    tn = _largest_divisor(width, (COLS_IN_PROJ, 256, 128))
    bpo = width // tn
    nj = n_total // tn

    def norm_block(i, j):
        return jnp.where((i == 0) & (j == 0), 0, jnp.minimum(i + 1, nb - 1))

    def mod_spec(chunk):
        return pl.BlockSpec((None, S, 1, D), lambda i, j: (layer, _seq_block(norm_block(i, j), S, R, T), 0, chunk))

    def col(j, k):
        return jnp.clip(j - k * bpo, 0, bpo - 1)

    out_specs = [pl.BlockSpec((rows, tn), lambda i, j: (i, col(j, 0)))]
    out_shape = [jax.ShapeDtypeStruct((N, width), out_dtypes[0])]
    for k in range(1, n_out):
        if stacks is None:
            spec = pl.BlockSpec((n_slots, rows, tn), functools.partial(lambda i, j, k: (0, i, col(j, k)), k=k))
        else:
            spec = pl.BlockSpec((None, rows, tn), functools.partial(lambda i, j, k: (slot, i, col(j, k)), k=k))
        out_specs.append(spec)
        out_shape.append(jax.ShapeDtypeStruct((n_slots, N, width), out_dtypes[k]))
    in_specs = [pl.BlockSpec((rows, D), lambda i, j: (norm_block(i, j), 0)),
                pl.BlockSpec((1, D), lambda i, j: (0, 0)),
                mod_spec(sc_chunk), mod_spec(sh_chunk),
                pl.BlockSpec((None, D, tn), lambda i, j: (wi, 0, j))]
    args = [x, g.reshape(1, D), mod, mod, w]
    aliases = {}
    if stacks is not None:
        for k, st in enumerate(stacks):
            aliases[len(args)] = k + 1
            in_specs.append(pl.BlockSpec(memory_space=pl.ANY))
            args.append(st)
    return pl.pallas_call(
        functools.partial(_in_proj_kernel, S=S, R=R, nj=nj, blocks_per_out=bpo, n_out=n_out,
                          out_scales=tuple(out_scales), fill_slots=stacks is None),
        grid=(nb, nj),
        in_specs=in_specs, out_specs=out_specs, out_shape=out_shape,
        scratch_shapes=[pltpu.VMEM((rows, D), BF16), pltpu.VMEM((rows, D), BF16)],
        input_output_aliases=aliases,
        compiler_params=_params("arbitrary", "arbitrary"),
        name="in_proj",
    )(*args)


def _sgu_matrices(ws_ref, bs_ref, g, T, chunk, whole_chunks):
    w = ws_ref[g]
    b = jnp.broadcast_to(bs_ref[g], (chunk, chunk))
    ii = lax.broadcasted_iota(jnp.int32, (chunk, chunk), 0)
    jj = lax.broadcasted_iota(jnp.int32, (chunk, chunk), 1)
    if whole_chunks:
        return jnp.where(jj // GMLP_CAUSAL_BLOCK <= ii // GMLP_CAUSAL_BLOCK, w, 0.0), b
    w_tl = jnp.where((ii < T) & (jj < T), w, 0.0)
    b_tl = jnp.where(ii < T, b, 0.0)
    wm, bm = w_tl, b_tl
    for k in range(1, chunk // T):
        wm = wm + pltpu.roll(pltpu.roll(w_tl, k * T, 0), k * T, 1)
        bm = bm + pltpu.roll(b_tl, k * T, 0)
    return wm, bm


def _mix_even_kernel(*refs, S, R, T, chunk, groups, halo, emit_vn):
    refs = list(refs)
    xa_ref, gb_ref, gc_ref, u_ref, v_ref = refs[:5]
    del refs[:5]
    if halo:
        hxa_ref, hgc_ref = refs[:2]
        del refs[:2]
    st_ref, x_ref, g1_ref, wc_ref, gs_ref, ws_ref, bs_ref, wo_ref, gn_ref = refs[:9]
    del refs[:9]
    xo_ref, tail_ref = refs[:2]
    del refs[:2]
    if emit_vn:
        vn_out_ref = refs.pop(0)
    cat_ref, vn_ref = refs

    i = pl.program_id(0)
    rows = S * R
    C = xa_ref.shape[1]

    ca = gc_ref[...] * xa_ref[...]
    st = st_ref[...]
    p0, p1 = st[:, 0:1, :], st[:, 1:2, :]
    if halo:
        hca = hgc_ref[...] * hxa_ref[...]
        first = (i % (T // R)) == 0
        p0 = jnp.where(first, p0, hca[SUBLANES - 2:SUBLANES - 1][None])
        p1 = jnp.where(first, p1, hca[SUBLANES - 1:SUBLANES][None])
    conv = _causal_conv(ca, _per_row(p0, R), _per_row(p1, R), wc_ref[...], R, S)
    cat_ref[:, 0:C] = (gb_ref[...] * conv).astype(BF16)
    tail_ref[...] = ca.reshape(S, R, C)[:, R - SUBLANES:R, :]

    vn = _rms(v_ref[...], gs_ref[...])
    if emit_vn:
        vn_out_ref[...] = vn
    vn_ref[...] = vn.astype(BF16)
    gd = v_ref.shape[1] // groups
    for g in range(groups):
        wm, bm = _sgu_matrices(ws_ref, bs_ref, g, T, chunk, whole_chunks=(R % chunk == 0))
        wm = wm.astype(BF16)
        cs = slice(g * gd, (g + 1) * gd)
        for c in range(rows // chunk):
            rs = slice(c * chunk, (c + 1) * chunk)
            mixed = jnp.dot(wm, vn_ref[rs, cs], preferred_element_type=F32) + bm
            cat_ref[rs, C + g * gd:C + (g + 1) * gd] = (u_ref[rs, cs] * mixed).astype(BF16)

    y = jnp.dot(cat_ref[...], wo_ref[...], preferred_element_type=F32)
    xo_ref[...] = x_ref[...] + _per_row(g1_ref[...], R) * _rms(y, gn_ref[...])


def _mix_even(z, x, n_seq, T, state, mod, layer, wi, w_conv, g_sgu, w_sgu, b_sgu, w_out, g_norm, emit_vn):
    N, D = x.shape
    C = w_conv.shape[2]
    groups, chunk = w_sgu.shape[1], w_sgu.shape[2]
    Cb = g_sgu.shape[1]
    S, R = _split_rows(n_seq, T, ROWS_MIX // 2 if emit_vn else ROWS_MIX)
    rows = S * R
    halo = R < T
    assert rows % chunk == 0 and (R % chunk == 0 or (chunk % T == 0 and T <= GMLP_CAUSAL_BLOCK))
    assert z.shape[1] == 3 * C + 2 * Cb and C == Cb

    in_specs = [pl.BlockSpec((rows, C), functools.partial(lambda i, k: (i, k), k=k)) for k in range(5)]
    args = [z] * 5
    if halo:
        per = R // SUBLANES
        in_specs += [pl.BlockSpec((SUBLANES, C), functools.partial(
            lambda i, k: (jnp.maximum(i * per - 1, 0), k), k=k)) for k in (0, 2)]
        args += [z, z]
    in_specs += [
        pl.BlockSpec((None, S, CONV_TAPS - 1, C), lambda i: (wi, _seq_block(i, S, R, T), 0, 0)),
        pl.BlockSpec((rows, D), lambda i: (i, 0)),
        pl.BlockSpec((None, S, 1, D), lambda i: (layer, _seq_block(i, S, R, T), 0, 2)),
        pl.BlockSpec((None, CONV_TAPS, C), lambda i: (wi, 0, 0)),
        pl.BlockSpec((None, 1, Cb), lambda i: (wi, 0, 0)),
        pl.BlockSpec((None, groups, chunk, chunk), lambda i: (wi, 0, 0, 0)),
        pl.BlockSpec((None, groups, chunk, 1), lambda i: (wi, 0, 0, 0)),
        pl.BlockSpec((None, C + Cb, D), lambda i: (wi, 0, 0), pipeline_mode=pl.Buffered(1)),
        pl.BlockSpec((1, D), lambda i: (0, 0)),
    ]
    n_even = w_conv.shape[0]
    args += [state, x, mod, w_conv, g_sgu.reshape(n_even, 1, Cb), w_sgu, b_sgu.reshape(n_even, groups, chunk, 1),
             w_out, g_norm.reshape(1, D)]
    out_specs = [pl.BlockSpec((rows, D), lambda i: (i, 0)),
                 pl.BlockSpec((S, SUBLANES, C), lambda i: (i, 0, 0))]
    out_shape = [jax.ShapeDtypeStruct((N, D), F32),
                 jax.ShapeDtypeStruct((N // rows * S, SUBLANES, C), F32)]
    if emit_vn:
        out_specs.append(pl.BlockSpec((rows, Cb), lambda i: (i, 0)))
        out_shape.append(jax.ShapeDtypeStruct((N, Cb), F32))
    return pl.pallas_call(
        functools.partial(_mix_even_kernel, S=S, R=R, T=T, chunk=chunk, groups=groups, halo=halo, emit_vn=emit_vn),
        grid=(N // rows,),
        in_specs=in_specs, out_specs=out_specs, out_shape=out_shape,
        scratch_shapes=[pltpu.VMEM((rows, C + Cb), BF16), pltpu.VMEM((rows, Cb), BF16)],
        compiler_params=_params("parallel"),
        name="mix_even",
    )(*args)


def _upper_inclusive(n):
    r = lax.broadcasted_iota(jnp.int32, (n, n), 0)
    c = lax.broadcasted_iota(jnp.int32, (n, n), 1)
    return jnp.where(r >= c, 1.0, 0.0).astype(BF16)


def _strictly_earlier(n):
    t = lax.broadcasted_iota(jnp.int32, (n, n), 0)
    s = lax.broadcasted_iota(jnp.int32, (n, n), 1)
    return s < t


def _sb_blocks(qs, ks, vs, u_inc, cs, accs, mask):
    tq = qs[0].shape[0]
    zs, parts = [], []
    for q, k in zip(qs, ks):
        z = jnp.minimum(lax.dot_general(q, k, (((1,), (1,)), ((), ())), preferred_element_type=F32), SB_Z_MAX)
        fail = jnp.log(1.0 + jnp.exp2(z))
        if mask is not None:
            fail = jnp.where(mask, fail, 0.0)
        parts.append(fail.astype(BF16))
        zs.append(z)
    sums = jnp.dot(jnp.concatenate(parts, axis=0), u_inc, preferred_element_type=F32)
    cs_out, accs_out = [], []
    for h, (z, v, c, acc) in enumerate(zip(zs, vs, cs, accs)):
        incl = sums[h * tq:(h + 1) * tq]
        w = jnp.exp2(z - (incl + c) * LOG2E)
        if mask is not None:
            w = jnp.where(mask, w, 0.0)
        accs_out.append(acc + jnp.dot(w.astype(BF16), v, preferred_element_type=F32))
        cs_out.append(c + incl[:, 0:1])
    return tuple(cs_out), tuple(accs_out)


def _unfinished(cs):
    return (jnp.min(functools.reduce(jnp.minimum, cs)) < SB_DONE).astype(jnp.int32)


def _attn_prompt_kernel(q_ref, k_ref, v_ref, o_ref, kb_ref, vb_ref, u_ref, *, tq, heads, dh):
    qi = pl.program_id(2)

    @pl.when(qi == 0)
    def _():
        kb_ref[...] = k_ref[...].astype(BF16)
        vb_ref[...] = v_ref[...].astype(BF16)
        u_ref[...] = _upper_inclusive(tq)

    u = u_ref[...]
    cols = [slice(h * dh, (h + 1) * dh) for h in range(heads)]
    qs = [q_ref[:, cs] for cs in cols]

    def sweep(n, cs, accs, mask):
        rows = pl.ds(pl.multiple_of((qi - n) * tq, tq), tq)
        return _sb_blocks(qs, [kb_ref[rows, cs_] for cs_ in cols], [vb_ref[rows, cs_] for cs_ in cols], u, cs, accs,
                          mask)

    def diagonal():
        return sweep(0, (jnp.zeros((tq, 1), F32),) * heads, (jnp.zeros((tq, dh), F32),) * heads,
                     _strictly_earlier(tq))

    def store(accs):
        for h in range(heads):
            o_ref[:, cols[h]] = accs[h].astype(o_ref.dtype)

    def body(carry):
        n, _, cs, accs = carry
        cs, accs = sweep(n, cs, accs, None)
        return n + 1, _unfinished(cs), cs, accs

    @pl.when(qi == 0)
    def _():
        store(diagonal()[1])

    @pl.when(qi > 0)
    def _():
        cs, accs = sweep(1, *diagonal(), None)
        _, _, _, accs = lax.while_loop(lambda carry: (carry[0] <= qi) & (carry[1] > 0), body,
                                       (jnp.int32(2), _unfinished(cs), cs, accs))
        store(accs)


def _attn_prompt(q, k_stack, v_stack, slot, n_seq, T, heads):
    N, D = q.shape
    dh = D // heads
    tq = min(ATTN_BLOCK, T)
    hp = _largest_divisor(heads, (ATTN_PROMPT_HEADS, 1))
    assert T % tq == 0
    n_slots = k_stack.shape[0]
    kv_spec = pl.BlockSpec((None, None, T, hp * dh), lambda b, h, i: (slot, b, 0, h))
    o = pl.pallas_call(
        functools.partial(_attn_prompt_kernel, tq=tq, heads=hp, dh=dh),
        grid=(n_seq, heads // hp, T // tq),
        in_specs=[pl.BlockSpec((None, tq, hp * dh), lambda b, h, i: (b, i, h)), kv_spec, kv_spec],
        out_specs=pl.BlockSpec((None, tq, hp * dh), lambda b, h, i: (b, i, h)),
        out_shape=jax.ShapeDtypeStruct((n_seq, T, D), BF16),
        scratch_shapes=[pltpu.VMEM((T, hp * dh), BF16), pltpu.VMEM((T, hp * dh), BF16),
                        pltpu.VMEM((tq, tq), BF16)],
        compiler_params=_params("parallel", "parallel", "arbitrary"),
        name="attn_prompt",
    )(q.reshape(n_seq, T, D), k_stack.reshape(n_slots, n_seq, T, D), v_stack.reshape(n_slots, n_seq, T, D))
    return o.reshape(N, D)


def _attn_sample_kernel(q_ref, kn_ref, vn_ref, ck_last_ref, cv_last_ref, ck_hbm, cv_hbm, o_ref, kbuf, vbuf, sem,
                        *, T, P, tk, heads, all_heads, dh, slot):
    u_past = _upper_inclusive(tk)
    cols = [slice(h * dh, (h + 1) * dh) for h in range(heads)]
    qs = [q_ref[:, cs] for cs in cols]

    cs, accs = _sb_blocks(qs, [kn_ref[:, c].astype(BF16) for c in cols], [vn_ref[:, c].astype(BF16) for c in cols],
                          _upper_inclusive(T), (jnp.zeros((T, 1), F32),) * heads,
                          (jnp.zeros((T, dh), F32),) * heads, _strictly_earlier(T))

    b, h0 = pl.program_id(0), pl.program_id(1) * heads

    def head_rows(ref):
        return [ref[pl.ds(h0 + h, tk, stride=all_heads), :].astype(BF16) for h in range(heads)]

    cs, accs = _sb_blocks(qs, head_rows(ck_last_ref), head_rows(cv_last_ref), u_past, cs, accs, None)

    def body(carry):
        n, _, cs, accs = carry
        rows = pl.ds(pl.multiple_of((P - (n + 1) * tk) * all_heads, tk * all_heads), tk * all_heads)
        copies = [pltpu.make_async_copy(src.at[slot, b, rows, :], dst, sem.at[s])
                  for s, (src, dst) in enumerate(((ck_hbm, kbuf), (cv_hbm, vbuf)))]
        for c in copies:
            c.start()
        for c in copies:
            c.wait()
        cs, accs = _sb_blocks(qs, head_rows(kbuf), head_rows(vbuf), u_past, cs, accs, None)
        return n + 1, _unfinished(cs), cs, accs

    _, _, _, accs = lax.while_loop(lambda carry: (carry[0] < P // tk) & (carry[1] > 0), body,
                                   (jnp.int32(1), _unfinished(cs), cs, accs))
    for h in range(heads):
        o_ref[:, cols[h]] = accs[h].astype(o_ref.dtype)


def _attn_sample(q, k_stack, v_stack, slot, cache_k, cache_v, n_seq, T, heads):
    N, D = q.shape
    dh = D // heads
    P = cache_k.shape[2]
    tk = _largest_divisor(P, (ATTN_BLOCK, 128))
    hg = _largest_divisor(heads, (ATTN_SAMPLE_HEADS, heads))
    n_slots = k_stack.shape[0]
    q_spec = pl.BlockSpec((None, T, hg * dh), lambda b, g: (b, 0, g))
    new_spec = pl.BlockSpec((None, None, T, hg * dh), lambda b, g: (slot, b, 0, g))
    last_spec = pl.BlockSpec((None, None, tk * heads, dh), lambda b, g: (slot, b, P // tk - 1, 0))
    hbm_spec = pl.BlockSpec(memory_space=pl.ANY)
    past_shape = cache_k.shape[:2] + (P * heads, dh)
    cache_k, cache_v = cache_k.reshape(past_shape), cache_v.reshape(past_shape)
    o = pl.pallas_call(
        functools.partial(_attn_sample_kernel, T=T, P=P, tk=tk, heads=hg, all_heads=heads, dh=dh, slot=slot),
        grid=(n_seq, heads // hg),
        in_specs=[q_spec, new_spec, new_spec, last_spec, last_spec, hbm_spec, hbm_spec],
        out_specs=q_spec,
        out_shape=jax.ShapeDtypeStruct((n_seq, T, D), BF16),
        scratch_shapes=[pltpu.VMEM((tk * heads, dh), F32), pltpu.VMEM((tk * heads, dh), F32),
                        pltpu.SemaphoreType.DMA((2,))],
        compiler_params=_params("parallel", "arbitrary"),
        name="attn_sample",
    )(q.reshape(n_seq, T, D), k_stack.reshape(n_slots, n_seq, T, D), v_stack.reshape(n_slots, n_seq, T, D),
      cache_k, cache_v, cache_k, cache_v)
    return o.reshape(N, D)


def _out_proj_kernel(o_ref, w_ref, x_ref, g1_ref, gn_ref, xo_ref, *, R):
    y = jnp.dot(o_ref[...], w_ref[...], preferred_element_type=F32)
    xo_ref[...] = x_ref[...] + _per_row(g1_ref[...], R) * _rms(y, gn_ref[...])


def _out_proj(o, w, wi, x, n_seq, T, mod, layer, g_norm):
    N, D = x.shape
    S, R = _split_rows(n_seq, T, ROWS_OUT_PROJ)
    rows = S * R
    return pl.pallas_call(
        functools.partial(_out_proj_kernel, R=R),
        grid=(N // rows,),
        in_specs=[pl.BlockSpec((rows, D), lambda i: (i, 0)),
                  pl.BlockSpec((None, D, D), lambda i: (wi, 0, 0)),
                  pl.BlockSpec((rows, D), lambda i: (i, 0)),
                  pl.BlockSpec((None, S, 1, D), lambda i: (layer, _seq_block(i, S, R, T), 0, 2)),
                  pl.BlockSpec((1, D), lambda i: (0, 0))],
        out_specs=pl.BlockSpec((rows, D), lambda i: (i, 0)),
        out_shape=jax.ShapeDtypeStruct((N, D), F32),
        compiler_params=_params("parallel"),
        name="out_proj",
    )(o, w, x, mod, g_norm.reshape(1, D))


def _ffn_kernel(x_ref, xp_ref, gn_in_ref, sc_ref, sh_ref, wa_ref, wg_ref, wc_ref, bc_ref, wd_ref, st_ref, g2_ref,
                gn_out_ref, xo_ref, tail_ref, h_ref, acc_ref, act0_ref, act1_ref, carry_ref, *, S, R, T, nf, nb):
    i, j = pl.program_id(0), pl.program_id(1)
    acts = (act0_ref, act1_ref)
    last_act = acts[(nf - 1) % 2]

    def up_gate(dst_ref):
        h = h_ref[...]
        a = jnp.dot(h, wa_ref[...], preferred_element_type=F32)
        gate = jnp.dot(h, wg_ref[...], preferred_element_type=F32)
        tail = a.reshape(S, R, -1)[:, R - SUBLANES:R, :]
        tail_ref[...] = tail
        p0, p1 = st_ref[:, 0:1, :], st_ref[:, 1:2, :]
        if R < T:
            first = (i % (T // R)) == 0
            prev = carry_ref[j]
            p0 = jnp.where(first, p0, prev[SUBLANES - 2:SUBLANES - 1][None])
            p1 = jnp.where(first, p1, prev[SUBLANES - 1:SUBLANES][None])
            carry_ref[j] = tail[0]
        half = 0.5 * (_causal_conv(a, _per_row(p0, R), _per_row(p1, R), wc_ref[...], R, S) + bc_ref[...])
        dst_ref[...] = ((half + half * jnp.tanh(half)) * gate).astype(BF16)

    def down(src_ref):
        acc_ref[...] += jnp.dot(src_ref[...], wd_ref[...], preferred_element_type=F32)

    def finish_previous():
        down(last_act)
        _residual_norm_rows(xp_ref, acc_ref, g2_ref, gn_out_ref[...], xo_ref, S, R, straight_line=True)

    def start_block():
        _norm_mod_rows(x_ref, gn_in_ref[...], sc_ref, sh_ref, h_ref, S, R, straight_line=True)
        acc_ref[...] = jnp.zeros_like(acc_ref)
        up_gate(acts[0])

    @pl.when((j == 0) & (i == 0))
    def _():
        if R < T:
            carry_ref[...] = jnp.zeros_like(carry_ref)
        start_block()

    @pl.when((j == 0) & (i > 0) & (i < nb))
    def _():
        finish_previous()
        start_block()

    @pl.when((j == 0) & (i == nb))
    def _():
        finish_previous()

    for parity in range(2):
        @pl.when((j > 0) & (i < nb) & (j % 2 == parity))
        def _(parity=parity):
            up_gate(acts[parity])
            down(acts[1 - parity])


def _ffn(x, n_seq, T, state, mod, layer, g_in, w_up, w_conv, b_conv, w_down, g_out):
    N, D = x.shape
    L, F = w_down.shape[0], w_down.shape[1]
    S, R = _split_rows(n_seq, T, ROWS_FFN)
    rows = S * R
    tf = _largest_divisor(F, (COLS_FFN, 256, 128))
    nf = F // tf

    nb = N // rows

    def cur(i):
        return jnp.minimum(i, nb - 1)

    def prev(i):
        return jnp.maximum(i - 1, 0)

    def up_chunk(i, j):
        return jnp.where(i < nb, j, nf - 1)

    def down_chunk(i, j):
        return jnp.where((i == nb) & (j > 0), nf - 1, (j + nf - 1) % nf)

    def mod_spec(chunk, blk):
        return pl.BlockSpec((None, S, 1, D), lambda i, j: (layer, _seq_block(blk(i), S, R, T), 0, chunk))

    return pl.pallas_call(
        functools.partial(_ffn_kernel, S=S, R=R, T=T, nf=nf, nb=nb),
        grid=(nb + 1, nf),
        in_specs=[pl.BlockSpec((rows, D), lambda i, j: (cur(i), 0)),
                  pl.BlockSpec((rows, D), lambda i, j: (prev(i), 0)),
                  pl.BlockSpec((1, D), lambda i, j: (0, 0)),
                  mod_spec(4, cur), mod_spec(3, cur),
                  pl.BlockSpec((None, D, tf), lambda i, j: (layer, 0, up_chunk(i, j))),
                  pl.BlockSpec((None, D, tf), lambda i, j: (layer, 0, nf + up_chunk(i, j))),
                  pl.BlockSpec((None, CONV_TAPS, tf), lambda i, j: (layer, 0, up_chunk(i, j))),
                  pl.BlockSpec((None, 1, tf), lambda i, j: (layer, 0, up_chunk(i, j))),
                  pl.BlockSpec((None, tf, D), lambda i, j: (layer, down_chunk(i, j), 0)),
                  pl.BlockSpec((None, S, CONV_TAPS - 1, tf),
                               lambda i, j: (layer, _seq_block(cur(i), S, R, T), 0, up_chunk(i, j))),
                  mod_spec(5, prev),
                  pl.BlockSpec((1, D), lambda i, j: (0, 0))],
        out_specs=[pl.BlockSpec((rows, D), lambda i, j: (prev(i), 0)),
                   pl.BlockSpec((S, SUBLANES, tf), lambda i, j: (cur(i), 0, up_chunk(i, j)))],
        out_shape=[jax.ShapeDtypeStruct((N, D), F32),
                   jax.ShapeDtypeStruct((N // rows * S, SUBLANES, F), F32)],
        scratch_shapes=[pltpu.VMEM((rows, D), BF16), pltpu.VMEM((rows, D), F32),
                        pltpu.VMEM((rows, tf), BF16), pltpu.VMEM((rows, tf), BF16),
                        pltpu.VMEM((nf, SUBLANES, tf), F32)],
        compiler_params=_params("arbitrary", "arbitrary"),
        name="ffn",
    )(x, x, g_in.reshape(1, D), mod, mod, w_up, w_up, w_conv, b_conv.reshape(L, 1, F), w_down, state, mod,
      g_out.reshape(1, D))


def _last_rows(tails, n_seq):
    t = tails.reshape(n_seq, tails.shape[0] // n_seq, SUBLANES, tails.shape[2])
    return t[:, -1, SUBLANES - (CONV_TAPS - 1):, :]


def _trunk(x, mod, cache_k, cache_v, conv_a_prev, ffn_prev, norm_g, w_in_ab, w_conv_a, g_sgu, w_sgu, b_sgu,
           w_out_ab, w_qkv_sb, w_o_sb, w_ffn_up, w_ffn_conv, b_ffn_conv, w_ffn_down, heads, emit_vn):
    n_seq, T, D = x.shape
    x = x.reshape(n_seq * T, D)
    depth = norm_g.shape[0]
    n_odd = w_qkv_sb.shape[0]
    q_scale = (D // heads) ** -0.5 * LOG2E
    kv = None
    conv_a_new, ffn_new, sgu_v_new = [], [], []
    for l in range(depth):
        i = l // 2
        if l % 2 == 0:
            (z,) = _in_proj(x, n_seq, T, norm_g[l, 0], mod, l, 1, 0, w_in_ab, i, (F32,), (1.0,))
            res = _mix_even(z, x, n_seq, T, conv_a_prev, mod, l, i, w_conv_a, g_sgu, w_sgu, b_sgu, w_out_ab,
                            norm_g[l, 1], emit_vn)
            x = res[0]
            conv_a_new.append(_last_rows(res[1], n_seq))
            if emit_vn:
                sgu_v_new.append(res[2].reshape(n_seq, T, -1))
        else:
            q, *kv = _in_proj(x, n_seq, T, norm_g[l, 0], mod, l, 1, 0, w_qkv_sb, i, (BF16, F32, F32),
                              (q_scale, 1.0, 1.0), stacks=kv, slot=i, n_slots=n_odd)
            if cache_k is None:
                o = _attn_prompt(q, kv[0], kv[1], i, n_seq, T, heads)
            else:
                o = _attn_sample(q, kv[0], kv[1], i, cache_k, cache_v, n_seq, T, heads)
            x = _out_proj(o, w_o_sb, i, x, n_seq, T, mod, l, norm_g[l, 1])
        x, tails = _ffn(x, n_seq, T, ffn_prev, mod, l, norm_g[l, 2], w_ffn_up, w_ffn_conv, b_ffn_conv, w_ffn_down,
                        norm_g[l, 3])
        ffn_new.append(_last_rows(tails, n_seq))
    k_new, v_new = (a.reshape(n_odd, n_seq, T, heads, D // heads) for a in kv)
    return x.reshape(n_seq, T, D), k_new, v_new, jnp.stack(conv_a_new), jnp.stack(ffn_new), sgu_v_new


def kernel(x_prompt, x_sample, c_prompt, c_sample, cache_sb_k, cache_sb_v, state_conv_a, state_ffn_conv, w_mod, b_mod,
           norm_g, w_in_ab, w_conv_a, g_sgu, w_sgu, b_sgu, w_out_ab, w_qkv_sb, w_o_sb, w_ffn_up, w_ffn_conv,
           b_ffn_conv, w_ffn_down):
    bp, bs = x_prompt.shape[0], x_sample.shape[0]
    depth = norm_g.shape[0]
    heads = cache_sb_k.shape[3]

    mod = _modulation(jnp.concatenate([c_prompt, c_sample], axis=0), w_mod, b_mod)
    mod_p = mod[:, :bp].reshape(depth, bp, 1, -1)
    mod_s = mod[:, bp:].reshape(depth, bs, 1, -1)

    weights = (norm_g, w_in_ab.astype(BF16), w_conv_a, g_sgu, w_sgu, b_sgu, w_out_ab.astype(BF16),
               w_qkv_sb.astype(BF16), w_o_sb.astype(BF16), w_ffn_up.astype(BF16), w_ffn_conv, b_ffn_conv,
               w_ffn_down.astype(BF16))

    zeros_a = jnp.zeros((state_conv_a.shape[0], bp) + state_conv_a.shape[2:], F32)
    zeros_f = jnp.zeros((depth, bp) + state_ffn_conv.shape[2:], F32)
    y_s, k_s, v_s, conv_a_s, ffn_s, sgu_v = _trunk(x_sample, mod_s, cache_sb_k, cache_sb_v, state_conv_a,
                                                   state_ffn_conv, *weights, heads=heads, emit_vn=True)
    sample_out, (x_prompt, mod_p) = lax.optimization_barrier(
        ((y_s, k_s, v_s, conv_a_s, ffn_s, sgu_v), (x_prompt, mod_p)))
    y_s, k_s, v_s, conv_a_s, ffn_s, sgu_v = sample_out
    y_p, k_p, v_p, conv_a_p, ffn_p, _ = _trunk(x_prompt, mod_p, None, None, zeros_a, zeros_f, *weights,
                                               heads=heads, emit_vn=False)
    return (y_p, y_s, k_p, v_p, conv_a_p, ffn_p, k_s, v_s, conv_a_s, ffn_s, jnp.stack(sgu_v))
```

```python
import functools
import math

import jax
import jax.numpy as jnp
from jax import lax
from jax.experimental import pallas as pl
from jax.experimental.pallas import tpu as pltpu

F32 = jnp.float32
BF16 = jnp.bfloat16

EPS = 1e-6
CONV_TAPS = 3
GMLP_CAUSAL_BLOCK = 64
SUBLANES = 8
VMEM_LIMIT_BYTES = 58 * 1024 * 1024

ROWS_IN_PROJ = 1024
ROWS_MIX = 512
ROWS_OUT_PROJ = 512
ROWS_FFN = 512
COLS_IN_PROJ = 512
COLS_FFN = 512
COLS_MOD = 1024
ATTN_BLOCK = 256
ATTN_PROMPT_HEADS = 8
ATTN_SAMPLE_HEADS = 8

LOG2E = math.log2(math.e)
SB_Z_MAX = 126.0
SB_DONE = 150.0 / LOG2E


def _params(*semantics):
    return pltpu.CompilerParams(dimension_semantics=semantics, vmem_limit_bytes=VMEM_LIMIT_BYTES)


def _split_rows(n_seq, seq_len, rows):
    if seq_len >= rows:
        assert seq_len % rows == 0
        return 1, rows
    s = min(rows // seq_len, n_seq)
    assert n_seq % s == 0 and seq_len % SUBLANES == 0
    return s, seq_len


def _seq_block(i, S, R, T):
    return i // (T // R) if S == 1 else i


def _largest_divisor(n, candidates):
    for c in candidates:
        if n % c == 0:
            return c
    raise ValueError(f"no block size among {candidates} divides {n}")


def _rms(x, g):
    return x * lax.rsqrt(jnp.mean(x * x, axis=-1, keepdims=True) + EPS) * g


def _per_row(v, R):
    S, _, C = v.shape
    if S == 1:
        return v[0]
    return jnp.broadcast_to(v, (S, R, C)).reshape(S * R, C)


def _sigmoid(x):
    return 1.0 / (1.0 + jnp.exp(-x))


def _conv_taps(a, p0, p1, w, R):
    rt = lax.broadcasted_iota(jnp.int32, a.shape, 0) & (R - 1)
    s1 = jnp.where(rt == 0, p1, pltpu.roll(a, 1, 0))
    s2 = jnp.where(rt == 0, p0, jnp.where(rt == 1, p1, pltpu.roll(a, 2, 0)))
    return s2 * w[0:1] + s1 * w[1:2] + a * w[2:3]


def _causal_conv(a, p0, p1, w, R, S):
    if S > 1:
        return _conv_taps(a, p0, p1, w, R)
    body = pltpu.roll(a, 2, 0) * w[0:1] + pltpu.roll(a, 1, 0) * w[1:2] + a * w[2:3]
    head = _conv_taps(a[0:SUBLANES], p0, p1, w, R)
    return jnp.concatenate([head, body[SUBLANES:]], axis=0)


NORM_ROWS = 128


def _row_chunks(ref_sc, c, S, R, step):
    if S == 1:
        return ref_sc[0]
    k = step // R
    return _per_row(ref_sc[pl.ds(c * k, k)], R)


def _norm_mod_rows(x_ref, g, sc_ref, sh_ref, h_ref, S, R, straight_line=False):
    rows = S * R
    step = min(NORM_ROWS, rows)
    assert rows % step == 0 and (S == 1 or step % R == 0)

    def body(c, carry):
        _norm_mod_chunk(x_ref, g, sc_ref, sh_ref, h_ref, S, R, c)
        return carry

    _for_chunks(rows // step, body, straight_line)


def _norm_mod_chunk(x_ref, g, sc_ref, sh_ref, h_ref, S, R, c):
    step = min(NORM_ROWS, S * R)
    rs = pl.ds(c * step, step) if isinstance(c, int) else pl.ds(pl.multiple_of(c * step, step), step)
    h = _rms(x_ref[rs, :], g) * (1.0 + _row_chunks(sc_ref, c, S, R, step)) + _row_chunks(sh_ref, c, S, R, step)
    h_ref[rs, :] = h.astype(BF16)


def _residual_norm_rows(x_ref, y_ref, gate_ref, g, xo_ref, S, R, straight_line=False):
    rows = S * R
    step = min(NORM_ROWS, rows)
    assert rows % step == 0 and (S == 1 or step % R == 0)

    def body(c, carry):
        rs = pl.ds(c * step, step) if straight_line else pl.ds(pl.multiple_of(c * step, step), step)
        xo_ref[rs, :] = x_ref[rs, :] + _row_chunks(gate_ref, c, S, R, step) * _rms(y_ref[rs, :], g)
        return carry

    _for_chunks(rows // step, body, straight_line)


def _for_chunks(n, body, straight_line):
    if straight_line:
        for c in range(n):
            body(c, 0)
    else:
        lax.fori_loop(0, n, body, 0)


def _mod_kernel(c_ref, w_ref, b_ref, o_ref):
    c = c_ref[...]
    s = (c * _sigmoid(c)).astype(BF16)
    o_ref[...] = jnp.dot(s, w_ref[...].astype(BF16), preferred_element_type=F32) + b_ref[...]


def _modulation(c, w_mod, b_mod):
    L, D, N = w_mod.shape
    B = c.shape[0]
    tn = _largest_divisor(N, (COLS_MOD, 512, 256, 128))
    return pl.pallas_call(
        _mod_kernel,
        grid=(L, N // tn),
        in_specs=[pl.BlockSpec((B, D), lambda l, j: (0, 0)),
                  pl.BlockSpec((None, D, tn), lambda l, j: (l, 0, j)),
                  pl.BlockSpec((None, 1, tn), lambda l, j: (l, 0, j))],
        out_specs=pl.BlockSpec((None, B, tn), lambda l, j: (l, 0, j)),
        out_shape=jax.ShapeDtypeStruct((L, B, N), F32),
        compiler_params=_params("parallel", "parallel"),
        name="modulation",
    )(c, w_mod, b_mod.reshape(L, 1, N))


def _in_proj_kernel(x_ref, g_ref, sc_ref, sh_ref, w_ref, *rest, S, R, nj, blocks_per_out, n_out, out_scales,
                    fill_slots):
    outs, hs = rest[-2 - n_out:-2], rest[-2:]
    i, j = pl.program_id(0), pl.program_id(1)
    n_chunks = S * R // min(NORM_ROWS, S * R)
    per_step = next(d for d in range(1, n_chunks + 1) if n_chunks % d == 0 and d * (nj - 1) >= n_chunks)
    norm_steps = n_chunks // per_step

    def store(k, y):
        val = (y if out_scales[k] == 1.0 else y * out_scales[k]).astype(outs[k].dtype)
        if k > 0 and fill_slots:
            for s in range(outs[k].shape[0]):
                outs[k][s] = val
        else:
            outs[k][...] = val

    def project(parity):
        y = jnp.dot(hs[parity][...], w_ref[...], preferred_element_type=F32)
        if n_out == 1:
            store(0, y)
        else:
            for k in range(n_out):
                @pl.when(j // blocks_per_out == k)
                def _(k=k):
                    store(k, y)

    @pl.when((i == 0) & (j == 0))
    def _():
        _norm_mod_rows(x_ref, g_ref[...], sc_ref, sh_ref, hs[0], S, R)

    for parity in range(2):
        @pl.when((i % 2 == parity) & ((j == 0) | (j > norm_steps)))
        def _(parity=parity):
            project(parity)

        @pl.when((i % 2 == parity) & (j > 0) & (j <= norm_steps))
        def _(parity=parity):
            for t in range(per_step):
                _norm_mod_chunk(x_ref, g_ref[...], sc_ref, sh_ref, hs[1 - parity], S, R, (j - 1) * per_step + t)
            project(parity)


def _in_proj(x, n_seq, T, g, mod, layer, sc_chunk, sh_chunk, w, wi, out_dtypes, out_scales, stacks=None, slot=0,
             n_slots=1):
    N, D = x.shape
    n_total = w.shape[2]
    n_out = len(out_dtypes)
    S, R = _split_rows(n_seq, T, ROWS_IN_PROJ)
    rows = S * R
    nb = N // rows
    width = n_total // n_out
    tn = _largest_divisor(width, (COLS_IN_PROJ, 256, 128))
    bpo = width // tn
    nj = n_total // tn

    def norm_block(i, j):
        return jnp.where((i == 0) & (j == 0), 0, jnp.minimum(i + 1, nb - 1))

    def mod_spec(chunk):
        return pl.BlockSpec((None, S, 1, D), lambda i, j: (layer, _seq_block(norm_block(i, j), S, R, T), 0, chunk))

    def col(j, k):
        return jnp.clip(j - k * bpo, 0, bpo - 1)

    out_specs = [pl.BlockSpec((rows, tn), lambda i, j: (i, col(j, 0)))]
    out_shape = [jax.ShapeDtypeStruct((N, width), out_dtypes[0])]
    for k in range(1, n_out):
        if stacks is None:
            spec = pl.BlockSpec((n_slots, rows, tn), functools.partial(lambda i, j, k: (0, i, col(j, k)), k=k))
        else:
            spec = pl.BlockSpec((None, rows, tn), functools.partial(lambda i, j, k: (slot, i, col(j, k)), k=k))
        out_specs.append(spec)
        out_shape.append(jax.ShapeDtypeStruct((n_slots, N, width), out_dtypes[k]))
    in_specs = [pl.BlockSpec((rows, D), lambda i, j: (norm_block(i, j), 0)),
                pl.BlockSpec((1, D), lambda i, j: (0, 0)),
                mod_spec(sc_chunk), mod_spec(sh_chunk),
                pl.BlockSpec((None, D, tn), lambda i, j: (wi, 0, j))]
    args = [x, g.reshape(1, D), mod, mod, w]
    aliases = {}
    if stacks is not None:
        for k, st in enumerate(stacks):
            aliases[len(args)] = k + 1
            in_specs.append(pl.BlockSpec(memory_space=pl.ANY))
            args.append(st)
    return pl.pallas_call(
        functools.partial(_in_proj_kernel, S=S, R=R, nj=nj, blocks_per_out=bpo, n_out=n_out,
                          out_scales=tuple(out_scales), fill_slots=stacks is None),
        grid=(nb, nj),
        in_specs=in_specs, out_specs=out_specs, out_shape=out_shape,
        scratch_shapes=[pltpu.VMEM((rows, D), BF16), pltpu.VMEM((rows, D), BF16)],
        input_output_aliases=aliases,
        compiler_params=_params("arbitrary", "arbitrary"),
        name="in_proj",
    )(*args)


def _sgu_matrices(ws_ref, bs_ref, g, T, chunk, whole_chunks):
    w = ws_ref[g]
    b = jnp.broadcast_to(bs_ref[g], (chunk, chunk))
    ii = lax.broadcasted_iota(jnp.int32, (chunk, chunk), 0)
    jj = lax.broadcasted_iota(jnp.int32, (chunk, chunk), 1)
    if whole_chunks:
        return jnp.where(jj // GMLP_CAUSAL_BLOCK <= ii // GMLP_CAUSAL_BLOCK, w, 0.0), b
    w_tl = jnp.where((ii < T) & (jj < T), w, 0.0)
    b_tl = jnp.where(ii < T, b, 0.0)
    wm, bm = w_tl, b_tl
    for k in range(1, chunk // T):
        wm = wm + pltpu.roll(pltpu.roll(w_tl, k * T, 0), k * T, 1)
        bm = bm + pltpu.roll(b_tl, k * T, 0)
    return wm, bm


def _mix_even_kernel(*refs, S, R, T, chunk, groups, halo, emit_vn):
    refs = list(refs)
    xa_ref, gb_ref, gc_ref, u_ref, v_ref = refs[:5]
    del refs[:5]
    if halo:
        hxa_ref, hgc_ref = refs[:2]
        del refs[:2]
    st_ref, x_ref, g1_ref, wc_ref, gs_ref, ws_ref, bs_ref, wo_ref, gn_ref = refs[:9]
    del refs[:9]
    xo_ref, tail_ref = refs[:2]
    del refs[:2]
    if emit_vn:
        vn_out_ref = refs.pop(0)
    cat_ref, vn_ref = refs

    i = pl.program_id(0)
    rows = S * R
    C = xa_ref.shape[1]

    ca = gc_ref[...] * xa_ref[...]
    st = st_ref[...]
    p0, p1 = st[:, 0:1, :], st[:, 1:2, :]
    if halo:
        hca = hgc_ref[...] * hxa_ref[...]
        first = (i % (T // R)) == 0
        p0 = jnp.where(first, p0, hca[SUBLANES - 2:SUBLANES - 1][None])
        p1 = jnp.where(first, p1, hca[SUBLANES - 1:SUBLANES][None])
    conv = _causal_conv(ca, _per_row(p0, R), _per_row(p1, R), wc_ref[...], R, S)
    cat_ref[:, 0:C] = (gb_ref[...] * conv).astype(BF16)
    tail_ref[...] = ca.reshape(S, R, C)[:, R - SUBLANES:R, :]

    vn = _rms(v_ref[...], gs_ref[...])
    if emit_vn:
        vn_out_ref[...] = vn
    vn_ref[...] = vn.astype(BF16)
    gd = v_ref.shape[1] // groups
    for g in range(groups):
        wm, bm = _sgu_matrices(ws_ref, bs_ref, g, T, chunk, whole_chunks=(R % chunk == 0))
        wm = wm.astype(BF16)
        cs = slice(g * gd, (g + 1) * gd)
        for c in range(rows // chunk):
            rs = slice(c * chunk, (c + 1) * chunk)
            mixed = jnp.dot(wm, vn_ref[rs, cs], preferred_element_type=F32) + bm
            cat_ref[rs, C + g * gd:C + (g + 1) * gd] = (u_ref[rs, cs] * mixed).astype(BF16)

    y = jnp.dot(cat_ref[...], wo_ref[...], preferred_element_type=F32)
    xo_ref[...] = x_ref[...] + _per_row(g1_ref[...], R) * _rms(y, gn_ref[...])


def _mix_even(z, x, n_seq, T, state, mod, layer, wi, w_conv, g_sgu, w_sgu, b_sgu, w_out, g_norm, emit_vn):
    N, D = x.shape
    C = w_conv.shape[2]
    groups, chunk = w_sgu.shape[1], w_sgu.shape[2]
    Cb = g_sgu.shape[1]
    S, R = _split_rows(n_seq, T, ROWS_MIX // 2 if emit_vn else ROWS_MIX)
    rows = S * R
    halo = R < T
    assert rows % chunk == 0 and (R % chunk == 0 or (chunk % T == 0 and T <= GMLP_CAUSAL_BLOCK))
    assert z.shape[1] == 3 * C + 2 * Cb and C == Cb

    in_specs = [pl.BlockSpec((rows, C), functools.partial(lambda i, k: (i, k), k=k)) for k in range(5)]
    args = [z] * 5
    if halo:
        per = R // SUBLANES
        in_specs += [pl.BlockSpec((SUBLANES, C), functools.partial(
            lambda i, k: (jnp.maximum(i * per - 1, 0), k), k=k)) for k in (0, 2)]
        args += [z, z]
    in_specs += [
        pl.BlockSpec((None, S, CONV_TAPS - 1, C), lambda i: (wi, _seq_block(i, S, R, T), 0, 0)),
        pl.BlockSpec((rows, D), lambda i: (i, 0)),
        pl.BlockSpec((None, S, 1, D), lambda i: (layer, _seq_block(i, S, R, T), 0, 2)),
        pl.BlockSpec((None, CONV_TAPS, C), lambda i: (wi, 0, 0)),
        pl.BlockSpec((None, 1, Cb), lambda i: (wi, 0, 0)),
        pl.BlockSpec((None, groups, chunk, chunk), lambda i: (wi, 0, 0, 0)),
        pl.BlockSpec((None, groups, chunk, 1), lambda i: (wi, 0, 0, 0)),
        pl.BlockSpec((None, C + Cb, D), lambda i: (wi, 0, 0), pipeline_mode=pl.Buffered(1)),
        pl.BlockSpec((1, D), lambda i: (0, 0)),
    ]
    n_even = w_conv.shape[0]
    args += [state, x, mod, w_conv, g_sgu.reshape(n_even, 1, Cb), w_sgu, b_sgu.reshape(n_even, groups, chunk, 1),
             w_out, g_norm.reshape(1, D)]
    out_specs = [pl.BlockSpec((rows, D), lambda i: (i, 0)),
                 pl.BlockSpec((S, SUBLANES, C), lambda i: (i, 0, 0))]
    out_shape = [jax.ShapeDtypeStruct((N, D), F32),
                 jax.ShapeDtypeStruct((N // rows * S, SUBLANES, C), F32)]
    if emit_vn:
        out_specs.append(pl.BlockSpec((rows, Cb), lambda i: (i, 0)))
        out_shape.append(jax.ShapeDtypeStruct((N, Cb), F32))
    return pl.pallas_call(
        functools.partial(_mix_even_kernel, S=S, R=R, T=T, chunk=chunk, groups=groups, halo=halo, emit_vn=emit_vn),
        grid=(N // rows,),
        in_specs=in_specs, out_specs=out_specs, out_shape=out_shape,
        scratch_shapes=[pltpu.VMEM((rows, C + Cb), BF16), pltpu.VMEM((rows, Cb), BF16)],
        compiler_params=_params("parallel"),
        name="mix_even",
    )(*args)


def _upper_inclusive(n):
    r = lax.broadcasted_iota(jnp.int32, (n, n), 0)
    c = lax.broadcasted_iota(jnp.int32, (n, n), 1)
    return jnp.where(r >= c, 1.0, 0.0).astype(BF16)


def _strictly_earlier(n):
    t = lax.broadcasted_iota(jnp.int32, (n, n), 0)
    s = lax.broadcasted_iota(jnp.int32, (n, n), 1)
    return s < t


def _sb_blocks(qs, ks, vs, u_inc, cs, accs, mask):
    tq = qs[0].shape[0]
    zs, parts = [], []
    for q, k in zip(qs, ks):
        z = jnp.minimum(lax.dot_general(q, k, (((1,), (1,)), ((), ())), preferred_element_type=F32), SB_Z_MAX)
        fail = jnp.log(1.0 + jnp.exp2(z))
        if mask is not None:
            fail = jnp.where(mask, fail, 0.0)
        hi = fail.astype(BF16)
        parts += [hi, (fail - hi.astype(F32)).astype(BF16)]
        zs.append(z)
    sums = jnp.dot(jnp.concatenate(parts, axis=0), u_inc, preferred_element_type=F32)
    cs_out, accs_out = [], []
    for h, (z, v, c, acc) in enumerate(zip(zs, vs, cs, accs)):
        incl = sums[2 * h * tq:(2 * h + 1) * tq] + sums[(2 * h + 1) * tq:(2 * h + 2) * tq]
        w = jnp.exp2(z - (incl + c) * LOG2E)
        if mask is not None:
            w = jnp.where(mask, w, 0.0)
        accs_out.append(acc + jnp.dot(w.astype(BF16), v, preferred_element_type=F32))
        cs_out.append(c + incl[:, 0:1])
    return tuple(cs_out), tuple(accs_out)


def _unfinished(cs):
    return (jnp.min(functools.reduce(jnp.minimum, cs)) < SB_DONE).astype(jnp.int32)


def _attn_prompt_kernel(q_ref, k_ref, v_ref, o_ref, kb_ref, vb_ref, u_ref, *, tq, heads, dh):
    qi = pl.program_id(2)

    @pl.when(qi == 0)
    def _():
        kb_ref[...] = k_ref[...].astype(BF16)
        vb_ref[...] = v_ref[...].astype(BF16)
        u_ref[...] = _upper_inclusive(tq)

    u = u_ref[...]
    cols = [slice(h * dh, (h + 1) * dh) for h in range(heads)]
    qs = [q_ref[:, cs] for cs in cols]

    def sweep(n, cs, accs, mask):
        rows = pl.ds(pl.multiple_of((qi - n) * tq, tq), tq)
        return _sb_blocks(qs, [kb_ref[rows, cs_] for cs_ in cols], [vb_ref[rows, cs_] for cs_ in cols], u, cs, accs,
                          mask)

    def diagonal():
        return sweep(0, (jnp.zeros((tq, 1), F32),) * heads, (jnp.zeros((tq, dh), F32),) * heads,
                     _strictly_earlier(tq))

    def store(accs):
        for h in range(heads):
            o_ref[:, cols[h]] = accs[h].astype(o_ref.dtype)

    def body(carry):
        n, _, cs, accs = carry
        cs, accs = sweep(n, cs, accs, None)
        return n + 1, _unfinished(cs), cs, accs

    @pl.when(qi == 0)
    def _():
        store(diagonal()[1])

    @pl.when(qi > 0)
    def _():
        cs, accs = sweep(1, *diagonal(), None)
        _, _, _, accs = lax.while_loop(lambda carry: (carry[0] <= qi) & (carry[1] > 0), body,
                                       (jnp.int32(2), _unfinished(cs), cs, accs))
        store(accs)


def _attn_prompt(q, k_stack, v_stack, slot, n_seq, T, heads):
    N, D = q.shape
    dh = D // heads
    tq = min(ATTN_BLOCK, T)
    hp = _largest_divisor(heads, (ATTN_PROMPT_HEADS, 1))
    assert T % tq == 0
    n_slots = k_stack.shape[0]
    kv_spec = pl.BlockSpec((None, None, T, hp * dh), lambda b, h, i: (slot, b, 0, h))
    o = pl.pallas_call(
        functools.partial(_attn_prompt_kernel, tq=tq, heads=hp, dh=dh),
        grid=(n_seq, heads // hp, T // tq),
        in_specs=[pl.BlockSpec((None, tq, hp * dh), lambda b, h, i: (b, i, h)), kv_spec, kv_spec],
        out_specs=pl.BlockSpec((None, tq, hp * dh), lambda b, h, i: (b, i, h)),
        out_shape=jax.ShapeDtypeStruct((n_seq, T, D), BF16),
        scratch_shapes=[pltpu.VMEM((T, hp * dh), BF16), pltpu.VMEM((T, hp * dh), BF16),
                        pltpu.VMEM((tq, tq), BF16)],
        compiler_params=_params("parallel", "parallel", "arbitrary"),
        name="attn_prompt",
    )(q.reshape(n_seq, T, D), k_stack.reshape(n_slots, n_seq, T, D), v_stack.reshape(n_slots, n_seq, T, D))
    return o.reshape(N, D)


def _attn_sample_kernel(q_ref, kn_ref, vn_ref, ck_last_ref, cv_last_ref, ck_hbm, cv_hbm, o_ref, kbuf, vbuf, sem,
                        *, T, P, tk, heads, all_heads, dh, slot):
    u_past = _upper_inclusive(tk)
    cols = [slice(h * dh, (h + 1) * dh) for h in range(heads)]
    qs = [q_ref[:, cs] for cs in cols]

    cs, accs = _sb_blocks(qs, [kn_ref[:, c].astype(BF16) for c in cols], [vn_ref[:, c].astype(BF16) for c in cols],
                          _upper_inclusive(T), (jnp.zeros((T, 1), F32),) * heads,
                          (jnp.zeros((T, dh), F32),) * heads, _strictly_earlier(T))

    b, h0 = pl.program_id(0), pl.program_id(1) * heads

    def head_rows(ref):
        return [ref[pl.ds(h0 + h, tk, stride=all_heads), :].astype(BF16) for h in range(heads)]

    cs, accs = _sb_blocks(qs, head_rows(ck_last_ref), head_rows(cv_last_ref), u_past, cs, accs, None)

    def body(carry):
        n, _, cs, accs = carry
        rows = pl.ds(pl.multiple_of((P - (n + 1) * tk) * all_heads, tk * all_heads), tk * all_heads)
        copies = [pltpu.make_async_copy(src.at[slot, b, rows, :], dst, sem.at[s])
                  for s, (src, dst) in enumerate(((ck_hbm, kbuf), (cv_hbm, vbuf)))]
        for c in copies:
            c.start()
        for c in copies:
            c.wait()
        cs, accs = _sb_blocks(qs, head_rows(kbuf), head_rows(vbuf), u_past, cs, accs, None)
        return n + 1, _unfinished(cs), cs, accs

    _, _, _, accs = lax.while_loop(lambda carry: (carry[0] < P // tk) & (carry[1] > 0), body,
                                   (jnp.int32(1), _unfinished(cs), cs, accs))
    for h in range(heads):
        o_ref[:, cols[h]] = accs[h].astype(o_ref.dtype)


def _attn_sample(q, k_stack, v_stack, slot, cache_k, cache_v, n_seq, T, heads):
    N, D = q.shape
    dh = D // heads
    P = cache_k.shape[2]
    tk = _largest_divisor(P, (ATTN_BLOCK, 128))
    hg = _largest_divisor(heads, (ATTN_SAMPLE_HEADS, heads))
    n_slots = k_stack.shape[0]
    q_spec = pl.BlockSpec((None, T, hg * dh), lambda b, g: (b, 0, g))
    new_spec = pl.BlockSpec((None, None, T, hg * dh), lambda b, g: (slot, b, 0, g))
    last_spec = pl.BlockSpec((None, None, tk * heads, dh), lambda b, g: (slot, b, P // tk - 1, 0))
    hbm_spec = pl.BlockSpec(memory_space=pl.ANY)
    past_shape = cache_k.shape[:2] + (P * heads, dh)
    cache_k, cache_v = cache_k.reshape(past_shape), cache_v.reshape(past_shape)
    o = pl.pallas_call(
        functools.partial(_attn_sample_kernel, T=T, P=P, tk=tk, heads=hg, all_heads=heads, dh=dh, slot=slot),
        grid=(n_seq, heads // hg),
        in_specs=[q_spec, new_spec, new_spec, last_spec, last_spec, hbm_spec, hbm_spec],
        out_specs=q_spec,
        out_shape=jax.ShapeDtypeStruct((n_seq, T, D), BF16),
        scratch_shapes=[pltpu.VMEM((tk * heads, dh), F32), pltpu.VMEM((tk * heads, dh), F32),
                        pltpu.SemaphoreType.DMA((2,))],
        compiler_params=_params("parallel", "arbitrary"),
        name="attn_sample",
    )(q.reshape(n_seq, T, D), k_stack.reshape(n_slots, n_seq, T, D), v_stack.reshape(n_slots, n_seq, T, D),
      cache_k, cache_v, cache_k, cache_v)
    return o.reshape(N, D)


def _out_proj_kernel(o_ref, w_ref, x_ref, g1_ref, gn_ref, xo_ref, *, R):
    y = jnp.dot(o_ref[...], w_ref[...], preferred_element_type=F32)
    xo_ref[...] = x_ref[...] + _per_row(g1_ref[...], R) * _rms(y, gn_ref[...])


def _out_proj(o, w, wi, x, n_seq, T, mod, layer, g_norm):
    N, D = x.shape
    S, R = _split_rows(n_seq, T, ROWS_OUT_PROJ)
    rows = S * R
    return pl.pallas_call(
        functools.partial(_out_proj_kernel, R=R),
        grid=(N // rows,),
        in_specs=[pl.BlockSpec((rows, D), lambda i: (i, 0)),
                  pl.BlockSpec((None, D, D), lambda i: (wi, 0, 0)),
                  pl.BlockSpec((rows, D), lambda i: (i, 0)),
                  pl.BlockSpec((None, S, 1, D), lambda i: (layer, _seq_block(i, S, R, T), 0, 2)),
                  pl.BlockSpec((1, D), lambda i: (0, 0))],
        out_specs=pl.BlockSpec((rows, D), lambda i: (i, 0)),
        out_shape=jax.ShapeDtypeStruct((N, D), F32),
        compiler_params=_params("parallel"),
        name="out_proj",
    )(o, w, x, mod, g_norm.reshape(1, D))


def _ffn_kernel(x_ref, xp_ref, gn_in_ref, sc_ref, sh_ref, wa_ref, wg_ref, wc_ref, bc_ref, wd_ref, st_ref, g2_ref,
                gn_out_ref, xo_ref, tail_ref, h_ref, acc_ref, act0_ref, act1_ref, carry_ref, *, S, R, T, nf, nb):
    i, j = pl.program_id(0), pl.program_id(1)
    acts = (act0_ref, act1_ref)
    last_act = acts[(nf - 1) % 2]

    def up_gate(dst_ref):
        h = h_ref[...]
        a = jnp.dot(h, wa_ref[...], preferred_element_type=F32)
        gate = jnp.dot(h, wg_ref[...], preferred_element_type=F32)
        tail = a.reshape(S, R, -1)[:, R - SUBLANES:R, :]
        tail_ref[...] = tail
        p0, p1 = st_ref[:, 0:1, :], st_ref[:, 1:2, :]
        if R < T:
            first = (i % (T // R)) == 0
            prev = carry_ref[j]
            p0 = jnp.where(first, p0, prev[SUBLANES - 2:SUBLANES - 1][None])
            p1 = jnp.where(first, p1, prev[SUBLANES - 1:SUBLANES][None])
            carry_ref[j] = tail[0]
        half = 0.5 * (_causal_conv(a, _per_row(p0, R), _per_row(p1, R), wc_ref[...], R, S) + bc_ref[...])
        dst_ref[...] = ((half + half * jnp.tanh(half)) * gate).astype(BF16)

    def down(src_ref):
        acc_ref[...] += jnp.dot(src_ref[...], wd_ref[...], preferred_element_type=F32)

    def finish_previous():
        down(last_act)
        _residual_norm_rows(xp_ref, acc_ref, g2_ref, gn_out_ref[...], xo_ref, S, R, straight_line=True)

    def start_block():
        _norm_mod_rows(x_ref, gn_in_ref[...], sc_ref, sh_ref, h_ref, S, R, straight_line=True)
        acc_ref[...] = jnp.zeros_like(acc_ref)
        up_gate(acts[0])

    @pl.when((j == 0) & (i == 0))
    def _():
        if R < T:
            carry_ref[...] = jnp.zeros_like(carry_ref)
        start_block()

    @pl.when((j == 0) & (i > 0) & (i < nb))
    def _():
        finish_previous()
        start_block()

    @pl.when((j == 0) & (i == nb))
    def _():
        finish_previous()

    for parity in range(2):
        @pl.when((j > 0) & (i < nb) & (j % 2 == parity))
        def _(parity=parity):
            up_gate(acts[parity])
            down(acts[1 - parity])


def _ffn(x, n_seq, T, state, mod, layer, g_in, w_up, w_conv, b_conv, w_down, g_out):
    N, D = x.shape
    L, F = w_down.shape[0], w_down.shape[1]
    S, R = _split_rows(n_seq, T, ROWS_FFN)
    rows = S * R
    tf = _largest_divisor(F, (COLS_FFN, 256, 128))
    nf = F // tf

    nb = N // rows

    def cur(i):
        return jnp.minimum(i, nb - 1)

    def prev(i):
        return jnp.maximum(i - 1, 0)

    def up_chunk(i, j):
        return jnp.where(i < nb, j, nf - 1)

    def down_chunk(i, j):
        return jnp.where((i == nb) & (j > 0), nf - 1, (j + nf - 1) % nf)

    def mod_spec(chunk, blk):
        return pl.BlockSpec((None, S, 1, D), lambda i, j: (layer, _seq_block(blk(i), S, R, T), 0, chunk))

    return pl.pallas_call(
        functools.partial(_ffn_kernel, S=S, R=R, T=T, nf=nf, nb=nb),
        grid=(nb + 1, nf),
        in_specs=[pl.BlockSpec((rows, D), lambda i, j: (cur(i), 0)),
                  pl.BlockSpec((rows, D), lambda i, j: (prev(i), 0)),
                  pl.BlockSpec((1, D), lambda i, j: (0, 0)),
                  mod_spec(4, cur), mod_spec(3, cur),
                  pl.BlockSpec((None, D, tf), lambda i, j: (layer, 0, up_chunk(i, j))),
                  pl.BlockSpec((None, D, tf), lambda i, j: (layer, 0, nf + up_chunk(i, j))),
                  pl.BlockSpec((None, CONV_TAPS, tf), lambda i, j: (layer, 0, up_chunk(i, j))),
                  pl.BlockSpec((None, 1, tf), lambda i, j: (layer, 0, up_chunk(i, j))),
                  pl.BlockSpec((None, tf, D), lambda i, j: (layer, down_chunk(i, j), 0)),
                  pl.BlockSpec((None, S, CONV_TAPS - 1, tf),
                               lambda i, j: (layer, _seq_block(cur(i), S, R, T), 0, up_chunk(i, j))),
                  mod_spec(5, prev),
                  pl.BlockSpec((1, D), lambda i, j: (0, 0))],
        out_specs=[pl.BlockSpec((rows, D), lambda i, j: (prev(i), 0)),
                   pl.BlockSpec((S, SUBLANES, tf), lambda i, j: (cur(i), 0, up_chunk(i, j)))],
        out_shape=[jax.ShapeDtypeStruct((N, D), F32),
                   jax.ShapeDtypeStruct((N // rows * S, SUBLANES, F), F32)],
        scratch_shapes=[pltpu.VMEM((rows, D), BF16), pltpu.VMEM((rows, D), F32),
                        pltpu.VMEM((rows, tf), BF16), pltpu.VMEM((rows, tf), BF16),
                        pltpu.VMEM((nf, SUBLANES, tf), F32)],
        compiler_params=_params("arbitrary", "arbitrary"),
        name="ffn",
    )(x, x, g_in.reshape(1, D), mod, mod, w_up, w_up, w_conv, b_conv.reshape(L, 1, F), w_down, state, mod,
      g_out.reshape(1, D))


def _last_rows(tails, n_seq):
    t = tails.reshape(n_seq, tails.shape[0] // n_seq, SUBLANES, tails.shape[2])
    return t[:, -1, SUBLANES - (CONV_TAPS - 1):, :]


def _trunk(x, mod, cache_k, cache_v, conv_a_prev, ffn_prev, norm_g, w_in_ab, w_conv_a, g_sgu, w_sgu, b_sgu,
           w_out_ab, w_qkv_sb, w_o_sb, w_ffn_up, w_ffn_conv, b_ffn_conv, w_ffn_down, heads, emit_vn):
    n_seq, T, D = x.shape
    x = x.reshape(n_seq * T, D)
    depth = norm_g.shape[0]
    n_odd = w_qkv_sb.shape[0]
    q_scale = (D // heads) ** -0.5 * LOG2E
    kv = None
    conv_a_new, ffn_new, sgu_v_new = [], [], []
    for l in range(depth):
        i = l // 2
        if l % 2 == 0:
            (z,) = _in_proj(x, n_seq, T, norm_g[l, 0], mod, l, 1, 0, w_in_ab, i, (F32,), (1.0,))
            res = _mix_even(z, x, n_seq, T, conv_a_prev, mod, l, i, w_conv_a, g_sgu, w_sgu, b_sgu, w_out_ab,
                            norm_g[l, 1], emit_vn)
            x = res[0]
            conv_a_new.append(_last_rows(res[1], n_seq))
            if emit_vn:
                sgu_v_new.append(res[2].reshape(n_seq, T, -1))
        else:
            q, *kv = _in_proj(x, n_seq, T, norm_g[l, 0], mod, l, 1, 0, w_qkv_sb, i, (BF16, F32, F32),
                              (q_scale, 1.0, 1.0), stacks=kv, slot=i, n_slots=n_odd)
            if cache_k is None:
                o = _attn_prompt(q, kv[0], kv[1], i, n_seq, T, heads)
            else:
                o = _attn_sample(q, kv[0], kv[1], i, cache_k, cache_v, n_seq, T, heads)
            x = _out_proj(o, w_o_sb, i, x, n_seq, T, mod, l, norm_g[l, 1])
        x, tails = _ffn(x, n_seq, T, ffn_prev, mod, l, norm_g[l, 2], w_ffn_up, w_ffn_conv, b_ffn_conv, w_ffn_down,
                        norm_g[l, 3])
        ffn_new.append(_last_rows(tails, n_seq))
    k_new, v_new = (a.reshape(n_odd, n_seq, T, heads, D // heads) for a in kv)
    return x.reshape(n_seq, T, D), k_new, v_new, jnp.stack(conv_a_new), jnp.stack(ffn_new), sgu_v_new


def kernel(x_prompt, x_sample, c_prompt, c_sample, cache_sb_k, cache_sb_v, state_conv_a, state_ffn_conv, w_mod, b_mod,
           norm_g, w_in_ab, w_conv_a, g_sgu, w_sgu, b_sgu, w_out_ab, w_qkv_sb, w_o_sb, w_ffn_up, w_ffn_conv,
           b_ffn_conv, w_ffn_down):
    bp, bs = x_prompt.shape[0], x_sample.shape[0]
    depth = norm_g.shape[0]
    heads = cache_sb_k.shape[3]

    mod = _modulation(jnp.concatenate([c_prompt, c_sample], axis=0), w_mod, b_mod)
    mod_p = mod[:, :bp].reshape(depth, bp, 1, -1)
    mod_s = mod[:, bp:].reshape(depth, bs, 1, -1)

    weights = (norm_g, w_in_ab.astype(BF16), w_conv_a, g_sgu, w_sgu, b_sgu, w_out_ab.astype(BF16),
               w_qkv_sb.astype(BF16), w_o_sb.astype(BF16), w_ffn_up.astype(BF16), w_ffn_conv, b_ffn_conv,
               w_ffn_down.astype(BF16))

    zeros_a = jnp.zeros((state_conv_a.shape[0], bp) + state_conv_a.shape[2:], F32)
    zeros_f = jnp.zeros((depth, bp) + state_ffn_conv.shape[2:], F32)
    y_s, k_s, v_s, conv_a_s, ffn_s, sgu_v = _trunk(x_sample, mod_s, cache_sb_k, cache_sb_v, state_conv_a,
                                                   state_ffn_conv, *weights, heads=heads, emit_vn=True)
    sample_out, (x_prompt, mod_p) = lax.optimization_barrier(
        ((y_s, k_s, v_s, conv_a_s, ffn_s, sgu_v), (x_prompt, mod_p)))
    y_s, k_s, v_s, conv_a_s, ffn_s, sgu_v = sample_out
    y_p, k_p, v_p, conv_a_p, ffn_p, _ = _trunk(x_prompt, mod_p, None, None, zeros_a, zeros_f, *weights,
                                               heads=heads, emit_vn=False)
    return (y_p, y_s, k_p, v_p, conv_a_p, ffn_p, k_s, v_s, conv_a_s, ffn_s, jnp.stack(sgu_v))
```
